```python
import jax
import jax.numpy as jnp
from jax import lax
import numpy as np

D_MODEL = 2048
BATCH = 2
SEQ = 4096
DEPTH = 4
DEC_BATCH = 16
DEC_SEQ = 32
PAST_LEN = 1024

CHUNK = 64
N_MIXERS = 3
N_A = (DEPTH + 2) // 3
N_B = (DEPTH + 1) // 3
N_C = DEPTH // 3
D_FF = 4 * D_MODEL
H_A = 16
DH_A = D_MODEL // H_A
LEFT_CHUNKS = 8
REL_CLIP = 128
H_B = 16
KV_B = 4
DH_B = D_MODEL // H_B
H_IDX = 16
D_IDX = 64
TOPK_MAX = 256
Q_BLOCK = 128
B_SIZES = (H_B * DH_B, KV_B * DH_B, KV_B * DH_B, H_IDX * D_IDX, D_IDX, H_IDX)
B_IN_DIM = sum(B_SIZES)
HK_C = 16
HV_C = 32
DK_C = 128
DV_C = 128
CONV_W = 4
C_QK_DIM = HK_C * DK_C
C_V_DIM = HV_C * DV_C
C_CONV_DIM = 2 * C_QK_DIM + C_V_DIM
C_IN_DIM = C_CONV_DIM + C_V_DIM + 2 * HV_C
ALPHA = (2.0 * DEPTH) ** 0.25
BETA_INIT = (8.0 * DEPTH) ** -0.25
LN_EPS = 1e-5
RMS_EPS = 1e-6

kernel_name = 'hybrid_streaming_encoder_step'


def _split(y, sizes):
    out, off = [], 0
    for s in sizes:
        out.append(y[..., off:off + s])
        off += s
    return out


def _layer_norm(x, g, b):
    xf = x.astype(jnp.float32)
    mu = jnp.mean(xf, axis=-1, keepdims=True)
    var = jnp.mean(jnp.square(xf - mu), axis=-1, keepdims=True)
    return ((xf - mu) * lax.rsqrt(var + LN_EPS)).astype(x.dtype) * g + b


def _sq_relu_mlp(x, w1, w2):
    h = jax.nn.relu(x @ w1)
    return (h * h) @ w2


def _band_attention(q, k, v, q_pos, k_pos, rel_bias):
    s = jnp.einsum('bqhd,bkhd->bhqk', q, k, preferred_element_type=jnp.float32) * (DH_A ** -0.5)
    rel = jnp.clip(q_pos[:, None] - k_pos[None, :], -REL_CLIP, REL_CLIP) + REL_CLIP
    s = s + rel_bias[:, rel].astype(jnp.float32)[None]
    q_chunk = q_pos // CHUNK
    k_chunk = k_pos // CHUNK
    visible = ((k_pos[None, :] >= 0)
               & (k_chunk[None, :] <= q_chunk[:, None])
               & (k_chunk[None, :] >= q_chunk[:, None] - LEFT_CHUNKS))
    s = jnp.where(visible[None, None], s, -jnp.inf)
    p = jax.nn.softmax(s, axis=-1)
    return jnp.einsum('bhqk,bkhd->bqhd', p.astype(v.dtype), v)


def _a_project(x, w_in):
    B, T, _ = x.shape
    q, k, v = _split(x @ w_in, (D_MODEL, D_MODEL, D_MODEL))
    return (q.reshape(B, T, H_A, DH_A), k.reshape(B, T, H_A, DH_A), v.reshape(B, T, H_A, DH_A))


def _mixer_a_prompt(x, w_in, rel_bias, w_out):
    B, T, _ = x.shape
    q, k, v = _a_project(x, w_in)
    pad = LEFT_CHUNKS * CHUNK
    band = pad + CHUNK
    kp = jnp.pad(k, ((0, 0), (pad, 0), (0, 0), (0, 0)))
    vp = jnp.pad(v, ((0, 0), (pad, 0), (0, 0), (0, 0)))

    def one_chunk(c):
        start = c * CHUNK
        q_c = lax.dynamic_slice_in_dim(q, start, CHUNK, axis=1)
        k_c = lax.dynamic_slice_in_dim(kp, start, band, axis=1)
        v_c = lax.dynamic_slice_in_dim(vp, start, band, axis=1)
        q_pos = start + jnp.arange(CHUNK)
        k_pos = start - pad + jnp.arange(band)
        return _band_attention(q_c, k_c, v_c, q_pos, k_pos, rel_bias)

    o = lax.map(one_chunk, jnp.arange(T // CHUNK))
    o = jnp.moveaxis(o, 0, 1).reshape(B, T, D_MODEL)
    keep = min(pad, T)
    return o @ w_out, k[:, T - keep:], v[:, T - keep:]


def _mixer_a_sample(x, k_cache, v_cache, w_in, rel_bias, w_out):
    B, S, _ = x.shape
    past = k_cache.shape[1]
    q, k, v = _a_project(x, w_in)
    kk = jnp.concatenate([k_cache, k], axis=1)
    vv = jnp.concatenate([v_cache, v], axis=1)
    q_pos = PAST_LEN + jnp.arange(S)
    k_pos = PAST_LEN - past + jnp.arange(past + S)
    o = _band_attention(q, kk, vv, q_pos, k_pos, rel_bias)
    return o.reshape(B, S, D_MODEL) @ w_out, k, v


def _b_project(x, w_in):
    B, T, _ = x.shape
    q, k, v, qi, ki, wi = _split(x @ w_in, B_SIZES)
    return (q.reshape(B, T, H_B, DH_B), k.reshape(B, T, KV_B, DH_B), v.reshape(B, T, KV_B, DH_B),
            qi.reshape(B, T, H_IDX, D_IDX), ki, wi)


def _dsa_attend(q, qi, wi, q_pos, k_all, v_all, ki_all, n_sel):
    B, Tq = q.shape[0], q.shape[1]
    L = k_all.shape[1]
    k_pos = jnp.arange(L)
    logits = jnp.einsum('bqhd,bsd->bqhs', qi, ki_all, preferred_element_type=jnp.float32) * (D_IDX ** -0.5)
    score = jnp.einsum('bqh,bqhs->bqs', wi.astype(jnp.float32) * (H_IDX ** -0.5), jax.nn.relu(logits))
    admissible = (k_pos[None, :] // CHUNK) <= (q_pos[:, None] // CHUNK)
    score = jnp.where(admissible[None], score, -jnp.inf)
    top_val, top_idx = lax.top_k(score, n_sel)
    sel_ok = jnp.isfinite(top_val)
    gather = jax.vmap(lambda rows, idx: rows[idx])
    k_sel = gather(k_all, top_idx)
    v_sel = gather(v_all, top_idx)
    qg = q.reshape(B, Tq, KV_B, H_B // KV_B, DH_B)
    s = jnp.einsum('bqhgd,bqshd->bqhgs', qg, k_sel, preferred_element_type=jnp.float32) * (DH_B ** -0.5)
    s = jnp.where(sel_ok[:, :, None, None, :], s, -jnp.inf)
    p = jax.nn.softmax(s, axis=-1)
    o = jnp.einsum('bqhgs,bqshd->bqhgd', p.astype(v_sel.dtype), v_sel)
    return o.reshape(B, Tq, H_B * DH_B)


def _mixer_b_prompt(x, w_in, w_out):
    B, T, _ = x.shape
    q, k, v, qi, ki, wi = _b_project(x, w_in)
    n_sel = min(TOPK_MAX, T // 4)

    def one_block(c):
        start = c * Q_BLOCK
        sl = lambda a: lax.dynamic_slice_in_dim(a, start, Q_BLOCK, axis=1)
        return _dsa_attend(sl(q), sl(qi), sl(wi), start + jnp.arange(Q_BLOCK), k, v, ki, n_sel)

    o = lax.map(one_block, jnp.arange(T // Q_BLOCK))
    o = jnp.moveaxis(o, 0, 1).reshape(B, T, D_MODEL)
    return o @ w_out, k, v, ki


def _mixer_b_sample(x, k_cache, v_cache, ki_cache, w_in, w_out):
    B, S, _ = x.shape
    q, k, v, qi, ki, wi = _b_project(x, w_in)
    kk = jnp.concatenate([k_cache, k], axis=1)
    vv = jnp.concatenate([v_cache, v], axis=1)
    kki = jnp.concatenate([ki_cache, ki], axis=1)
    n_sel = min(TOPK_MAX, kk.shape[1] // 4)
    o = _dsa_attend(q, qi, wi, PAST_LEN + jnp.arange(S), kk, vv, kki, n_sel)
    return o @ w_out, k, v, ki


def _causal_conv_silu(u, buf, w):
    T = u.shape[1]
    ext = jnp.concatenate([buf, u], axis=1)
    out = ext[:, 0:T] * w[0]
    for j in range(1, CONV_W):
        out = out + ext[:, j:j + T] * w[j]
    return jax.nn.silu(out), ext[:, ext.shape[1] - (CONV_W - 1):]


def _l2norm(a):
    af = a.astype(jnp.float32)
    return af * lax.rsqrt(jnp.sum(af * af, axis=-1, keepdims=True) + RMS_EPS)


def _gated_delta_rule(q, k, v, g, beta, s0):
    B, T, H, DK = q.shape
    DV = v.shape[-1]
    C = min(CHUNK, T)
    n = T // C

    def chunks(a):
        return jnp.moveaxis(a.reshape((B, n, C) + a.shape[2:]), 1, 0)

    incl = jnp.tril(jnp.ones((C, C), dtype=bool))
    strict = jnp.tril(jnp.ones((C, C), dtype=bool), k=-1)

    def step(S, blk):
        qc, kc, vc, gc, bc = blk
        qh = qc.transpose(0, 2, 1, 3)
        kh = kc.transpose(0, 2, 1, 3)
        vh = vc.transpose(0, 2, 1, 3)
        G = jnp.cumsum(gc.transpose(0, 2, 1), axis=-1)
        bh = bc.transpose(0, 2, 1)[..., None]
        diff = G[..., :, None] - G[..., None, :]
        d_incl = jnp.exp(jnp.where(incl, diff, -jnp.inf))
        d_strict = jnp.where(strict, d_incl, 0.0)
        kb = kh * bh
        m = jnp.einsum('bhid,bhjd->bhij', kb, kh) * d_strict
        rhs = jnp.concatenate([vh * bh, kb * jnp.exp(G)[..., None]], axis=-1)
        sol = lax.linalg.triangular_solve(m + jnp.eye(C, dtype=m.dtype), rhs,
                                          left_side=True, lower=True, unit_diagonal=True)
        u = sol[..., :DV] - sol[..., DV:] @ S
        o = (qh * jnp.exp(G)[..., None]) @ S + (jnp.einsum('bhid,bhjd->bhij', qh, kh) * d_incl) @ u
        g_last = G[..., -1:]
        S = S * jnp.exp(g_last)[..., None] + jnp.einsum(
            'bhcd,bhce->bhde', kh * jnp.exp(g_last - G)[..., None], u)
        return S, o.transpose(0, 2, 1, 3)

    S, o = lax.scan(step, s0, (chunks(q), chunks(k), chunks(v), chunks(g), chunks(beta)))
    return jnp.moveaxis(o, 0, 1).reshape(B, T, H, DV), S


def _mixer_c(x, conv_buf, s0, w_in, conv_w, a_log, dt_bias, norm_w, w_out):
    B, T, _ = x.shape
    qkv, z, b_raw, a_raw = _split(x @ w_in, (C_CONV_DIM, C_V_DIM, HV_C, HV_C))
    qkv, new_buf = _causal_conv_silu(qkv, conv_buf, conv_w)
    q, k, v = _split(qkv, (C_QK_DIM, C_QK_DIM, C_V_DIM))
    rep = HV_C // HK_C
    q = jnp.repeat(_l2norm(q.reshape(B, T, HK_C, DK_C)) * (DK_C ** -0.5), rep, axis=2)
    k = jnp.repeat(_l2norm(k.reshape(B, T, HK_C, DK_C)), rep, axis=2)
    v = v.reshape(B, T, HV_C, DV_C).astype(jnp.float32)
    beta = jax.nn.sigmoid(b_raw.astype(jnp.float32))
    g = -jnp.exp(a_log.astype(jnp.float32)) * jax.nn.softplus(a_raw.astype(jnp.float32) + dt_bias.astype(jnp.float32))
    o, s_new = _gated_delta_rule(q, k, v, g, beta, s0.astype(jnp.float32))
    o = o * lax.rsqrt(jnp.mean(o * o, axis=-1, keepdims=True) + RMS_EPS) * norm_w.astype(jnp.float32)
    o = o * jax.nn.silu(z.reshape(B, T, HV_C, DV_C).astype(jnp.float32))
    return o.astype(x.dtype).reshape(B, T, C_V_DIM) @ w_out, new_buf, s_new


def setup_inputs(seed: int = 0) -> dict:
    key = jax.random.key(seed)
    ks = jax.random.split(key, 26)
    f32 = jnp.float32

    def nrm(i, shape, scale):
        return scale * jax.random.normal(ks[i], shape, f32)

    a_band = min(LEFT_CHUNKS * CHUNK, PAST_LEN)
    return {
        'x_prompt': nrm(0, (BATCH, SEQ, D_MODEL), 1.0),
        'x_sample': nrm(1, (DEC_BATCH, DEC_SEQ, D_MODEL), 1.0),
        'cache_a_k': nrm(2, (N_A, DEC_BATCH, a_band, H_A, DH_A), 1.0),
        'cache_a_v': nrm(3, (N_A, DEC_BATCH, a_band, H_A, DH_A), 1.0),
        'cache_b_k': nrm(4, (N_B, DEC_BATCH, PAST_LEN, KV_B, DH_B), 1.0),
        'cache_b_v': nrm(5, (N_B, DEC_BATCH, PAST_LEN, KV_B, DH_B), 1.0),
        'cache_b_kidx': nrm(6, (N_B, DEC_BATCH, PAST_LEN, D_IDX), 1.0),
        'state_c_conv': nrm(7, (N_C, DEC_BATCH, CONV_W - 1, C_CONV_DIM), 1.0),
        'state_c_ssm': nrm(8, (N_C, DEC_BATCH, HV_C, DK_C, DV_C), 0.1),
        'a_w_in': nrm(9, (N_A, D_MODEL, 3 * D_MODEL), D_MODEL ** -0.5),
        'a_rel_bias': nrm(10, (N_A, H_A, 2 * REL_CLIP + 1), 0.2),
        'a_w_out': nrm(11, (N_A, D_MODEL, D_MODEL), BETA_INIT * D_MODEL ** -0.5),
        'b_w_in': nrm(12, (N_B, D_MODEL, B_IN_DIM), D_MODEL ** -0.5),
        'b_w_out': nrm(13, (N_B, D_MODEL, D_MODEL), BETA_INIT * D_MODEL ** -0.5),
        'c_w_in': nrm(14, (N_C, D_MODEL, C_IN_DIM), D_MODEL ** -0.5),
        'c_conv_w': nrm(15, (N_C, CONV_W, C_CONV_DIM), CONV_W ** -0.5),
        'c_a_log': jnp.log(jax.random.uniform(ks[16], (N_C, HV_C), f32, minval=1.0, maxval=16.0)),
        'c_dt_bias': nrm(17, (N_C, HV_C), 0.1),
        'c_norm_w': 1.0 + nrm(18, (N_C, DV_C), 0.05),
        'c_w_out': nrm(19, (N_C, C_V_DIM, D_MODEL), BETA_INIT * C_V_DIM ** -0.5),
        'ln1_g': 1.0 + nrm(20, (DEPTH, D_MODEL), 0.05),
        'ln1_b': nrm(21, (DEPTH, D_MODEL), 0.02),
        'mlp_w1': nrm(22, (DEPTH, D_MODEL, D_FF), D_MODEL ** -0.5),
        'mlp_w2': nrm(23, (DEPTH, D_FF, D_MODEL), BETA_INIT * D_FF ** -0.5),
        'ln2_g': 1.0 + nrm(24, (DEPTH, D_MODEL), 0.05),
        'ln2_b': nrm(25, (DEPTH, D_MODEL), 0.02),
    }


def reference(x_prompt, x_sample, cache_a_k, cache_a_v, cache_b_k, cache_b_v, cache_b_kidx,
              state_c_conv, state_c_ssm, a_w_in, a_rel_bias, a_w_out, b_w_in, b_w_out,
              c_w_in, c_conv_w, c_a_log, c_dt_bias, c_norm_w, c_w_out,
              ln1_g, ln1_b, mlp_w1, mlp_w2, ln2_g, ln2_b):
    yp, ys = x_prompt, x_sample
    akp, avp, aks, avs = [], [], [], []
    bkp, bvp, bip, bks, bvs, bis = [], [], [], [], [], []
    ccp, csp, ccs, css = [], [], [], []
    for i in range(DEPTH):
        kind, j = i % N_MIXERS, i // N_MIXERS
        if kind == 0:
            mp, k_, v_ = _mixer_a_prompt(yp, a_w_in[j], a_rel_bias[j], a_w_out[j])
            akp.append(k_)
            avp.append(v_)
            ms, k_, v_ = _mixer_a_sample(ys, cache_a_k[j], cache_a_v[j], a_w_in[j], a_rel_bias[j], a_w_out[j])
            aks.append(k_)
            avs.append(v_)
        elif kind == 1:
            mp, k_, v_, ki_ = _mixer_b_prompt(yp, b_w_in[j], b_w_out[j])
            bkp.append(k_)
            bvp.append(v_)
            bip.append(ki_)
            ms, k_, v_, ki_ = _mixer_b_sample(ys, cache_b_k[j], cache_b_v[j], cache_b_kidx[j], b_w_in[j], b_w_out[j])
            bks.append(k_)
            bvs.append(v_)
            bis.append(ki_)
        else:
            nb = yp.shape[0]
            buf0 = jnp.zeros((nb, CONV_W - 1, C_CONV_DIM), yp.dtype)
            st0 = jnp.zeros((nb, HV_C, DK_C, DV_C), jnp.float32)
            mp, cb_, st_ = _mixer_c(yp, buf0, st0, c_w_in[j], c_conv_w[j], c_a_log[j], c_dt_bias[j], c_norm_w[j], c_w_out[j])
            ccp.append(cb_)
            csp.append(st_)
            ms, cb_, st_ = _mixer_c(ys, state_c_conv[j], state_c_ssm[j], c_w_in[j], c_conv_w[j], c_a_log[j], c_dt_bias[j], c_norm_w[j], c_w_out[j])
            ccs.append(cb_)
            css.append(st_)
        yp = _layer_norm(ALPHA * yp + mp, ln1_g[i], ln1_b[i])
        ys = _layer_norm(ALPHA * ys + ms, ln1_g[i], ln1_b[i])
        yp = _layer_norm(ALPHA * yp + _sq_relu_mlp(yp, mlp_w1[i], mlp_w2[i]), ln2_g[i], ln2_b[i])
        ys = _layer_norm(ALPHA * ys + _sq_relu_mlp(ys, mlp_w1[i], mlp_w2[i]), ln2_g[i], ln2_b[i])
    return (yp, ys,
            jnp.stack(akp), jnp.stack(avp), jnp.stack(aks), jnp.stack(avs),
            jnp.stack(bkp), jnp.stack(bvp), jnp.stack(bip), jnp.stack(bks), jnp.stack(bvs), jnp.stack(bis),
            jnp.stack(ccp), jnp.stack(csp), jnp.stack(ccs), jnp.stack(css))
```

```python
import functools

import numpy as np
import jax
import jax.numpy as jnp
from jax import lax
from jax.experimental import pallas as pl
from jax.experimental.pallas import tpu as pltpu

D_MODEL = 2048
DEPTH = 4
PAST_LEN = 1024
CHUNK = 64
N_MIXERS = 3
D_FF = 4 * D_MODEL
H_A = 16
DH_A = D_MODEL // H_A
LEFT_CHUNKS = 8
REL_CLIP = 128
H_B = 16
KV_B = 4
DH_B = D_MODEL // H_B
H_IDX = 16
D_IDX = 64
TOPK_MAX = 256
HK_C = 16
HV_C = 32
DK_C = 128
DV_C = 128
CONV_W = 4
C_QK_DIM = HK_C * DK_C
C_V_DIM = HV_C * DV_C
C_CONV_DIM = 2 * C_QK_DIM + C_V_DIM
ALPHA = (2.0 * DEPTH) ** 0.25
LN_EPS = 1e-5
RMS_EPS = 1e-6

LANES = 128
SUBLANES = 8
VMEM_CAP_MB = 56

F32 = jnp.float32
BF16 = jnp.bfloat16
NEG_INF = float("-inf")
INT_MIN = -(2 ** 31)


def _params(sem, vmem_mb):
    return pltpu.CompilerParams(dimension_semantics=sem,
                                vmem_limit_bytes=min(vmem_mb, VMEM_CAP_MB) * 1024 * 1024)


def _dot(a, b):
    return jnp.dot(a, b, preferred_element_type=F32)


def _dot_nt(a, b):
    return lax.dot_general(a, b, (((1,), (1,)), ((), ())), preferred_element_type=F32)


def _dot_tn(a, b):
    return lax.dot_general(a, b, (((0,), (0,)), ((), ())), preferred_element_type=F32)


def _split2(a):
    hi = a.astype(BF16)
    lo = (a - hi.astype(F32)).astype(BF16)
    return hi, lo


def _dot_x3(a, b):
    ah, al = _split2(a)
    bh, bl = _split2(b)
    return _dot(ah, bh) + (_dot(ah, bl) + _dot(al, bh))


def _layer_norm(z, g, b):
    mu = jnp.mean(z, axis=-1, keepdims=True)
    zc = z - mu
    var = jnp.mean(zc * zc, axis=-1, keepdims=True)
    return zc * lax.rsqrt(var + LN_EPS) * g + b


def _sigmoid(x):
    return 1.0 / (1.0 + jnp.exp(-x))


def _mm_kernel(x_ref, w_ref, o_ref, xb_ref):
    @pl.when(pl.program_id(1) == 0)
    def _():
        xb_ref[...] = x_ref[...].astype(BF16)

    o_ref[...] = _dot(xb_ref[...], w_ref[...])


def _mm(x, w, tm, tn):
    m, k = x.shape
    n = w.shape[1]
    vmem = (2 * tm * k * 4 + tm * k * 2 + 2 * k * tn * 2 + 2 * tm * tn * 4) // 2 ** 20 + 4
    return pl.pallas_call(
        _mm_kernel,
        grid=(m // tm, n // tn),
        in_specs=[pl.BlockSpec((tm, k), lambda i, j: (i, 0)),
                  pl.BlockSpec((k, tn), lambda i, j: (0, j))],
        out_specs=pl.BlockSpec((tm, tn), lambda i, j: (i, j)),
        out_shape=jax.ShapeDtypeStruct((m, n), F32),
        scratch_shapes=[pltpu.VMEM((tm, k), BF16)],
        compiler_params=_params(("parallel", "arbitrary"), vmem),
        name="mm",
    )(x, w)


def _mm_x3_kernel(x_ref, w_ref, o_ref):
    o_ref[...] = _dot_x3(x_ref[...], w_ref[...])


def _mm_x3(x, w, tm):
    m, k = x.shape
    n = w.shape[1]
    return pl.pallas_call(
        _mm_x3_kernel,
        grid=(m // tm,),
        in_specs=[pl.BlockSpec((tm, k), lambda i: (i, 0)),
                  pl.BlockSpec((k, n), lambda i: (0, 0))],
        out_specs=pl.BlockSpec((tm, n), lambda i: (i, 0)),
        out_shape=jax.ShapeDtypeStruct((m, n), F32),
        compiler_params=_params(("parallel",), 24),
        name="mm_x3",
    )(x, w)


def _mm_ln_kernel(x_ref, w_ref, r_ref, g_ref, b_ref, o_ref, acc_ref):
    k = pl.program_id(1)

    @pl.when(k == 0)
    def _():
        acc_ref[...] = jnp.zeros_like(acc_ref)

    acc_ref[...] += _dot(x_ref[...].astype(BF16), w_ref[...])

    @pl.when(k == pl.num_programs(1) - 1)
    def _():
        o_ref[...] = _layer_norm(ALPHA * r_ref[...] + acc_ref[...], g_ref[...], b_ref[...])


def _mm_ln(x, w, resid, g, b, tm, tk):
    m, k = x.shape
    d = w.shape[1]
    xb = x.dtype.itemsize
    vmem = (2 * tm * tk * xb + 2 * tk * d * 2 + 5 * tm * d * 4) // 2 ** 20 + 4
    return pl.pallas_call(
        _mm_ln_kernel,
        grid=(m // tm, k // tk),
        in_specs=[pl.BlockSpec((tm, tk), lambda i, kk: (i, kk)),
                  pl.BlockSpec((tk, d), lambda i, kk: (kk, 0)),
                  pl.BlockSpec((tm, d), lambda i, kk: (i, 0)),
                  pl.BlockSpec((1, d), lambda i, kk: (0, 0)),
                  pl.BlockSpec((1, d), lambda i, kk: (0, 0))],
        out_specs=pl.BlockSpec((tm, d), lambda i, kk: (i, 0)),
        out_shape=jax.ShapeDtypeStruct((m, d), F32),
        scratch_shapes=[pltpu.VMEM((tm, d), F32)],
        compiler_params=_params(("parallel", "arbitrary"), vmem),
        name="mm_ln",
    )(x, w, resid, g.reshape(1, d), b.reshape(1, d))


def _mlp_ln_kernel(y_ref, w1_ref, w2_ref, g_ref, b_ref, o_ref, xb_ref, acc_ref):
    f = pl.program_id(1)

    @pl.when(f == 0)
    def _():
        xb_ref[...] = y_ref[...].astype(BF16)
        acc_ref[...] = jnp.zeros_like(acc_ref)

    h = jnp.maximum(_dot(xb_ref[...], w1_ref[...]), 0.0)
    acc_ref[...] += _dot((h * h).astype(BF16), w2_ref[...])

    @pl.when(f == pl.num_programs(1) - 1)
    def _():
        o_ref[...] = _layer_norm(ALPHA * y_ref[...] + acc_ref[...], g_ref[...], b_ref[...])


def _mlp_ln(y, w1, w2, g, b, tm, tf):
    m, d = y.shape
    f = w1.shape[1]
    vmem = (5 * tm * d * 4 + tm * d * 2 + 4 * d * tf * 2 + 2 * tm * tf * 4) // 2 ** 20 + 4
    return pl.pallas_call(
        _mlp_ln_kernel,
        grid=(m // tm, f // tf),
        in_specs=[pl.BlockSpec((tm, d), lambda i, ff: (i, 0)),
                  pl.BlockSpec((d, tf), lambda i, ff: (0, ff)),
                  pl.BlockSpec((tf, d), lambda i, ff: (ff, 0)),
                  pl.BlockSpec((1, d), lambda i, ff: (0, 0)),
                  pl.BlockSpec((1, d), lambda i, ff: (0, 0))],
        out_specs=pl.BlockSpec((tm, d), lambda i, ff: (i, 0)),
        out_shape=jax.ShapeDtypeStruct((m, d), F32),
        scratch_shapes=[pltpu.VMEM((tm, d), BF16), pltpu.VMEM((tm, d), F32)],
        compiler_params=_params(("parallel", "arbitrary"), vmem),
        name="mlp_ln",
    )(y, w1, w2, g.reshape(1, d), b.reshape(1, d))


A_QBLK = 512
A_PAIR = 2 * CHUNK
A_WIN = (LEFT_CHUNKS + 2) * CHUNK
A_UW = A_WIN + A_PAIR


def _softmax_pv(s, v):
    m = jnp.max(s, axis=-1, keepdims=True)
    p = jnp.exp(s - m)
    l = jnp.sum(p, axis=-1, keepdims=True)
    return _dot(p.astype(BF16), v) / l


def _bias_table(u_row, band, rows):
    u = jnp.broadcast_to(u_row, (rows, A_UW))
    return pltpu.roll(u, A_UW - A_PAIR + 1, 1, stride=1, stride_axis=0)[:, :band.shape[1]] + band


def _attn_a_prompt_kernel(q_ref, kp_ref, kc_ref, vp_ref, vc_ref, u_ref, band_ref, o_ref, kw_ref, vw_ref):
    i = pl.program_id(1)
    kw_ref[0:A_QBLK, :] = kp_ref[...].astype(BF16)
    kw_ref[A_QBLK:2 * A_QBLK, :] = kc_ref[...].astype(BF16)
    vw_ref[0:A_QBLK, :] = vp_ref[...].astype(BF16)
    vw_ref[A_QBLK:2 * A_QBLK, :] = vc_ref[...].astype(BF16)
    bias = _bias_table(u_ref[0], band_ref[...], A_PAIR)
    col = lax.broadcasted_iota(jnp.int32, (A_PAIR, A_WIN), 1)
    for c in range(A_QBLK // A_PAIR):
        r0 = c * A_PAIR
        q = (q_ref[r0:r0 + A_PAIR, :] * (DH_A ** -0.5)).astype(BF16)
        s = _dot_nt(q, kw_ref[r0:r0 + A_WIN, :]) + bias
        s = jnp.where(col + (i * A_QBLK - A_QBLK + r0) >= 0, s, NEG_INF)
        o_ref[r0:r0 + A_PAIR, :] = _softmax_pv(s, vw_ref[r0:r0 + A_WIN, :]).astype(BF16)


def _attn_a_prompt(qkv, u_rows, band, n_batch, t):
    nblk = t // A_QBLK

    def spec(col0, prev):
        if prev:
            return pl.BlockSpec((A_QBLK, DH_A), lambda b, i, h: (b * nblk + jnp.maximum(i - 1, 0), col0 + h))
        return pl.BlockSpec((A_QBLK, DH_A), lambda b, i, h: (b * nblk + i, col0 + h))

    return pl.pallas_call(
        _attn_a_prompt_kernel,
        grid=(n_batch, nblk, H_A),
        in_specs=[spec(0, False), spec(H_A, True), spec(H_A, False), spec(2 * H_A, True), spec(2 * H_A, False),
                  pl.BlockSpec((1, 1, A_UW), lambda b, i, h: (h, 0, 0)),
                  pl.BlockSpec((A_PAIR, A_WIN), lambda b, i, h: (0, 0))],
        out_specs=pl.BlockSpec((A_QBLK, DH_A), lambda b, i, h: (b * nblk + i, h)),
        out_shape=jax.ShapeDtypeStruct((n_batch * t, D_MODEL), BF16),
        scratch_shapes=[pltpu.VMEM((2 * A_QBLK, DH_A), BF16), pltpu.VMEM((2 * A_QBLK, DH_A), BF16)],
        compiler_params=_params(("parallel", "parallel", "parallel"), 16),
        name="attn_a_prompt",
    )(qkv, qkv, qkv, qkv, qkv, u_rows, band)


def _attn_a_sample_kernel(q_ref, kc_ref, kn_ref, vc_ref, vn_ref, u_ref, band_ref, o_ref, kw_ref, vw_ref,
                          *, past, s_len):
    win = kw_ref.shape[0]
    pad = win - past - s_len
    for h in range(H_A):
        cols = slice(h * DH_A, (h + 1) * DH_A)
        kw_ref[0:past, :] = kc_ref[:, cols].astype(BF16)
        kw_ref[past:past + s_len, :] = kn_ref[:, cols].astype(BF16)
        kw_ref[past + s_len:, :] = jnp.zeros((pad, DH_A), BF16)
        vw_ref[0:past, :] = vc_ref[:, cols].astype(BF16)
        vw_ref[past:past + s_len, :] = vn_ref[:, cols].astype(BF16)
        vw_ref[past + s_len:, :] = jnp.zeros((pad, DH_A), BF16)
        q = (q_ref[:, cols] * (DH_A ** -0.5)).astype(BF16)
        s = _dot_nt(q, kw_ref[...]) + _bias_table(u_ref[h], band_ref[...], s_len)
        o_ref[:, cols] = _softmax_pv(s, vw_ref[...]).astype(BF16)


def _attn_a_sample(qkv, k_cache, v_cache, u_rows, band, row0, n_batch, s_len):
    past = k_cache.shape[0] // n_batch
    win = band.shape[1]
    rb = row0 // s_len
    new = lambda colblk: pl.BlockSpec((s_len, D_MODEL), lambda b: (rb + b, colblk))
    old = pl.BlockSpec((past, D_MODEL), lambda b: (b, 0))
    return pl.pallas_call(
        functools.partial(_attn_a_sample_kernel, past=past, s_len=s_len),
        grid=(n_batch,),
        in_specs=[new(0), old, new(1), old, new(2),
                  pl.BlockSpec((H_A, 1, A_UW), lambda b: (0, 0, 0)),
                  pl.BlockSpec((s_len, win), lambda b: (0, 0))],
        out_specs=pl.BlockSpec((s_len, D_MODEL), lambda b: (b, 0)),
        out_shape=jax.ShapeDtypeStruct((n_batch * s_len, D_MODEL), BF16),
        scratch_shapes=[pltpu.VMEM((win, DH_A), BF16), pltpu.VMEM((win, DH_A), BF16)],
        compiler_params=_params(("parallel",), 32),
        name="attn_a_sample",
    )(qkv, k_cache, qkv, v_cache, qkv, u_rows, band)


def _a_bias_rows(rel_bias, q0):
    d = q0 + A_PAIR - 1 - np.arange(A_UW)
    return rel_bias[:, np.clip(d, -REL_CLIP, REL_CLIP) + REL_CLIP][:, None, :]


def _a_band_prompt():
    qc = LEFT_CHUNKS * CHUNK + np.arange(A_PAIR)[:, None]
    kc = np.arange(A_WIN)[None, :]
    vis = (kc // CHUNK <= qc // CHUNK) & (kc // CHUNK >= qc // CHUNK - LEFT_CHUNKS)
    return jnp.asarray(np.where(vis, 0.0, -np.inf).astype(np.float32))


def _a_band_sample(past, s_len, win):
    q_pos = PAST_LEN + np.arange(s_len)[:, None]
    k_pos = PAST_LEN - past + np.arange(win)[None, :]
    vis = ((k_pos >= 0) & (k_pos // CHUNK <= q_pos // CHUNK) & (k_pos // CHUNK >= q_pos // CHUNK - LEFT_CHUNKS)
           & (np.arange(win)[None, :] < past + s_len))
    return jnp.asarray(np.where(vis, 0.0, -np.inf).astype(np.float32))


B_COLS = H_B * DH_B + 2 * KV_B * DH_B + H_IDX * D_IDX + 2 * LANES
B_QI_COL = (H_B * DH_B + 2 * KV_B * DH_B) // (H_IDX * D_IDX)
B_WI_COL = (B_COLS - LANES) // LANES


def _key_to_float(key):
    return lax.bitcast_convert_type(jnp.where(key >= 0, key, key ^ jnp.int32(0x7FFFFFFF)), F32)


KEY_NEG_INF = int(np.array(-np.inf, np.float32).view(np.int32)) ^ 0x7FFFFFFF


def _lane_fold(x):
    acc = x[:, 0:LANES]
    for t in range(1, x.shape[1] // LANES):
        acc = acc + x[:, t * LANES:(t + 1) * LANES]
    return acc


def _dsa_kernel(q_ref, qi_ref, wi_ref, k_ref, v_ref, ka_ref, kb_ref, o_ref, qib_ref, sc_ref, msk_ref,
                qs_ref, m_ref, l_ref, acc_ref, *, nq, kblk, n_keys, q_pos0, q_step, n_sel, idx_bits):
    qbase = q_pos0 + pl.program_id(1) * q_step
    n_adm = jnp.minimum(((qbase + nq - 1) // CHUNK + 1) * CHUNK, n_keys)
    nb = (n_adm + kblk - 1) // kblk
    qpos = qbase + lax.broadcasted_iota(jnp.int32, (nq, kblk), 0)
    col0 = lax.broadcasted_iota(jnp.int32, (nq, kblk), 1)

    qib_ref[...] = qi_ref[...].astype(BF16)
    wsc = wi_ref[...] * (H_IDX ** -0.5 * D_IDX ** -0.5)

    def score_block(kb, carry):
        r = pl.multiple_of(kb * kblk, kblk)
        ka = ka_ref[0, pl.ds(r, kblk), :]
        kz = kb_ref[0, pl.ds(r, kblk), :]
        acc = jnp.zeros((nq, kblk), F32)
        for p in range(H_IDX // 2):
            q2 = qib_ref[:, p * LANES:(p + 1) * LANES]
            acc = acc + wsc[:, 2 * p:2 * p + 1] * jnp.maximum(_dot_nt(q2, ka), 0.0)
            acc = acc + wsc[:, 2 * p + 1:2 * p + 2] * jnp.maximum(_dot_nt(q2, kz), 0.0)
        col = col0 + kb * kblk
        adm = (col // CHUNK <= qpos // CHUNK) & (col < n_keys)
        sc_ref[kb] = jnp.where(adm, acc, NEG_INF)
        return carry

    lax.fori_loop(0, nb, score_block, 0)

    def count(pred_fn):
        def body(kb, c):
            return c + _lane_fold(jnp.where(pred_fn(kb, sc_ref[kb]), 1.0, 0.0))
        c = lax.fori_loop(0, nb, body, jnp.zeros((nq, LANES), F32))
        return jnp.sum(c, axis=-1, keepdims=True)

    def bit_step(bi, pre):
        cand = pre | jnp.left_shift(jnp.int32(1), 31 - bi)
        cand_s = cand ^ jnp.int32(INT_MIN)
        thr_c = _key_to_float(cand_s)
        cnt = count(lambda kb, sc: sc >= thr_c)
        return jnp.where((cnt >= n_sel) | (cand_s <= KEY_NEG_INF), cand, pre)

    pre = lax.fori_loop(0, 32, bit_step, jnp.zeros((nq, 1), jnp.int32))
    thr = _key_to_float(pre ^ jnp.int32(INT_MIN))
    need = n_sel - count(lambda kb, sc: sc > thr)
    n_ge = count(lambda kb, sc: sc >= thr)

    def idx_step(bi, lim):
        cand = lim | jnp.left_shift(jnp.int32(1), idx_bits - 1 - bi)
        cnt = count(lambda kb, sc: (sc == thr) & (col0 + kb * kblk < cand))
        return jnp.where(cnt < need, cand, lim)

    tied = jnp.max(jnp.where(thr > NEG_INF, n_ge, 0.0)) > n_sel
    lim = lax.cond(tied,
                   lambda: lax.fori_loop(0, idx_bits, idx_step, jnp.zeros((nq, 1), jnp.int32)),
                   lambda: jnp.full((nq, 1), 2 ** idx_bits, jnp.int32))

    def mask_block(kb, carry):
        sc = sc_ref[kb]
        sel = (sc > thr) | ((sc == thr) & (col0 + kb * kblk <= lim))
        sel = sel & (sc > NEG_INF) & (sc < float("inf"))
        msk_ref[kb] = jnp.where(sel, 0.0, NEG_INF)
        return carry

    lax.fori_loop(0, nb, mask_block, 0)

    group = H_B // KV_B
    gq = group * nq
    for h in range(H_B):
        qs_ref[h * nq:(h + 1) * nq, :] = (q_ref[:, h * DH_B:(h + 1) * DH_B] * (DH_B ** -0.5)).astype(BF16)
    m_ref[...] = jnp.full(m_ref.shape, NEG_INF, F32)
    l_ref[...] = jnp.zeros(l_ref.shape, F32)
    acc_ref[...] = jnp.zeros(acc_ref.shape, F32)

    def attend(kb, carry):
        r = pl.multiple_of(kb * kblk, kblk)
        msk = msk_ref[kb]
        s = [_dot_nt(qs_ref[g * gq:(g + 1) * gq, :], k_ref[0, pl.ds(r, kblk), g * DH_B:(g + 1) * DH_B])
             for g in range(KV_B)]
        for g in range(KV_B):
            ps = []
            for hq in range(group):
                rows = slice((g * group + hq) * nq, (g * group + hq + 1) * nq)
                sh = s[g][hq * nq:(hq + 1) * nq] + msk
                m_old = m_ref[rows]
                m_new = jnp.maximum(m_old, jnp.max(sh, axis=-1, keepdims=True))
                m_use = jnp.where(m_new == NEG_INF, 0.0, m_new)
                p = jnp.exp(sh - m_use)
                a = jnp.exp(m_old - m_use)
                m_ref[rows] = m_new
                l_ref[rows] = a * l_ref[rows] + jnp.sum(p, axis=-1, keepdims=True)
                acc_ref[rows] = a * acc_ref[rows]
                ps.append(p.astype(BF16))
            acc_ref[g * gq:(g + 1) * gq, :] += _dot(jnp.concatenate(ps, axis=0),
                                                    v_ref[0, pl.ds(r, kblk), g * DH_B:(g + 1) * DH_B])
        return carry

    lax.fori_loop(0, nb, attend, 0)
    for h in range(H_B):
        rows = slice(h * nq, (h + 1) * nq)
        o_ref[:, h * DH_B:(h + 1) * DH_B] = (acc_ref[rows] / l_ref[rows]).astype(BF16)


def _dsa(proj, k, v, ka, kz, *, row0, nq, n_qblk, kblk, n_keys, q_pos0, q_step):
    n_batch, lp, _ = k.shape
    nkb = lp // kblk
    n_sel = min(TOPK_MAX, n_keys // 4)
    rb = row0 // nq
    qspec = lambda width, colblk: pl.BlockSpec((nq, width), lambda b, i: (rb + b * n_qblk + i, colblk))
    kspec = lambda width: pl.BlockSpec((1, lp, width), lambda b, i: (b, 0, 0))
    kern = functools.partial(_dsa_kernel, nq=nq, kblk=kblk, n_keys=n_keys, q_pos0=q_pos0, q_step=q_step,
                             n_sel=n_sel, idx_bits=max(1, int(np.ceil(np.log2(lp)))))
    vmem = (2 * 2 * lp * (2 * KV_B * DH_B + 2 * LANES) + 2 * nkb * nq * kblk * 4
            + 2 * nq * (B_COLS + D_MODEL) * 4) // 2 ** 20 + 8
    return pl.pallas_call(
        kern,
        grid=(n_batch, n_qblk),
        in_specs=[qspec(H_B * DH_B, 0), qspec(H_IDX * D_IDX, B_QI_COL), qspec(LANES, B_WI_COL),
                  kspec(KV_B * DH_B), kspec(KV_B * DH_B), kspec(LANES), kspec(LANES)],
        out_specs=pl.BlockSpec((nq, D_MODEL), lambda b, i: (b * n_qblk + i, 0)),
        out_shape=jax.ShapeDtypeStruct((n_batch * n_qblk * nq, D_MODEL), BF16),
        scratch_shapes=[pltpu.VMEM((nq, H_IDX * D_IDX), BF16), pltpu.VMEM((nkb, nq, kblk), F32),
                        pltpu.VMEM((nkb, nq, kblk), F32), pltpu.VMEM((H_B * nq, DH_B), BF16),
                        pltpu.VMEM((H_B * nq, 1), F32), pltpu.VMEM((H_B * nq, 1), F32),
                        pltpu.VMEM((H_B * nq, DH_B), F32)],
        compiler_params=_params(("parallel", "arbitrary"), vmem),
        name="dsa",
    )(proj, proj, proj, k, v, ka, kz)


def _conv_kernel(u_ref, st_ref, w_ref, o_ref, ext_ref, *, t, cw, rows):
    j = pl.program_id(1)
    n_qk = C_QK_DIM // cw
    ext_ref[0:SUBLANES, :] = st_ref[0]
    ext_ref[SUBLANES:SUBLANES + t, :] = u_ref[...]
    w = w_ref[...]
    off = SUBLANES - (CONV_W - 1)
    scale = jnp.where(j < n_qk, DK_C ** -0.5, 1.0)

    def body(r, carry):
        base = pl.multiple_of(r * rows, rows)
        blk = ext_ref[pl.ds(base, rows + SUBLANES), :]
        acc = blk[off:off + rows, :] * w[0:1, :]
        for jj in range(1, CONV_W):
            acc = acc + blk[off + jj:off + jj + rows, :] * w[jj:jj + 1, :]
        a = acc * _sigmoid(acc)
        normed = []
        for hh in range(cw // DK_C):
            ah = a[:, hh * DK_C:(hh + 1) * DK_C]
            normed.append(ah * (lax.rsqrt(jnp.sum(ah * ah, axis=-1, keepdims=True) + RMS_EPS) * scale))
        an = jnp.concatenate(normed, axis=-1) if len(normed) > 1 else normed[0]
        o_ref[pl.ds(base, rows), :] = jnp.where(j < 2 * n_qk, an, a)
        return carry

    lax.fori_loop(0, t // rows, body, 0)


def _conv(u_all, state8, conv_w, *, row0, n_batch, t, cw, rows):
    rb = row0 // t
    return pl.pallas_call(
        functools.partial(_conv_kernel, t=t, cw=cw, rows=rows),
        grid=(n_batch, C_CONV_DIM // cw),
        in_specs=[pl.BlockSpec((t, cw), lambda b, j: (rb + b, j)),
                  pl.BlockSpec((1, SUBLANES, cw), lambda b, j: (b, 0, j)),
                  pl.BlockSpec((CONV_W, cw), lambda b, j: (0, j))],
        out_specs=pl.BlockSpec((t, cw), lambda b, j: (b, j)),
        out_shape=jax.ShapeDtypeStruct((n_batch * t, C_CONV_DIM), F32),
        scratch_shapes=[pltpu.VMEM((t + SUBLANES, cw), F32)],
        compiler_params=_params(("parallel", "parallel"), 6 * t * cw * 4 // 2 ** 20 + 8),
        name="conv",
    )(u_all, state8, conv_w)


def _gates_kernel(a_ref, at_ref, b_ref, alog_ref, alogt_ref, dt_ref, dtt_ref, gc_ref, gr_ref, beta_ref, *, c, tb):
    def decay(a_raw, a_log, dt):
        x = a_raw + dt
        softplus = jnp.maximum(x, 0.0) + jnp.log1p(jnp.exp(-jnp.abs(x)))
        return -jnp.exp(a_log) * softplus

    def split3(x):
        p0 = x.astype(BF16)
        r = x - p0.astype(F32)
        p1 = r.astype(BF16)
        p2 = (r - p1.astype(F32)).astype(BF16)
        return p0, p1, p2

    i = lax.broadcasted_iota(jnp.int32, (tb, tb), 0)
    j = lax.broadcasted_iota(jnp.int32, (tb, tb), 1)
    same = (i // c) == (j // c)
    lower = jnp.where(same & (j <= i), 1.0, 0.0).astype(BF16)
    upper = jnp.where(same & (i <= j), 1.0, 0.0).astype(BF16)
    g = decay(a_ref[...], alog_ref[...], dt_ref[...])
    gt = decay(at_ref[...], alogt_ref[...], dtt_ref[...])
    g0, g1, g2 = split3(g)
    gc_ref[...] = _dot(lower, g0) + (_dot(lower, g1) + _dot(lower, g2))
    t0, t1, t2 = split3(gt)
    gr_ref[...] = _dot(t0, upper) + (_dot(t1, upper) + _dot(t2, upper))
    beta_ref[...] = _sigmoid(b_ref[...])


def _gates(a_raw, b_raw, a_log, dt_bias, c, tb):
    n = a_raw.shape[0]
    tok = pl.BlockSpec((tb, HV_C), lambda i: (i, 0))
    hed = pl.BlockSpec((HV_C, tb), lambda i: (0, i))
    row = pl.BlockSpec((1, HV_C), lambda i: (0, 0))
    colv = pl.BlockSpec((HV_C, 1), lambda i: (0, 0))
    return pl.pallas_call(
        functools.partial(_gates_kernel, c=c, tb=tb),
        grid=(n // tb,),
        in_specs=[tok, hed, tok, row, colv, row, colv],
        out_specs=[tok, hed, tok],
        out_shape=[jax.ShapeDtypeStruct((n, HV_C), F32), jax.ShapeDtypeStruct((HV_C, n), F32),
                   jax.ShapeDtypeStruct((n, HV_C), F32)],
        compiler_params=_params(("parallel",), 16),
        name="gates",
    )(a_raw, a_raw.T, b_raw, a_log.reshape(1, HV_C), a_log.reshape(HV_C, 1),
      dt_bias.reshape(1, HV_C), dt_bias.reshape(HV_C, 1))


def _delta_kernel(q_ref, k_ref, v_ref, z_ref, gc_ref, gr_ref, beta_ref, s0_ref, nw_ref, o_ref, s_ref, *, c, hg):
    @pl.when(pl.program_id(2) == 0)
    def _():
        s_ref[...] = s0_ref[...]

    i = lax.broadcasted_iota(jnp.int32, (c, c), 0)
    j = lax.broadcasted_iota(jnp.int32, (c, c), 1)
    eye = jnp.where(i == j, 1.0, 0.0)
    rep = HV_C // HK_C
    heads = range(hg)
    gc_all, gr_all, beta_all = gc_ref[0, 0], gr_ref[0, 0], beta_ref[0, 0]
    q = [q_ref[:, (n // rep) * DK_C:(n // rep + 1) * DK_C] for n in heads]
    k = [k_ref[:, (n // rep) * DK_C:(n // rep + 1) * DK_C] for n in heads]
    v = [v_ref[:, n * DV_C:(n + 1) * DV_C] for n in heads]
    z = [z_ref[:, n * DV_C:(n + 1) * DV_C] for n in heads]
    s_old = [s_ref[n] for n in heads]
    gcol = [gc_all[:, n:n + 1] for n in heads]
    beta = [beta_all[:, n:n + 1] for n in heads]
    d_incl = [jnp.exp(jnp.where(i >= j, gcol[n] - gr_all[n:n + 1, :], NEG_INF)) for n in heads]
    kb = [k[n] * beta[n] for n in heads]
    kh = [k[n].astype(BF16) for n in heads]
    m = [_dot_nt(kb[n].astype(BF16), kh[n]) * jnp.where(i > j, d_incl[n], 0.0) for n in heads]
    t = [eye - jnp.where(i // 2 == j // 2, m[n], 0.0) for n in heads]
    s = 2
    while s < c:
        join = (i // (2 * s) == j // (2 * s)) & (i // s != j // s)
        tb = [t[n].astype(BF16) for n in heads]
        x = [_dot(tb[n], jnp.where(join, m[n], 0.0).astype(BF16)) for n in heads]
        t = [t[n] - _dot(x[n].astype(BF16), tb[n]) for n in heads]
        s *= 2
    e_g = [jnp.exp(gcol[n]) for n in heads]
    sol = [_dot(t[n].astype(BF16), jnp.concatenate([v[n] * beta[n], kb[n] * e_g[n]], axis=-1).astype(BF16))
           for n in heads]
    s_bf = [s_old[n].astype(BF16) for n in heads]
    ub = [(sol[n][:, :DV_C] - _dot(sol[n][:, DV_C:].astype(BF16), s_bf[n])).astype(BF16) for n in heads]
    qk = [(_dot_nt(q[n].astype(BF16), kh[n]) * d_incl[n]).astype(BF16) for n in heads]
    o = [_dot((q[n] * e_g[n]).astype(BF16), s_bf[n]) + _dot(qk[n], ub[n]) for n in heads]
    g_last = [gcol[n][c - 1:c, :] for n in heads]
    s_new = [s_old[n] * jnp.exp(g_last[n]) + _dot_tn((k[n] * jnp.exp(g_last[n] - gcol[n])).astype(BF16), ub[n])
             for n in heads]
    o = [o[n] * lax.rsqrt(jnp.mean(o[n] * o[n], axis=-1, keepdims=True) + RMS_EPS) * nw_ref[...] for n in heads]
    o = [(o[n] * (z[n] * _sigmoid(z[n]))).astype(BF16) for n in heads]
    for n in heads:
        s_ref[n] = s_new[n]
        o_ref[:, n * DV_C:(n + 1) * DV_C] = o[n]


def _delta(qkv_act, z_all, gc, gr, beta, s0, norm_w, *, row0, z_col0, n_batch, t, c, hg):
    nc = t // c
    nhg = HV_C // hg
    rep = HV_C // HK_C
    qw = hg // rep * DK_C
    rb = row0 // c
    vmem = 24
    return pl.pallas_call(
        functools.partial(_delta_kernel, c=c, hg=hg),
        grid=(n_batch, nhg, nc),
        in_specs=[pl.BlockSpec((c, qw), lambda b, g, ci: (b * nc + ci, g)),
                  pl.BlockSpec((c, qw), lambda b, g, ci: (b * nc + ci, C_QK_DIM // qw + g)),
                  pl.BlockSpec((c, hg * DV_C), lambda b, g, ci: (b * nc + ci, 2 * C_QK_DIM // (hg * DV_C) + g)),
                  pl.BlockSpec((c, hg * DV_C), lambda b, g, ci: (rb + b * nc + ci, z_col0 // (hg * DV_C) + g)),
                  pl.BlockSpec((1, 1, c, hg), lambda b, g, ci: (b * nc + ci, g, 0, 0)),
                  pl.BlockSpec((1, 1, hg, c), lambda b, g, ci: (b * nc + ci, g, 0, 0)),
                  pl.BlockSpec((1, 1, c, hg), lambda b, g, ci: (b * nc + ci, g, 0, 0)),
                  pl.BlockSpec((hg, DK_C, DV_C), lambda b, g, ci: (b * nhg + g, 0, 0)),
                  pl.BlockSpec((1, DV_C), lambda b, g, ci: (0, 0))],
        out_specs=[pl.BlockSpec((c, hg * DV_C), lambda b, g, ci: (b * nc + ci, g)),
                   pl.BlockSpec((hg, DK_C, DV_C), lambda b, g, ci: (b * nhg + g, 0, 0))],
        out_shape=[jax.ShapeDtypeStruct((n_batch * t, C_V_DIM), BF16),
                   jax.ShapeDtypeStruct((n_batch * HV_C, DK_C, DV_C), F32)],
        compiler_params=_params(("parallel", "parallel", "arbitrary"), vmem),
        name="delta",
    )(qkv_act, qkv_act, qkv_act, z_all, gc, gr, beta, s0, norm_w.reshape(1, DV_C))


def _chunk_layout(x, n_chunks, c, nhg, hg, head_major):
    if head_major:
        return x.reshape(nhg, hg, n_chunks, c).transpose(2, 0, 1, 3)
    return x.reshape(n_chunks, c, nhg, hg).transpose(0, 2, 1, 3)


TM = 512
C_HG = 8
B_QBLK = 128
B_KBLK = 512
B_KBLK_S = 384


def _pad_cols(w, width):
    return jnp.pad(w, ((0, 0), (0, width - w.shape[1])))


def kernel(x_prompt, x_sample, cache_a_k, cache_a_v, cache_b_k, cache_b_v, cache_b_kidx, state_c_conv,
           state_c_ssm, a_w_in, a_rel_bias, a_w_out, b_w_in, b_w_out, c_w_in, c_conv_w, c_a_log, c_dt_bias,
           c_norm_w, c_w_out, ln1_g, ln1_b, mlp_w1, mlp_w2, ln2_g, ln2_b):
    nb_p, t_p, _ = x_prompt.shape
    nb_s, t_s, _ = x_sample.shape
    mp = nb_p * t_p
    ms = nb_s * t_s
    y = jnp.concatenate([x_prompt.reshape(mp, D_MODEL), x_sample.reshape(ms, D_MODEL)], axis=0)
    outs = {n: [] for n in ("akp", "avp", "aks", "avs", "bkp", "bvp", "bip", "bks", "bvs", "bis",
                            "ccp", "csp", "ccs", "css")}
    for i in range(DEPTH):
        kind, j = i % N_MIXERS, i // N_MIXERS
        if kind == 0:
            qkv = _mm(y, a_w_in[j].astype(BF16), TM, 1024)
            k_all = qkv[:, D_MODEL:2 * D_MODEL]
            v_all = qkv[:, 2 * D_MODEL:]
            keep = min(LEFT_CHUNKS * CHUNK, t_p)
            for name, src in (("akp", k_all), ("avp", v_all)):
                outs[name].append(src[:mp].reshape(nb_p, t_p, H_A, DH_A)[:, t_p - keep:])
            for name, src in (("aks", k_all), ("avs", v_all)):
                outs[name].append(src[mp:].reshape(nb_s, t_s, H_A, DH_A))
            past = cache_a_k.shape[2]
            win = -(-(past + t_s) // LANES) * LANES
            o_p = _attn_a_prompt(qkv, _a_bias_rows(a_rel_bias[j], LEFT_CHUNKS * CHUNK), _a_band_prompt(), nb_p, t_p)
            o_s = _attn_a_sample(qkv, cache_a_k[j].reshape(nb_s * past, D_MODEL),
                                 cache_a_v[j].reshape(nb_s * past, D_MODEL),
                                 _a_bias_rows(a_rel_bias[j], past), _a_band_sample(past, t_s, win), mp, nb_s, t_s)
            mix_in, w_out = jnp.concatenate([o_p, o_s], axis=0), a_w_out[j]
        elif kind == 1:
            nq_, nk_, ni_ = H_B * DH_B, KV_B * DH_B, H_IDX * D_IDX
            w = b_w_in[j]
            w = jnp.concatenate([w[:, :nq_ + 2 * nk_ + ni_],
                                 _pad_cols(w[:, nq_ + 2 * nk_ + ni_:nq_ + 2 * nk_ + ni_ + D_IDX], LANES),
                                 _pad_cols(w[:, nq_ + 2 * nk_ + ni_ + D_IDX:], LANES)], axis=1)
            proj = _mm(y, w.astype(BF16), TM, B_COLS // 2)
            k_all = proj[:, nq_:nq_ + nk_]
            v_all = proj[:, nq_ + nk_:nq_ + 2 * nk_]
            ki_all = proj[:, nq_ + 2 * nk_ + ni_:nq_ + 2 * nk_ + ni_ + D_IDX]
            outs["bkp"].append(k_all[:mp].reshape(nb_p, t_p, KV_B, DH_B))
            outs["bvp"].append(v_all[:mp].reshape(nb_p, t_p, KV_B, DH_B))
            outs["bip"].append(ki_all[:mp].reshape(nb_p, t_p, D_IDX))
            outs["bks"].append(k_all[mp:].reshape(nb_s, t_s, KV_B, DH_B))
            outs["bvs"].append(v_all[mp:].reshape(nb_s, t_s, KV_B, DH_B))
            outs["bis"].append(ki_all[mp:].reshape(nb_s, t_s, D_IDX))

            def idx_keys(ki):
                z = jnp.zeros_like(ki)
                return jnp.concatenate([ki, z], axis=-1).astype(BF16), jnp.concatenate([z, ki], axis=-1).astype(BF16)

            ka_p, kz_p = idx_keys(ki_all[:mp].reshape(nb_p, t_p, D_IDX))
            o_p = _dsa(proj, k_all[:mp].reshape(nb_p, t_p, nk_).astype(BF16),
                       v_all[:mp].reshape(nb_p, t_p, nk_).astype(BF16), ka_p, kz_p,
                       row0=0, nq=B_QBLK, n_qblk=t_p // B_QBLK, kblk=B_KBLK, n_keys=t_p, q_pos0=0, q_step=B_QBLK)
            n_keys = PAST_LEN + t_s
            lp = -(-n_keys // B_KBLK_S) * B_KBLK_S

            def with_cache(cache, new):
                full = jnp.concatenate([cache.reshape(nb_s, PAST_LEN, -1), new.reshape(nb_s, t_s, -1)], axis=1)
                return jnp.pad(full, ((0, 0), (0, lp - n_keys), (0, 0)))

            ka_s, kz_s = idx_keys(with_cache(cache_b_kidx[j], ki_all[mp:]))
            o_s = _dsa(proj, with_cache(cache_b_k[j], k_all[mp:]).astype(BF16),
                       with_cache(cache_b_v[j], v_all[mp:]).astype(BF16), ka_s, kz_s,
                       row0=mp, nq=t_s, n_qblk=1, kblk=B_KBLK_S, n_keys=n_keys, q_pos0=PAST_LEN, q_step=0)
            mix_in, w_out = jnp.concatenate([o_p, o_s], axis=0), b_w_out[j]
        else:
            w = c_w_in[j]
            n_main = C_CONV_DIM + C_V_DIM
            proj = _mm(y, w[:, :n_main].astype(BF16), TM, 1536)
            gate_raw = _mm_x3(y, _pad_cols(w[:, n_main:], LANES), TM)
            b_raw, a_raw = gate_raw[:, :HV_C], gate_raw[:, HV_C:2 * HV_C]
            nhg = HV_C // C_HG
            pieces = []
            for (row0, nbt, tt, state, s0, names) in (
                    (0, nb_p, t_p, jnp.zeros((nb_p, CONV_W - 1, C_CONV_DIM), F32),
                     jnp.zeros((nb_p * HV_C, DK_C, DV_C), F32), ("ccp", "csp")),
                    (mp, nb_s, t_s, state_c_conv[j], state_c_ssm[j].reshape(nb_s * HV_C, DK_C, DV_C), ("ccs", "css"))):
                c = min(CHUNK, tt)
                n_rows = nbt * tt
                u = proj[row0:row0 + n_rows, :C_CONV_DIM].reshape(nbt, tt, C_CONV_DIM)
                ext_tail = jnp.concatenate([state, u], axis=1)[:, tt:]
                outs[names[0]].append(ext_tail)
                state8 = jnp.pad(state, ((0, 0), (SUBLANES - (CONV_W - 1), 0), (0, 0)))
                act = _conv(proj, state8, c_conv_w[j], row0=row0, n_batch=nbt, t=tt, cw=256, rows=min(tt, 256))
                gc, gr, beta = _gates(a_raw[row0:row0 + n_rows], b_raw[row0:row0 + n_rows], c_a_log[j], c_dt_bias[j],
                                      c, 512)
                n_chunks = n_rows // c
                o_c, s_new = _delta(act, proj, _chunk_layout(gc, n_chunks, c, nhg, C_HG, False),
                                    _chunk_layout(gr, n_chunks, c, nhg, C_HG, True),
                                    _chunk_layout(beta, n_chunks, c, nhg, C_HG, False), s0, c_norm_w[j],
                                    row0=row0, z_col0=C_CONV_DIM, n_batch=nbt, t=tt, c=c, hg=C_HG)
                outs[names[1]].append(s_new.reshape(nbt, HV_C, DK_C, DV_C))
                pieces.append(o_c)
            mix_in, w_out = jnp.concatenate(pieces, axis=0), c_w_out[j]
        y = _mm_ln(mix_in, w_out.astype(BF16), y, ln1_g[i], ln1_b[i], TM, 512)
        y = _mlp_ln(y, mlp_w1[i].astype(BF16), mlp_w2[i].astype(BF16), ln2_g[i], ln2_b[i], TM, 512)
    st = lambda name: jnp.stack(outs[name])
    return (y[:mp].reshape(nb_p, t_p, D_MODEL), y[mp:].reshape(nb_s, t_s, D_MODEL),
            st("akp"), st("avp"), st("aks"), st("avs"),
            st("bkp"), st("bvp"), st("bip"), st("bks"), st("bvs"), st("bis"),
            st("ccp"), st("csp"), st("ccs"), st("css"))
```

```python
import functools

import numpy as np
import jax
import jax.numpy as jnp
from jax import lax
from jax.experimental import pallas as pl
from jax.experimental.pallas import tpu as pltpu

D_MODEL = 2048
DEPTH = 4
PAST_LEN = 1024
CHUNK = 64
N_MIXERS = 3
D_FF = 4 * D_MODEL
H_A = 16
DH_A = D_MODEL // H_A
LEFT_CHUNKS = 8
REL_CLIP = 128
H_B = 16
KV_B = 4
DH_B = D_MODEL // H_B
H_IDX = 16
D_IDX = 64
TOPK_MAX = 256
HK_C = 16
HV_C = 32
DK_C = 128
DV_C = 128
CONV_W = 4
C_QK_DIM = HK_C * DK_C
C_V_DIM = HV_C * DV_C
C_CONV_DIM = 2 * C_QK_DIM + C_V_DIM
ALPHA = (2.0 * DEPTH) ** 0.25
LN_EPS = 1e-5
RMS_EPS = 1e-6

LANES = 128
SUBLANES = 8
VMEM_CAP_MB = 56

F32 = jnp.float32
BF16 = jnp.bfloat16
NEG_INF = float("-inf")
INT_MIN = -(2 ** 31)


def _params(sem, vmem_mb):
    return pltpu.CompilerParams(dimension_semantics=sem,
                                vmem_limit_bytes=min(vmem_mb, VMEM_CAP_MB) * 1024 * 1024)


def _dot(a, b):
    return jnp.dot(a, b, preferred_element_type=F32)


def _dot_nt(a, b):
    return lax.dot_general(a, b, (((1,), (1,)), ((), ())), preferred_element_type=F32)


def _dot_tn(a, b):
    return lax.dot_general(a, b, (((0,), (0,)), ((), ())), preferred_element_type=F32)


def _split2(a):
    hi = a.astype(BF16)
    lo = (a - hi.astype(F32)).astype(BF16)
    return hi, lo


def _dot_x3(a, b):
    ah, al = _split2(a)
    bh, bl = _split2(b)
    return _dot(ah, bh) + (_dot(ah, bl) + _dot(al, bh))


def _layer_norm(z, g, b):
    mu = jnp.mean(z, axis=-1, keepdims=True)
    zc = z - mu
    var = jnp.mean(zc * zc, axis=-1, keepdims=True)
    return zc * lax.rsqrt(var + LN_EPS) * g + b


def _sigmoid(x):
    return 0.5 * jnp.tanh(0.5 * x) + 0.5


def _mm_kernel(x_ref, w_ref, *refs):
    *o_refs, wb_ref = refs

    @pl.when(pl.program_id(1) == 0)
    def _():
        wb_ref[...] = w_ref[...].astype(BF16)

    res = _dot(x_ref[...].astype(BF16), wb_ref[...])
    for o_ref in o_refs:
        o_ref[...] = res.astype(o_ref.dtype)


def _mm(x, w, tm, tn, dtypes=(F32,)):
    m, k = x.shape
    n = w.shape[1]
    out_b = sum(jnp.dtype(d).itemsize for d in dtypes)
    vmem = (2 * tm * k * 4 + 2 * k * tn * 4 + k * tn * 2 + 2 * tm * tn * out_b + tm * tn * 4) // 2 ** 20 + 4
    outs = pl.pallas_call(
        _mm_kernel,
        grid=(n // tn, m // tm),
        in_specs=[pl.BlockSpec((tm, k), lambda j, i: (i, 0)),
                  pl.BlockSpec((k, tn), lambda j, i: (0, j))],
        out_specs=[pl.BlockSpec((tm, tn), lambda j, i: (i, j)) for _ in dtypes],
        out_shape=[jax.ShapeDtypeStruct((m, n), d) for d in dtypes],
        scratch_shapes=[pltpu.VMEM((k, tn), BF16)],
        compiler_params=_params(("parallel", "arbitrary"), vmem),
        name="mm",
    )(x, w)
    return outs[0] if len(dtypes) == 1 else outs


def _mm_x3_kernel(x_ref, w_ref, o_ref):
    o_ref[...] = _dot_x3(x_ref[...], w_ref[...])


def _mm_x3(x, w, tm):
    m, k = x.shape
    n = w.shape[1]
    return pl.pallas_call(
        _mm_x3_kernel,
        grid=(m // tm,),
        in_specs=[pl.BlockSpec((tm, k), lambda i: (i, 0)),
                  pl.BlockSpec((k, n), lambda i: (0, 0))],
        out_specs=pl.BlockSpec((tm, n), lambda i: (i, 0)),
        out_shape=jax.ShapeDtypeStruct((m, n), F32),
        compiler_params=_params(("parallel",), 24),
        name="mm_x3",
    )(x, w)


def _mm_ln_kernel(x_ref, w_ref, r_ref, g_ref, b_ref, o_ref, acc_ref):
    k = pl.program_id(1)

    @pl.when(k == 0)
    def _():
        acc_ref[...] = jnp.zeros_like(acc_ref)

    acc_ref[...] += _dot(x_ref[...].astype(BF16), w_ref[...])

    @pl.when(k == pl.num_programs(1) - 1)
    def _():
        o_ref[...] = _layer_norm(ALPHA * r_ref[...] + acc_ref[...], g_ref[...], b_ref[...])


def _mm_ln(x, w, resid, g, b, tm, tk):
    m, k = x.shape
    d = w.shape[1]
    xb = x.dtype.itemsize
    vmem = (2 * tm * tk * xb + 2 * tk * d * 2 + 8 * tm * d * 4) // 2 ** 20 + 4
    return pl.pallas_call(
        _mm_ln_kernel,
        grid=(m // tm, k // tk),
        in_specs=[pl.BlockSpec((tm, tk), lambda i, kk: (i, kk)),
                  pl.BlockSpec((tk, d), lambda i, kk: (kk, 0)),
                  pl.BlockSpec((tm, d), lambda i, kk: (i, 0)),
                  pl.BlockSpec((1, d), lambda i, kk: (0, 0)),
                  pl.BlockSpec((1, d), lambda i, kk: (0, 0))],
        out_specs=pl.BlockSpec((tm, d), lambda i, kk: (i, 0)),
        out_shape=jax.ShapeDtypeStruct((m, d), F32),
        scratch_shapes=[pltpu.VMEM((tm, d), F32)],
        compiler_params=_params(("parallel", "arbitrary"), vmem),
        name="mm_ln",
    )(x, w, resid, g.reshape(1, d), b.reshape(1, d))


def _mlp_ln_kernel(y_ref, w1_ref, w2_ref, g_ref, b_ref, o_ref, xb_ref, acc_ref):
    f = pl.program_id(1)

    @pl.when(f == 0)
    def _():
        xb_ref[...] = y_ref[...].astype(BF16)
        acc_ref[...] = jnp.zeros_like(acc_ref)

    h = jnp.maximum(_dot(xb_ref[...], w1_ref[...]), 0.0)
    acc_ref[...] += _dot((h * h).astype(BF16), w2_ref[...])

    @pl.when(f == pl.num_programs(1) - 1)
    def _():
        o_ref[...] = _layer_norm(ALPHA * y_ref[...] + acc_ref[...], g_ref[...], b_ref[...])


def _mlp_ln(y, w1, w2, g, b, tm, tf):
    m, d = y.shape
    f = w1.shape[1]
    vmem = (5 * tm * d * 4 + tm * d * 2 + 4 * d * tf * 2 + 2 * tm * tf * 4) // 2 ** 20 + 4
    return pl.pallas_call(
        _mlp_ln_kernel,
        grid=(m // tm, f // tf),
        in_specs=[pl.BlockSpec((tm, d), lambda i, ff: (i, 0)),
                  pl.BlockSpec((d, tf), lambda i, ff: (0, ff)),
                  pl.BlockSpec((tf, d), lambda i, ff: (ff, 0)),
                  pl.BlockSpec((1, d), lambda i, ff: (0, 0)),
                  pl.BlockSpec((1, d), lambda i, ff: (0, 0))],
        out_specs=pl.BlockSpec((tm, d), lambda i, ff: (i, 0)),
        out_shape=jax.ShapeDtypeStruct((m, d), F32),
        scratch_shapes=[pltpu.VMEM((tm, d), BF16), pltpu.VMEM((tm, d), F32)],
        compiler_params=_params(("parallel", "arbitrary"), vmem),
        name="mlp_ln",
    )(y, w1, w2, g.reshape(1, d), b.reshape(1, d))


A_QBLK = 512
A_PAIR = 2 * CHUNK
A_WIN = (LEFT_CHUNKS + 2) * CHUNK
A_UW = A_WIN + A_PAIR


def _softmax_pv(s, v):
    m = jnp.max(s, axis=-1, keepdims=True)
    p = jnp.exp(s - m)
    l = jnp.sum(p, axis=-1, keepdims=True)
    return _dot(p.astype(BF16), v) / l


def _bias_table(u_row, band, rows):
    u = jnp.broadcast_to(u_row, (rows, A_UW))
    return pltpu.roll(u, A_UW - A_PAIR + 1, 1, stride=1, stride_axis=0)[:, :band.shape[1]] + band


def _attn_a_prompt_kernel(q_ref, kp_ref, kc_ref, vp_ref, vc_ref, u_ref, band_ref, o_ref, ko_ref, vo_ref,
                          kw_ref, vw_ref):
    i = pl.program_id(2)

    @pl.when(i == pl.num_programs(2) - 1)
    def _():
        ko_ref[...] = kc_ref[...]
        vo_ref[...] = vc_ref[...]

    kw_ref[0:A_QBLK, :] = kp_ref[...].astype(BF16)
    kw_ref[A_QBLK:2 * A_QBLK, :] = kc_ref[...].astype(BF16)
    vw_ref[0:A_QBLK, :] = vp_ref[...].astype(BF16)
    vw_ref[A_QBLK:2 * A_QBLK, :] = vc_ref[...].astype(BF16)
    bias = _bias_table(u_ref[0], band_ref[...], A_PAIR)
    col = lax.broadcasted_iota(jnp.int32, (A_PAIR, A_WIN), 1)
    pairs = range(A_QBLK // A_PAIR)
    q = [(q_ref[c * A_PAIR:(c + 1) * A_PAIR, :] * (DH_A ** -0.5)).astype(BF16) for c in pairs]
    s = [_dot_nt(q[c], kw_ref[c * A_PAIR:c * A_PAIR + A_WIN, :]) + bias for c in pairs]
    s = [jnp.where(col + (i * A_QBLK - A_QBLK + c * A_PAIR) >= 0, s[c], NEG_INF) for c in pairs]
    o = [_softmax_pv(s[c], vw_ref[c * A_PAIR:c * A_PAIR + A_WIN, :]).astype(BF16) for c in pairs]
    for c in pairs:
        o_ref[c * A_PAIR:(c + 1) * A_PAIR, :] = o[c]


def _attn_a_prompt(qkv, u_rows, band, n_batch, t):
    nblk = t // A_QBLK

    def spec(col0, prev):
        if prev:
            return pl.BlockSpec((A_QBLK, DH_A), lambda b, h, i: (b * nblk + jnp.maximum(i - 1, 0), col0 + h))
        return pl.BlockSpec((A_QBLK, DH_A), lambda b, h, i: (b * nblk + i, col0 + h))

    tail = pl.BlockSpec((A_QBLK, DH_A), lambda b, h, i: (b, h))
    return pl.pallas_call(
        _attn_a_prompt_kernel,
        grid=(n_batch, H_A, nblk),
        in_specs=[spec(0, False), spec(H_A, True), spec(H_A, False), spec(2 * H_A, True), spec(2 * H_A, False),
                  pl.BlockSpec((1, 1, A_UW), lambda b, h, i: (h, 0, 0)),
                  pl.BlockSpec((A_PAIR, A_WIN), lambda b, h, i: (0, 0))],
        out_specs=[pl.BlockSpec((A_QBLK, DH_A), lambda b, h, i: (b * nblk + i, h)), tail, tail],
        out_shape=[jax.ShapeDtypeStruct((qkv.shape[0], D_MODEL), BF16),
                   jax.ShapeDtypeStruct((n_batch * A_QBLK, D_MODEL), F32),
                   jax.ShapeDtypeStruct((n_batch * A_QBLK, D_MODEL), F32)],
        scratch_shapes=[pltpu.VMEM((2 * A_QBLK, DH_A), BF16), pltpu.VMEM((2 * A_QBLK, DH_A), BF16)],
        compiler_params=_params(("parallel", "parallel", "arbitrary"), 16),
        name="attn_a_prompt",
    )(qkv, qkv, qkv, qkv, qkv, u_rows, band)


def _attn_a_sample_kernel(q_ref, kc_ref, kn_ref, vc_ref, vn_ref, u_ref, band_ref, o_hbm_ref, o_ref, ko_ref, vo_ref,
                          kw_ref, vw_ref, *, past, s_len):
    del o_hbm_ref
    ko_ref[...] = kn_ref[...]
    vo_ref[...] = vn_ref[...]
    win = kw_ref.shape[0]
    pad = win - past - s_len
    for h in range(H_A):
        cols = slice(h * DH_A, (h + 1) * DH_A)
        kw_ref[0:past, :] = kc_ref[:, cols].astype(BF16)
        kw_ref[past:past + s_len, :] = kn_ref[:, cols].astype(BF16)
        kw_ref[past + s_len:, :] = jnp.zeros((pad, DH_A), BF16)
        vw_ref[0:past, :] = vc_ref[:, cols].astype(BF16)
        vw_ref[past:past + s_len, :] = vn_ref[:, cols].astype(BF16)
        vw_ref[past + s_len:, :] = jnp.zeros((pad, DH_A), BF16)
        q = (q_ref[:, cols] * (DH_A ** -0.5)).astype(BF16)
        s = _dot_nt(q, kw_ref[...]) + _bias_table(u_ref[h], band_ref[...], s_len)
        o_ref[:, cols] = _softmax_pv(s, vw_ref[...]).astype(BF16)


def _attn_a_sample(qkv, k_cache, v_cache, u_rows, band, o_all, row0, n_batch, s_len):
    past = k_cache.shape[0] // n_batch
    win = band.shape[1]
    rb = row0 // s_len
    new = lambda colblk: pl.BlockSpec((s_len, D_MODEL), lambda b: (rb + b, colblk))
    old = pl.BlockSpec((past, D_MODEL), lambda b: (b, 0))
    rows = pl.BlockSpec((s_len, D_MODEL), lambda b: (b, 0))
    return pl.pallas_call(
        functools.partial(_attn_a_sample_kernel, past=past, s_len=s_len),
        grid=(n_batch,),
        in_specs=[new(0), old, new(1), old, new(2),
                  pl.BlockSpec((H_A, 1, A_UW), lambda b: (0, 0, 0)),
                  pl.BlockSpec((s_len, win), lambda b: (0, 0)),
                  pl.BlockSpec(memory_space=pl.ANY)],
        out_specs=[pl.BlockSpec((s_len, D_MODEL), lambda b: (rb + b, 0)), rows, rows],
        out_shape=[jax.ShapeDtypeStruct(o_all.shape, BF16),
                   jax.ShapeDtypeStruct((n_batch * s_len, D_MODEL), F32),
                   jax.ShapeDtypeStruct((n_batch * s_len, D_MODEL), F32)],
        input_output_aliases={7: 0},
        scratch_shapes=[pltpu.VMEM((win, DH_A), BF16), pltpu.VMEM((win, DH_A), BF16)],
        compiler_params=_params(("parallel",), 32),
        name="attn_a_sample",
    )(qkv, k_cache, qkv, v_cache, qkv, u_rows, band, o_all)


def _a_bias_rows(rel_bias, q0):
    d = q0 + A_PAIR - 1 - np.arange(A_UW)
    return rel_bias[:, np.clip(d, -REL_CLIP, REL_CLIP) + REL_CLIP][:, None, :]


def _a_band_prompt():
    qc = LEFT_CHUNKS * CHUNK + np.arange(A_PAIR)[:, None]
    kc = np.arange(A_WIN)[None, :]
    vis = (kc // CHUNK <= qc // CHUNK) & (kc // CHUNK >= qc // CHUNK - LEFT_CHUNKS)
    return jnp.asarray(np.where(vis, 0.0, -np.inf).astype(np.float32))


def _a_band_sample(past, s_len, win):
    q_pos = PAST_LEN + np.arange(s_len)[:, None]
    k_pos = PAST_LEN - past + np.arange(win)[None, :]
    vis = ((k_pos >= 0) & (k_pos // CHUNK <= q_pos // CHUNK) & (k_pos // CHUNK >= q_pos // CHUNK - LEFT_CHUNKS)
           & (np.arange(win)[None, :] < past + s_len))
    return jnp.asarray(np.where(vis, 0.0, -np.inf).astype(np.float32))


B_XW = 4 * LANES


def _key_to_float(key):
    return lax.bitcast_convert_type(jnp.where(key >= 0, key, key ^ jnp.int32(0x7FFFFFFF)), F32)


KEY_NEG_INF = int(np.array(-np.inf, np.float32).view(np.int32)) ^ 0x7FFFFFFF


def _lane_fold(x):
    acc = x[:, 0:LANES]
    for t in range(1, x.shape[1] // LANES):
        acc = acc + x[:, t * LANES:(t + 1) * LANES]
    return acc


def _dsa_kernel(q_ref, qi_ref, wi_ref, k_ref, v_ref, ka_ref, kb_ref, *refs,
                nq, kblk, n_keys, q_pos0, q_step, n_sel, idx_bits):
    o_ref, sc_ref, msk_ref, qs_ref, m_ref, l_ref, acc_ref = refs[-7:]
    qbase = q_pos0 + pl.program_id(1) * q_step
    n_adm = jnp.minimum(((qbase + nq - 1) // CHUNK + 1) * CHUNK, n_keys)
    nb = (n_adm + kblk - 1) // kblk
    qpos = qbase + lax.broadcasted_iota(jnp.int32, (nq, kblk), 0)
    col0 = lax.broadcasted_iota(jnp.int32, (nq, kblk), 1)

    wsc = wi_ref[...] * (H_IDX ** -0.5 * D_IDX ** -0.5)

    def score_block(kb, carry):
        r = pl.multiple_of(kb * kblk, kblk)
        ka = ka_ref[pl.ds(r, kblk), :]
        kz = kb_ref[pl.ds(r, kblk), :]
        acc = jnp.zeros((nq, kblk), F32)
        for p in range(H_IDX // 2):
            q2 = qi_ref[:, p * LANES:(p + 1) * LANES]
            acc = acc + wsc[:, 2 * p:2 * p + 1] * jnp.maximum(_dot_nt(q2, ka), 0.0)
            acc = acc + wsc[:, 2 * p + 1:2 * p + 2] * jnp.maximum(_dot_nt(q2, kz), 0.0)
        col = col0 + kb * kblk
        adm = (col // CHUNK <= qpos // CHUNK) & (col < n_keys)
        sc_ref[kb] = jnp.where(adm, acc, NEG_INF)
        return carry

    lax.fori_loop(0, nb, score_block, 0)

    def count(pred_fn):
        def body(kb, c):
            return c + _lane_fold(jnp.where(pred_fn(kb, sc_ref[kb]), 1.0, 0.0))
        c = lax.fori_loop(0, nb, body, jnp.zeros((nq, LANES), F32))
        return jnp.sum(c, axis=-1, keepdims=True)

    def bit_step(bi, pre):
        cand = pre | jnp.left_shift(jnp.int32(1), 31 - bi)
        cand_s = cand ^ jnp.int32(INT_MIN)
        thr_c = _key_to_float(cand_s)
        cnt = count(lambda kb, sc: sc >= thr_c)
        return jnp.where((cnt >= n_sel) | (cand_s <= KEY_NEG_INF), cand, pre)

    pre = lax.fori_loop(0, 32, bit_step, jnp.zeros((nq, 1), jnp.int32))
    thr = _key_to_float(pre ^ jnp.int32(INT_MIN))
    need = n_sel - count(lambda kb, sc: sc > thr)
    n_ge = count(lambda kb, sc: sc >= thr)

    def idx_step(bi, lim):
        cand = lim | jnp.left_shift(jnp.int32(1), idx_bits - 1 - bi)
        cnt = count(lambda kb, sc: (sc == thr) & (col0 + kb * kblk < cand))
        return jnp.where(cnt < need, cand, lim)

    tied = jnp.max(jnp.where(thr > NEG_INF, n_ge, 0.0)) > n_sel
    lim = lax.cond(tied,
                   lambda: lax.fori_loop(0, idx_bits, idx_step, jnp.zeros((nq, 1), jnp.int32)),
                   lambda: jnp.full((nq, 1), 2 ** idx_bits, jnp.int32))

    def mask_block(kb, carry):
        sc = sc_ref[kb]
        sel = (sc > thr) | ((sc == thr) & (col0 + kb * kblk <= lim))
        sel = sel & (sc > NEG_INF) & (sc < float("inf"))
        msk_ref[kb] = jnp.where(sel, 0.0, NEG_INF)
        return carry

    lax.fori_loop(0, nb, mask_block, 0)

    group = H_B // KV_B
    gq = group * nq
    for h in range(H_B):
        qs_ref[h * nq:(h + 1) * nq, :] = q_ref[:, h * DH_B:(h + 1) * DH_B]
    m_ref[...] = jnp.full(m_ref.shape, NEG_INF, F32)
    l_ref[...] = jnp.zeros(l_ref.shape, F32)
    acc_ref[...] = jnp.zeros(acc_ref.shape, F32)

    def attend(kb, carry):
        r = pl.multiple_of(kb * kblk, kblk)
        msk = msk_ref[kb]
        s = [_dot_nt(qs_ref[g * gq:(g + 1) * gq, :], k_ref[pl.ds(r, kblk), g * DH_B:(g + 1) * DH_B])
             for g in range(KV_B)]
        for g in range(KV_B):
            ps = []
            for hq in range(group):
                rows = slice((g * group + hq) * nq, (g * group + hq + 1) * nq)
                sh = s[g][hq * nq:(hq + 1) * nq] + msk
                m_old = m_ref[rows]
                m_new = jnp.maximum(m_old, jnp.max(sh, axis=-1, keepdims=True))
                m_use = jnp.where(m_new == NEG_INF, 0.0, m_new)
                p = jnp.exp(sh - m_use)
                a = jnp.exp(m_old - m_use)
                m_ref[rows] = m_new
                l_ref[rows] = a * l_ref[rows] + jnp.sum(p, axis=-1, keepdims=True)
                acc_ref[rows] = a * acc_ref[rows]
                ps.append(p.astype(BF16))
            acc_ref[g * gq:(g + 1) * gq, :] += _dot(jnp.concatenate(ps, axis=0),
                                                    v_ref[pl.ds(r, kblk), g * DH_B:(g + 1) * DH_B])
        return carry

    lax.fori_loop(0, nb, attend, 0)
    for h in range(H_B):
        rows = slice(h * nq, (h + 1) * nq)
        o_ref[:, h * DH_B:(h + 1) * DH_B] = (acc_ref[rows] / l_ref[rows]).astype(BF16)


def _dsa(qq, x32, k, v, ka, kz, o_all, *, ka_col, kz_col, row0, n_batch, lp, nq, n_qblk, kblk, n_keys,
         q_pos0, q_step):
    nkb = lp // kblk
    n_sel = min(TOPK_MAX, n_keys // 4)
    rb = row0 // nq
    qspec = lambda width, colblk: pl.BlockSpec((nq, width), lambda b, i: (rb + b * n_qblk + i, colblk))
    kspec = lambda width, colblk: pl.BlockSpec((lp, width), lambda b, i: (b, colblk))
    kern = functools.partial(_dsa_kernel, nq=nq, kblk=kblk, n_keys=n_keys, q_pos0=q_pos0, q_step=q_step,
                             n_sel=n_sel, idx_bits=max(1, int(np.ceil(np.log2(lp)))))
    vmem = (2 * 2 * lp * (2 * KV_B * DH_B + 2 * LANES) + 2 * nkb * nq * kblk * 4
            + H_B * nq * (DH_B * 6 + 2 * LANES * 4) + 4 * nq * (qq.shape[1] + D_MODEL) * 2) // 2 ** 20 + 8
    operands = [qq, qq, x32, k, v, ka, kz]
    in_specs = [qspec(H_B * DH_B, 0), qspec(H_IDX * D_IDX, H_B * DH_B // (H_IDX * D_IDX)), qspec(LANES, 2),
                kspec(KV_B * DH_B, 0), kspec(KV_B * DH_B, 0), kspec(LANES, ka_col), kspec(LANES, kz_col)]
    aliases = {}
    if o_all is not None:
        operands.append(o_all)
        in_specs.append(pl.BlockSpec(memory_space=pl.ANY))
        aliases = {len(operands) - 1: 0}
    return pl.pallas_call(
        kern,
        grid=(n_batch, n_qblk),
        in_specs=in_specs,
        out_specs=pl.BlockSpec((nq, D_MODEL), lambda b, i: (rb + b * n_qblk + i, 0)),
        out_shape=jax.ShapeDtypeStruct((qq.shape[0], D_MODEL), BF16),
        input_output_aliases=aliases,
        scratch_shapes=[pltpu.VMEM((nkb, nq, kblk), F32),
                        pltpu.VMEM((nkb, nq, kblk), F32), pltpu.VMEM((H_B * nq, DH_B), BF16),
                        pltpu.VMEM((H_B * nq, 1), F32), pltpu.VMEM((H_B * nq, 1), F32),
                        pltpu.VMEM((H_B * nq, DH_B), F32)],
        compiler_params=_params(("parallel", "arbitrary"), vmem),
        name="dsa",
    )(*operands)


def _sample_keys_kernel(ck_ref, cv_ref, ci_ref, kn_ref, vn_ref, xn_ref, k_ref, v_ref, ka_ref, kz_ref,
                        *, past, s_len):
    end = past + s_len
    pad = k_ref.shape[0] - end
    k_ref[0:past, :] = ck_ref[...].astype(BF16)
    k_ref[past:end, :] = kn_ref[...]
    k_ref[end:, :] = jnp.zeros((pad, k_ref.shape[1]), BF16)
    v_ref[0:past, :] = cv_ref[...].astype(BF16)
    v_ref[past:end, :] = vn_ref[...]
    v_ref[end:, :] = jnp.zeros((pad, v_ref.shape[1]), BF16)
    ci = ci_ref[...]
    zero = jnp.zeros((past, LANES - D_IDX), F32)
    ka_ref[0:past, :] = jnp.concatenate([ci, zero], axis=1).astype(BF16)
    kz_ref[0:past, :] = jnp.concatenate([zero, ci], axis=1).astype(BF16)
    ka_ref[past:end, :] = xn_ref[:, 0:LANES]
    kz_ref[past:end, :] = xn_ref[:, LANES:2 * LANES]
    ka_ref[end:, :] = jnp.zeros((pad, LANES), BF16)
    kz_ref[end:, :] = jnp.zeros((pad, LANES), BF16)


def _sample_keys(cache_k, cache_v, cache_ki, k16, v16, x16, *, row0, n_batch, past, s_len, lp):
    rb = row0 // s_len
    nk = KV_B * DH_B
    old = lambda width: pl.BlockSpec((past, width), lambda b: (b, 0))
    new = lambda width: pl.BlockSpec((s_len, width), lambda b: (rb + b, 0))
    out = lambda width: pl.BlockSpec((lp, width), lambda b: (b, 0))
    return pl.pallas_call(
        functools.partial(_sample_keys_kernel, past=past, s_len=s_len),
        grid=(n_batch,),
        in_specs=[old(nk), old(nk), old(D_IDX), new(nk), new(nk), new(B_XW)],
        out_specs=[out(nk), out(nk), out(LANES), out(LANES)],
        out_shape=[jax.ShapeDtypeStruct((n_batch * lp, nk), BF16), jax.ShapeDtypeStruct((n_batch * lp, nk), BF16),
                   jax.ShapeDtypeStruct((n_batch * lp, LANES), BF16),
                   jax.ShapeDtypeStruct((n_batch * lp, LANES), BF16)],
        compiler_params=_params(("parallel",), 24),
        name="sample_keys",
    )(cache_k, cache_v, cache_ki, k16, v16, x16)


def _conv_kernel(u_ref, st_ref, w_ref, o_ref, ext_ref, *, t, cw, rows):
    j = pl.program_id(1)
    n_qk = C_QK_DIM // cw
    ext_ref[0:SUBLANES, :] = st_ref[0]
    ext_ref[SUBLANES:SUBLANES + t, :] = u_ref[...]
    w = w_ref[...]
    off = SUBLANES - (CONV_W - 1)
    scale = jnp.where(j < n_qk, DK_C ** -0.5, 1.0)

    def body(r, carry):
        base = pl.multiple_of(r * rows, rows)
        blk = ext_ref[pl.ds(base, rows + SUBLANES), :]
        acc = blk[SUBLANES:, :] * w[CONV_W - 1:CONV_W, :]
        for jj in range(CONV_W - 1):
            tap = pltpu.roll(blk, SUBLANES - off - jj, 0)[SUBLANES:, :]
            acc = acc + tap * w[jj:jj + 1, :]
        a = acc * _sigmoid(acc)
        normed = []
        for hh in range(cw // DK_C):
            ah = a[:, hh * DK_C:(hh + 1) * DK_C]
            normed.append(ah * (lax.rsqrt(jnp.sum(ah * ah, axis=-1, keepdims=True) + RMS_EPS) * scale))
        an = jnp.concatenate(normed, axis=-1) if len(normed) > 1 else normed[0]
        o_ref[pl.ds(base, rows), :] = jnp.where(j < 2 * n_qk, an, a)
        return carry

    lax.fori_loop(0, t // rows, body, 0)


def _conv(u_all, state8, conv_w, *, row0, n_batch, t, cw, rows):
    rb = row0 // t
    return pl.pallas_call(
        functools.partial(_conv_kernel, t=t, cw=cw, rows=rows),
        grid=(n_batch, C_CONV_DIM // cw),
        in_specs=[pl.BlockSpec((t, cw), lambda b, j: (rb + b, j)),
                  pl.BlockSpec((1, SUBLANES, cw), lambda b, j: (b, 0, j)),
                  pl.BlockSpec((CONV_W, cw), lambda b, j: (0, j))],
        out_specs=pl.BlockSpec((t, cw), lambda b, j: (b, j)),
        out_shape=jax.ShapeDtypeStruct((n_batch * t, C_CONV_DIM), F32),
        scratch_shapes=[pltpu.VMEM((t + SUBLANES, cw), F32)],
        compiler_params=_params(("parallel", "parallel"), 6 * t * cw * 4 // 2 ** 20 + 8),
        name="conv",
    )(u_all, state8, conv_w)


def _gates_kernel(a_ref, at_ref, b_ref, alog_ref, alogt_ref, dt_ref, dtt_ref, gc_ref, gr_ref, beta_ref, *, c, tb):
    def decay(a_raw, a_log, dt):
        x = a_raw + dt
        softplus = jnp.maximum(x, 0.0) + jnp.log1p(jnp.exp(-jnp.abs(x)))
        return -jnp.exp(a_log) * softplus

    def split3(x):
        p0 = x.astype(BF16)
        r = x - p0.astype(F32)
        p1 = r.astype(BF16)
        p2 = (r - p1.astype(F32)).astype(BF16)
        return p0, p1, p2

    i = lax.broadcasted_iota(jnp.int32, (tb, tb), 0)
    j = lax.broadcasted_iota(jnp.int32, (tb, tb), 1)
    same = (i // c) == (j // c)
    lower = jnp.where(same & (j <= i), 1.0, 0.0).astype(BF16)
    upper = jnp.where(same & (i <= j), 1.0, 0.0).astype(BF16)
    g = decay(a_ref[...], alog_ref[...], dt_ref[...])
    gt = decay(at_ref[...], alogt_ref[...], dtt_ref[...])
    g0, g1, g2 = split3(g)
    gc_ref[...] = _dot(lower, g0) + (_dot(lower, g1) + _dot(lower, g2))
    t0, t1, t2 = split3(gt)
    gr_ref[...] = _dot(t0, upper) + (_dot(t1, upper) + _dot(t2, upper))
    beta_ref[...] = _sigmoid(b_ref[...])


def _gates(a_raw, b_raw, a_log, dt_bias, c, tb):
    n = a_raw.shape[0]
    tok = pl.BlockSpec((tb, HV_C), lambda i: (i, 0))
    hed = pl.BlockSpec((HV_C, tb), lambda i: (0, i))
    row = pl.BlockSpec((1, HV_C), lambda i: (0, 0))
    colv = pl.BlockSpec((HV_C, 1), lambda i: (0, 0))
    return pl.pallas_call(
        functools.partial(_gates_kernel, c=c, tb=tb),
        grid=(n // tb,),
        in_specs=[tok, hed, tok, row, colv, row, colv],
        out_specs=[tok, hed, tok],
        out_shape=[jax.ShapeDtypeStruct((n, HV_C), F32), jax.ShapeDtypeStruct((HV_C, n), F32),
                   jax.ShapeDtypeStruct((n, HV_C), F32)],
        compiler_params=_params(("parallel",), 16),
        name="gates",
    )(a_raw, a_raw.T, b_raw, a_log.reshape(1, HV_C), a_log.reshape(HV_C, 1),
      dt_bias.reshape(1, HV_C), dt_bias.reshape(HV_C, 1))


def _delta_kernel(q_ref, k_ref, v_ref, z_ref, gc_ref, gr_ref, beta_ref, s0_ref, nw_ref, *refs, c, hg):
    o_ref, s_ref = refs[-2:]

    @pl.when(pl.program_id(2) == 0)
    def _():
        s_ref[...] = s0_ref[...]

    i = lax.broadcasted_iota(jnp.int32, (c, c), 0)
    j = lax.broadcasted_iota(jnp.int32, (c, c), 1)
    eye = jnp.where(i == j, 1.0, 0.0)
    rep = HV_C // HK_C
    heads = range(hg)
    gc_all, gr_all, beta_all = gc_ref[0, 0], gr_ref[0, 0], beta_ref[0, 0]
    q = [q_ref[:, (n // rep) * DK_C:(n // rep + 1) * DK_C] for n in heads]
    k = [k_ref[:, (n // rep) * DK_C:(n // rep + 1) * DK_C] for n in heads]
    v = [v_ref[:, n * DV_C:(n + 1) * DV_C] for n in heads]
    z = [z_ref[:, n * DV_C:(n + 1) * DV_C] for n in heads]
    s_old = [s_ref[n] for n in heads]
    gcol = [gc_all[:, n:n + 1] for n in heads]
    beta = [beta_all[:, n:n + 1] for n in heads]
    d_incl = [jnp.exp(jnp.where(i >= j, gcol[n] - gr_all[n:n + 1, :], NEG_INF)) for n in heads]
    kb = [k[n] * beta[n] for n in heads]
    kh = [k[n].astype(BF16) for n in heads]
    m = [_dot_nt(kb[n].astype(BF16), kh[n]) * jnp.where(i > j, d_incl[n], 0.0) for n in heads]
    t = [eye - jnp.where(i // 2 == j // 2, m[n], 0.0) for n in heads]
    s = 2
    while s < c:
        join = (i // (2 * s) == j // (2 * s)) & (i // s != j // s)
        tb = [t[n].astype(BF16) for n in heads]
        x = [_dot(tb[n], jnp.where(join, m[n], 0.0).astype(BF16)) for n in heads]
        t = [t[n] - _dot(x[n].astype(BF16), tb[n]) for n in heads]
        s *= 2
    e_g = [jnp.exp(gcol[n]) for n in heads]
    sol = [_dot(t[n].astype(BF16), jnp.concatenate([v[n] * beta[n], kb[n] * e_g[n]], axis=-1).astype(BF16))
           for n in heads]
    s_bf = [s_old[n].astype(BF16) for n in heads]
    ub = [(sol[n][:, :DV_C] - _dot(sol[n][:, DV_C:].astype(BF16), s_bf[n])).astype(BF16) for n in heads]
    qk = [(_dot_nt(q[n].astype(BF16), kh[n]) * d_incl[n]).astype(BF16) for n in heads]
    o = [_dot((q[n] * e_g[n]).astype(BF16), s_bf[n]) + _dot(qk[n], ub[n]) for n in heads]
    g_last = [gcol[n][c - 1:c, :] for n in heads]
    s_new = [s_old[n] * jnp.exp(g_last[n]) + _dot_tn((k[n] * jnp.exp(g_last[n] - gcol[n])).astype(BF16), ub[n])
             for n in heads]
    o = [o[n] * lax.rsqrt(jnp.mean(o[n] * o[n], axis=-1, keepdims=True) + RMS_EPS) * nw_ref[...] for n in heads]
    o = [(o[n] * (z[n] * _sigmoid(z[n]))).astype(BF16) for n in heads]
    for n in heads:
        s_ref[n] = s_new[n]
        o_ref[:, n * DV_C:(n + 1) * DV_C] = o[n]


def _delta(qkv_act, z_all, gc, gr, beta, s0, norm_w, o_all, *, row0, z_col0, n_batch, t, c, hg):
    nc = t // c
    nhg = HV_C // hg
    rep = HV_C // HK_C
    qw = hg // rep * DK_C
    rb = row0 // c
    vmem = 24
    operands = [qkv_act, qkv_act, qkv_act, z_all, gc, gr, beta, s0, norm_w.reshape(1, DV_C)]
    in_specs = [pl.BlockSpec((c, qw), lambda b, g, ci: (b * nc + ci, g)),
                pl.BlockSpec((c, qw), lambda b, g, ci: (b * nc + ci, C_QK_DIM // qw + g)),
                pl.BlockSpec((c, hg * DV_C), lambda b, g, ci: (b * nc + ci, 2 * C_QK_DIM // (hg * DV_C) + g)),
                pl.BlockSpec((c, hg * DV_C), lambda b, g, ci: (rb + b * nc + ci, z_col0 // (hg * DV_C) + g)),
                pl.BlockSpec((1, 1, c, hg), lambda b, g, ci: (b * nc + ci, g, 0, 0)),
                pl.BlockSpec((1, 1, hg, c), lambda b, g, ci: (b * nc + ci, g, 0, 0)),
                pl.BlockSpec((1, 1, c, hg), lambda b, g, ci: (b * nc + ci, g, 0, 0)),
                pl.BlockSpec((hg, DK_C, DV_C), lambda b, g, ci: (b * nhg + g, 0, 0)),
                pl.BlockSpec((1, DV_C), lambda b, g, ci: (0, 0))]
    aliases = {}
    if o_all is not None:
        operands.append(o_all)
        in_specs.append(pl.BlockSpec(memory_space=pl.ANY))
        aliases = {len(operands) - 1: 0}
    return pl.pallas_call(
        functools.partial(_delta_kernel, c=c, hg=hg),
        grid=(n_batch, nhg, nc),
        in_specs=in_specs,
        out_specs=[pl.BlockSpec((c, hg * DV_C), lambda b, g, ci: (rb + b * nc + ci, g)),
                   pl.BlockSpec((hg, DK_C, DV_C), lambda b, g, ci: (b * nhg + g, 0, 0))],
        out_shape=[jax.ShapeDtypeStruct((z_all.shape[0], C_V_DIM), BF16),
                   jax.ShapeDtypeStruct((n_batch * HV_C, DK_C, DV_C), F32)],
        input_output_aliases=aliases,
        compiler_params=_params(("parallel", "parallel", "arbitrary"), vmem),
        name="delta",
    )(*operands)


def _chunk_layout(x, n_chunks, c, nhg, hg, head_major):
    if head_major:
        return x.reshape(nhg, hg, n_chunks, c).transpose(2, 0, 1, 3)
    return x.reshape(n_chunks, c, nhg, hg).transpose(0, 2, 1, 3)


TM = 512
C_HG = 8
B_QBLK = 128
B_KBLK = 512
B_KBLK_S = 384


def _pad_cols(w, width):
    return jnp.pad(w, ((0, 0), (0, width - w.shape[1])))


def kernel(x_prompt, x_sample, cache_a_k, cache_a_v, cache_b_k, cache_b_v, cache_b_kidx, state_c_conv,
           state_c_ssm, a_w_in, a_rel_bias, a_w_out, b_w_in, b_w_out, c_w_in, c_conv_w, c_a_log, c_dt_bias,
           c_norm_w, c_w_out, ln1_g, ln1_b, mlp_w1, mlp_w2, ln2_g, ln2_b):
    nb_p, t_p, _ = x_prompt.shape
    nb_s, t_s, _ = x_sample.shape
    mp = nb_p * t_p
    ms = nb_s * t_s
    y = jnp.concatenate([x_prompt.reshape(mp, D_MODEL), x_sample.reshape(ms, D_MODEL)], axis=0)
    outs = {n: [] for n in ("akp", "avp", "aks", "avs", "bkp", "bvp", "bip", "bks", "bvs", "bis",
                            "ccp", "csp", "ccs", "css")}
    for i in range(DEPTH):
        kind, j = i % N_MIXERS, i // N_MIXERS
        if kind == 0:
            assert t_p % A_QBLK == 0 and t_p >= LEFT_CHUNKS * CHUNK == A_QBLK
            qkv = _mm(y, a_w_in[j], TM, 1536)
            past = cache_a_k.shape[2]
            win = -(-(past + t_s) // LANES) * LANES
            mix_in, k_tail, v_tail = _attn_a_prompt(qkv, _a_bias_rows(a_rel_bias[j], LEFT_CHUNKS * CHUNK),
                                                    _a_band_prompt(), nb_p, t_p)
            mix_in, k_new, v_new = _attn_a_sample(qkv, cache_a_k[j].reshape(nb_s * past, D_MODEL),
                                                  cache_a_v[j].reshape(nb_s * past, D_MODEL),
                                                  _a_bias_rows(a_rel_bias[j], past), _a_band_sample(past, t_s, win),
                                                  mix_in, mp, nb_s, t_s)
            outs["akp"].append(k_tail.reshape(nb_p, A_QBLK, H_A, DH_A))
            outs["avp"].append(v_tail.reshape(nb_p, A_QBLK, H_A, DH_A))
            outs["aks"].append(k_new.reshape(nb_s, t_s, H_A, DH_A))
            outs["avs"].append(v_new.reshape(nb_s, t_s, H_A, DH_A))
            w_out = a_w_out[j]
        elif kind == 1:
            nq_, nk_, ni_ = H_B * DH_B, KV_B * DH_B, H_IDX * D_IDX
            w = b_w_in[j]
            w_q, w_k, w_v = w[:, :nq_], w[:, nq_:nq_ + nk_], w[:, nq_ + nk_:nq_ + 2 * nk_]
            w_qi = w[:, nq_ + 2 * nk_:nq_ + 2 * nk_ + ni_]
            w_ki = w[:, nq_ + 2 * nk_ + ni_:nq_ + 2 * nk_ + ni_ + D_IDX]
            w_wi = w[:, nq_ + 2 * nk_ + ni_ + D_IDX:]
            zc = lambda n: jnp.zeros((D_MODEL, n), F32)
            w_x = jnp.concatenate([w_ki, zc(LANES - D_IDX), zc(LANES - D_IDX), w_ki,
                                   w_wi, zc(2 * LANES - H_IDX)], axis=1)
            qq = _mm(y, jnp.concatenate([w_q * (DH_B ** -0.5), w_qi], axis=1), TM, 1536, (BF16,))
            k32, k16 = _mm(y, w_k, TM, nk_, (F32, BF16))
            v32, v16 = _mm(y, w_v, TM, nk_, (F32, BF16))
            x32, x16 = _mm(y, w_x, TM, B_XW, (F32, BF16))
            outs["bkp"].append(k32[:mp].reshape(nb_p, t_p, KV_B, DH_B))
            outs["bvp"].append(v32[:mp].reshape(nb_p, t_p, KV_B, DH_B))
            outs["bip"].append(x32[:mp, :D_IDX].reshape(nb_p, t_p, D_IDX))
            outs["bks"].append(k32[mp:].reshape(nb_s, t_s, KV_B, DH_B))
            outs["bvs"].append(v32[mp:].reshape(nb_s, t_s, KV_B, DH_B))
            outs["bis"].append(x32[mp:, :D_IDX].reshape(nb_s, t_s, D_IDX))
            mix_in = _dsa(qq, x32, k16, v16, x16, x16, None, ka_col=0, kz_col=1, row0=0, n_batch=nb_p, lp=t_p,
                          nq=B_QBLK, n_qblk=t_p // B_QBLK, kblk=B_KBLK, n_keys=t_p, q_pos0=0, q_step=B_QBLK)
            past = cache_b_k.shape[2]
            n_keys = past + t_s
            lp = -(-n_keys // B_KBLK_S) * B_KBLK_S
            k_s, v_s, ka_s, kz_s = _sample_keys(
                cache_b_k[j].reshape(nb_s * past, nk_), cache_b_v[j].reshape(nb_s * past, nk_),
                cache_b_kidx[j].reshape(nb_s * past, D_IDX), k16, v16, x16,
                row0=mp, n_batch=nb_s, past=past, s_len=t_s, lp=lp)
            mix_in = _dsa(qq, x32, k_s, v_s, ka_s, kz_s, mix_in, ka_col=0, kz_col=0, row0=mp, n_batch=nb_s, lp=lp,
                          nq=t_s, n_qblk=1, kblk=B_KBLK_S, n_keys=n_keys, q_pos0=PAST_LEN, q_step=0)
            w_out = b_w_out[j]
        else:
            w = c_w_in[j]
            n_main = C_CONV_DIM + C_V_DIM
            proj = _mm(y, w[:, :n_main], TM, 1536)
            gate_raw = _mm_x3(y, _pad_cols(w[:, n_main:], LANES), TM)
            b_raw, a_raw = gate_raw[:, :HV_C], gate_raw[:, HV_C:2 * HV_C]
            nhg = HV_C // C_HG
            mix_in = None
            for (row0, nbt, tt, state, s0, names) in (
                    (0, nb_p, t_p, jnp.zeros((nb_p, CONV_W - 1, C_CONV_DIM), F32),
                     jnp.zeros((nb_p * HV_C, DK_C, DV_C), F32), ("ccp", "csp")),
                    (mp, nb_s, t_s, state_c_conv[j], state_c_ssm[j].reshape(nb_s * HV_C, DK_C, DV_C), ("ccs", "css"))):
                c = min(CHUNK, tt)
                n_rows = nbt * tt
                n_new = min(tt, CONV_W - 1)
                u_tail = proj[row0:row0 + n_rows].reshape(nbt, tt, -1)[:, tt - n_new:, :C_CONV_DIM]
                outs[names[0]].append(jnp.concatenate([state[:, n_new:], u_tail], axis=1))
                state8 = jnp.pad(state, ((0, 0), (SUBLANES - (CONV_W - 1), 0), (0, 0)))
                act = _conv(proj, state8, c_conv_w[j], row0=row0, n_batch=nbt, t=tt, cw=256, rows=min(tt, 256))
                gc, gr, beta = _gates(a_raw[row0:row0 + n_rows], b_raw[row0:row0 + n_rows], c_a_log[j], c_dt_bias[j],
                                      c, 512)
                n_chunks = n_rows // c
                mix_in, s_new = _delta(act, proj, _chunk_layout(gc, n_chunks, c, nhg, C_HG, False),
                                       _chunk_layout(gr, n_chunks, c, nhg, C_HG, True),
                                       _chunk_layout(beta, n_chunks, c, nhg, C_HG, False), s0, c_norm_w[j], mix_in,
                                       row0=row0, z_col0=C_CONV_DIM, n_batch=nbt, t=tt, c=c, hg=C_HG)
                outs[names[1]].append(s_new.reshape(nbt, HV_C, DK_C, DV_C))
            w_out = c_w_out[j]
        y = _mm_ln(mix_in, w_out.astype(BF16), y, ln1_g[i], ln1_b[i], TM, 1024)
        y = _mlp_ln(y, mlp_w1[i].astype(BF16), mlp_w2[i].astype(BF16), ln2_g[i], ln2_b[i], TM, 1024)
    st = lambda name: jnp.stack(outs[name])
    return (y[:mp].reshape(nb_p, t_p, D_MODEL), y[mp:].reshape(nb_s, t_s, D_MODEL),
            st("akp"), st("avp"), st("aks"), st("avs"),
            st("bkp"), st("bvp"), st("bip"), st("bks"), st("bvs"), st("bis"),
            st("ccp"), st("csp"), st("ccs"), st("css"))
```

```python
import functools

import numpy as np
import jax
import jax.numpy as jnp
from jax import lax
from jax.experimental import pallas as pl
from jax.experimental.pallas import tpu as pltpu

D_MODEL = 2048
DEPTH = 4
PAST_LEN = 1024
CHUNK = 64
N_MIXERS = 3
D_FF = 4 * D_MODEL
H_A = 16
DH_A = D_MODEL // H_A
LEFT_CHUNKS = 8
REL_CLIP = 128
H_B = 16
KV_B = 4
DH_B = D_MODEL // H_B
H_IDX = 16
D_IDX = 64
TOPK_MAX = 256
HK_C = 16
HV_C = 32
DK_C = 128
DV_C = 128
CONV_W = 4
C_QK_DIM = HK_C * DK_C
C_V_DIM = HV_C * DV_C
C_CONV_DIM = 2 * C_QK_DIM + C_V_DIM
ALPHA = (2.0 * DEPTH) ** 0.25
LN_EPS = 1e-5
RMS_EPS = 1e-6

LANES = 128
SUBLANES = 8
VMEM_CAP_MB = 56

F32 = jnp.float32
BF16 = jnp.bfloat16
NEG_INF = float("-inf")
INT_MIN = -(2 ** 31)


def _params(sem, vmem_mb):
    return pltpu.CompilerParams(dimension_semantics=sem,
                                vmem_limit_bytes=min(vmem_mb, VMEM_CAP_MB) * 1024 * 1024)


def _dot(a, b):
    return jnp.dot(a, b, preferred_element_type=F32)


def _dot_nt(a, b):
    return lax.dot_general(a, b, (((1,), (1,)), ((), ())), preferred_element_type=F32)


def _dot_tn(a, b):
    return lax.dot_general(a, b, (((0,), (0,)), ((), ())), preferred_element_type=F32)


def _split2(a):
    hi = a.astype(BF16)
    lo = (a - hi.astype(F32)).astype(BF16)
    return hi, lo


def _dot_x3(a, b):
    ah, al = _split2(a)
    bh, bl = _split2(b)
    return _dot(ah, bh) + (_dot(ah, bl) + _dot(al, bh))


def _layer_norm(z, g, b):
    mu = jnp.mean(z, axis=-1, keepdims=True)
    zc = z - mu
    var = jnp.mean(zc * zc, axis=-1, keepdims=True)
    return zc * lax.rsqrt(var + LN_EPS) * g + b


def _sigmoid(x):
    return 0.5 * jnp.tanh(0.5 * x) + 0.5


def _mm_kernel(x_ref, w_ref, *refs, n_scaled, scale):
    *o_refs, wb_ref = refs

    @pl.when(pl.program_id(1) == 0)
    def _():
        wb_ref[...] = w_ref[0].astype(BF16)

    res = _dot(x_ref[...].astype(BF16), wb_ref[...])
    if n_scaled:
        res = res * jnp.where(pl.program_id(0) < n_scaled, scale, 1.0)
    for o_ref in o_refs:
        o_ref[...] = res.astype(o_ref.dtype)


def _mm(x, w, layer, tm, tn, n_tiles, dtypes=(F32,), col_of=lambda j: j, n_scaled=0, scale=1.0):
    m, k = x.shape
    out_b = sum(jnp.dtype(d).itemsize for d in dtypes)
    vmem = (2 * tm * k * 4 + tm * k * 2 + 2 * k * tn * 4 + k * tn * 2 + 2 * tm * tn * out_b
            + tm * tn * 4) // 2 ** 20 + 4
    outs = pl.pallas_call(
        functools.partial(_mm_kernel, n_scaled=n_scaled, scale=scale),
        grid=(n_tiles, m // tm),
        in_specs=[pl.BlockSpec((tm, k), lambda j, i: (i, 0)),
                  pl.BlockSpec((1, k, tn), lambda j, i: (layer, 0, col_of(j)))],
        out_specs=[pl.BlockSpec((tm, tn), lambda j, i: (i, j)) for _ in dtypes],
        out_shape=[jax.ShapeDtypeStruct((m, n_tiles * tn), d) for d in dtypes],
        scratch_shapes=[pltpu.VMEM((k, tn), BF16)],
        compiler_params=_params(("parallel", "arbitrary"), vmem),
        name="mm",
    )(x, w)
    return outs[0] if len(dtypes) == 1 else outs


def _mm_x3_kernel(x_ref, w_ref, o_ref):
    o_ref[...] = _dot_x3(x_ref[...], w_ref[...])


def _mm_x3(x, w, tm):
    m, k = x.shape
    n = w.shape[1]
    return pl.pallas_call(
        _mm_x3_kernel,
        grid=(m // tm,),
        in_specs=[pl.BlockSpec((tm, k), lambda i: (i, 0)),
                  pl.BlockSpec((k, n), lambda i: (0, 0))],
        out_specs=pl.BlockSpec((tm, n), lambda i: (i, 0)),
        out_shape=jax.ShapeDtypeStruct((m, n), F32),
        compiler_params=_params(("parallel",), 24),
        name="mm_x3",
    )(x, w)


def _mm_ln_kernel(x_ref, w_ref, r_ref, g_ref, b_ref, o_ref, acc_ref, *, rows):
    k = pl.program_id(1)

    @pl.when(k == 0)
    def _():
        acc_ref[...] = jnp.zeros_like(acc_ref)

    acc_ref[...] += _dot(x_ref[...], w_ref[0])

    @pl.when(k == pl.num_programs(1) - 1)
    def _():
        def norm_rows(r, carry):
            sl = pl.ds(pl.multiple_of(r * rows, rows), rows)
            o_ref[sl, :] = _layer_norm(ALPHA * r_ref[sl, :] + acc_ref[sl, :], g_ref[...], b_ref[...])
            return carry

        lax.fori_loop(0, o_ref.shape[0] // rows, norm_rows, 0)


def _mm_ln(x, w, layer, resid, g, b, tm, tk):
    m, k = x.shape
    d = w.shape[2]
    vmem = (2 * tm * tk * 2 + 2 * tk * d * 2 + 6 * tm * d * 4) // 2 ** 20 + 4
    return pl.pallas_call(
        functools.partial(_mm_ln_kernel, rows=tm // 4),
        grid=(m // tm, k // tk),
        in_specs=[pl.BlockSpec((tm, tk), lambda i, kk: (i, kk)),
                  pl.BlockSpec((1, tk, d), lambda i, kk: (layer, kk, 0)),
                  pl.BlockSpec((tm, d), lambda i, kk: (i, 0)),
                  pl.BlockSpec((1, d), lambda i, kk: (0, 0)),
                  pl.BlockSpec((1, d), lambda i, kk: (0, 0))],
        out_specs=pl.BlockSpec((tm, d), lambda i, kk: (i, 0)),
        out_shape=jax.ShapeDtypeStruct((m, d), F32),
        scratch_shapes=[pltpu.VMEM((tm, d), F32)],
        compiler_params=_params(("parallel", "arbitrary"), vmem),
        name="mm_ln",
    )(x, w, resid, g.reshape(1, d), b.reshape(1, d))


def _fill_kernel(o_ref):
    o_ref[...] = jnp.zeros(o_ref.shape, o_ref.dtype)


def _zeros(m, d, tm):
    return pl.pallas_call(
        _fill_kernel,
        grid=(m // tm,),
        out_specs=pl.BlockSpec((tm, d), lambda i: (i, 0)),
        out_shape=jax.ShapeDtypeStruct((m, d), BF16),
        compiler_params=_params(("parallel",), 2 * tm * d * 2 // 2 ** 20 + 4),
        name="fill",
    )()


def _mlp_ln_kernel(y_ref, w1_ref, w2_ref, g_ref, b_ref, o_ref, xb_ref, *, rows):
    f = pl.program_id(1)

    @pl.when(f == 0)
    def _():
        xb_ref[...] = y_ref[...].astype(BF16)
        o_ref[...] = jnp.zeros_like(o_ref)

    h = jnp.maximum(_dot(xb_ref[...], w1_ref[0].astype(BF16)), 0.0)
    o_ref[...] += _dot((h * h).astype(BF16), w2_ref[0].astype(BF16))

    @pl.when(f == pl.num_programs(1) - 1)
    def _():
        def norm_rows(r, carry):
            sl = pl.ds(pl.multiple_of(r * rows, rows), rows)
            o_ref[sl, :] = _layer_norm(ALPHA * y_ref[sl, :] + o_ref[sl, :], g_ref[...], b_ref[...])
            return carry

        lax.fori_loop(0, o_ref.shape[0] // rows, norm_rows, 0)


def _mlp_ln(y, w1, w2, layer, g, b, tm, tf):
    m, d = y.shape
    f = w1.shape[2]
    vmem = (2 * tm * d * 4 + tm * d * 2 + 4 * d * tf * 4 + 2 * d * tf * 2 + 3 * tm * tf * 4) // 2 ** 20 + 12
    once = pl.Buffered(1)
    return pl.pallas_call(
        functools.partial(_mlp_ln_kernel, rows=tm // 8),
        grid=(m // tm, f // tf),
        in_specs=[pl.BlockSpec((tm, d), lambda i, ff: (i, 0), pipeline_mode=once),
                  pl.BlockSpec((1, d, tf), lambda i, ff: (layer, 0, ff)),
                  pl.BlockSpec((1, tf, d), lambda i, ff: (layer, ff, 0)),
                  pl.BlockSpec((1, d), lambda i, ff: (0, 0)),
                  pl.BlockSpec((1, d), lambda i, ff: (0, 0))],
        out_specs=pl.BlockSpec((tm, d), lambda i, ff: (i, 0), pipeline_mode=once),
        out_shape=jax.ShapeDtypeStruct((m, d), F32),
        scratch_shapes=[pltpu.VMEM((tm, d), BF16)],
        compiler_params=_params(("parallel", "arbitrary"), vmem),
        name="mlp_ln",
    )(y, w1, w2, g.reshape(1, d), b.reshape(1, d))


A_QBLK = 512
A_PAIR = 2 * CHUNK
A_WIN = (LEFT_CHUNKS + 2) * CHUNK
A_UW = A_WIN + A_PAIR


def _softmax_pv(s, v):
    m = jnp.max(s, axis=-1, keepdims=True)
    p = jnp.exp(s - m)
    l = jnp.sum(p, axis=-1, keepdims=True)
    return _dot(p.astype(BF16), v) / l


def _bias_table(u_row, band, rows):
    u = jnp.broadcast_to(u_row, (rows, A_UW))
    return pltpu.roll(u, A_UW - A_PAIR + 1, 1, stride=1, stride_axis=0)[:, :band.shape[1]] + band


def _attn_a_prompt_kernel(q_ref, kp_ref, kc_ref, vp_ref, vc_ref, u_ref, band_ref, o_hbm_ref, o_ref, ko_ref, vo_ref,
                          kw_ref, vw_ref):
    del o_hbm_ref
    i = pl.program_id(2)

    @pl.when(i == pl.num_programs(2) - 1)
    def _():
        ko_ref[...] = kc_ref[...]
        vo_ref[...] = vc_ref[...]

    kw_ref[0:A_QBLK, :] = kp_ref[...].astype(BF16)
    kw_ref[A_QBLK:2 * A_QBLK, :] = kc_ref[...].astype(BF16)
    vw_ref[0:A_QBLK, :] = vp_ref[...].astype(BF16)
    vw_ref[A_QBLK:2 * A_QBLK, :] = vc_ref[...].astype(BF16)
    bias = _bias_table(u_ref[0], band_ref[...], A_PAIR)
    col = lax.broadcasted_iota(jnp.int32, (A_PAIR, A_WIN), 1)
    pairs = range(A_QBLK // A_PAIR)
    q = [(q_ref[c * A_PAIR:(c + 1) * A_PAIR, :] * (DH_A ** -0.5)).astype(BF16) for c in pairs]
    s = [_dot_nt(q[c], kw_ref[c * A_PAIR:c * A_PAIR + A_WIN, :]) + bias for c in pairs]
    s = [jnp.where(col + (i * A_QBLK - A_QBLK + c * A_PAIR) >= 0, s[c], NEG_INF) for c in pairs]
    o = [_softmax_pv(s[c], vw_ref[c * A_PAIR:c * A_PAIR + A_WIN, :]).astype(BF16) for c in pairs]
    for c in pairs:
        o_ref[c * A_PAIR:(c + 1) * A_PAIR, :] = o[c]


def _attn_a_prompt(qkv, u_rows, band, o_all, n_batch, t):
    nblk = t // A_QBLK

    def spec(col0, prev):
        if prev:
            return pl.BlockSpec((A_QBLK, DH_A), lambda b, h, i: (b * nblk + jnp.maximum(i - 1, 0), col0 + h))
        return pl.BlockSpec((A_QBLK, DH_A), lambda b, h, i: (b * nblk + i, col0 + h))

    tail = pl.BlockSpec((A_QBLK, DH_A), lambda b, h, i: (b, h))
    return pl.pallas_call(
        _attn_a_prompt_kernel,
        grid=(n_batch, H_A, nblk),
        in_specs=[spec(0, False), spec(H_A, True), spec(H_A, False), spec(2 * H_A, True), spec(2 * H_A, False),
                  pl.BlockSpec((1, 1, A_UW), lambda b, h, i: (h, 0, 0)),
                  pl.BlockSpec((A_PAIR, A_WIN), lambda b, h, i: (0, 0)),
                  pl.BlockSpec(memory_space=pl.ANY)],
        out_specs=[pl.BlockSpec((A_QBLK, DH_A), lambda b, h, i: (b * nblk + i, h)), tail, tail],
        out_shape=[jax.ShapeDtypeStruct(o_all.shape, BF16),
                   jax.ShapeDtypeStruct((n_batch * A_QBLK, D_MODEL), F32),
                   jax.ShapeDtypeStruct((n_batch * A_QBLK, D_MODEL), F32)],
        input_output_aliases={7: 0},
        scratch_shapes=[pltpu.VMEM((2 * A_QBLK, DH_A), BF16), pltpu.VMEM((2 * A_QBLK, DH_A), BF16)],
        compiler_params=_params(("parallel", "parallel", "arbitrary"), 16),
        name="attn_a_prompt",
    )(qkv, qkv, qkv, qkv, qkv, u_rows, band, o_all)


def _attn_a_sample_kernel(q_ref, kc_ref, kn_ref, vc_ref, vn_ref, u_ref, band_ref, o_hbm_ref, o_ref, ko_ref, vo_ref,
                          kw_ref, vw_ref, *, past, s_len):
    del o_hbm_ref
    ko_ref[...] = kn_ref[...]
    vo_ref[...] = vn_ref[...]
    win = kw_ref.shape[0]
    pad = win - past - s_len
    for h in range(H_A):
        cols = slice(h * DH_A, (h + 1) * DH_A)
        kw_ref[0:past, :] = kc_ref[:, cols].astype(BF16)
        kw_ref[past:past + s_len, :] = kn_ref[:, cols].astype(BF16)
        kw_ref[past + s_len:, :] = jnp.zeros((pad, DH_A), BF16)
        vw_ref[0:past, :] = vc_ref[:, cols].astype(BF16)
        vw_ref[past:past + s_len, :] = vn_ref[:, cols].astype(BF16)
        vw_ref[past + s_len:, :] = jnp.zeros((pad, DH_A), BF16)
        q = (q_ref[:, cols] * (DH_A ** -0.5)).astype(BF16)
        s = _dot_nt(q, kw_ref[...]) + _bias_table(u_ref[h], band_ref[...], s_len)
        o_ref[:, cols] = _softmax_pv(s, vw_ref[...]).astype(BF16)


def _attn_a_sample(qkv, k_cache, v_cache, u_rows, band, o_all, layer, past, row0, n_batch, s_len):
    win = band.shape[1]
    rb = row0 // s_len
    new = lambda colblk: pl.BlockSpec((s_len, D_MODEL), lambda b: (rb + b, colblk))
    old = pl.BlockSpec((past, D_MODEL), lambda b: (layer * n_batch + b, 0))
    rows = pl.BlockSpec((s_len, D_MODEL), lambda b: (b, 0))
    return pl.pallas_call(
        functools.partial(_attn_a_sample_kernel, past=past, s_len=s_len),
        grid=(n_batch,),
        in_specs=[new(0), old, new(1), old, new(2),
                  pl.BlockSpec((H_A, 1, A_UW), lambda b: (0, 0, 0)),
                  pl.BlockSpec((s_len, win), lambda b: (0, 0)),
                  pl.BlockSpec(memory_space=pl.ANY)],
        out_specs=[pl.BlockSpec((s_len, D_MODEL), lambda b: (rb + b, 0)), rows, rows],
        out_shape=[jax.ShapeDtypeStruct(o_all.shape, BF16),
                   jax.ShapeDtypeStruct((n_batch * s_len, D_MODEL), F32),
                   jax.ShapeDtypeStruct((n_batch * s_len, D_MODEL), F32)],
        input_output_aliases={7: 0},
        scratch_shapes=[pltpu.VMEM((win, DH_A), BF16), pltpu.VMEM((win, DH_A), BF16)],
        compiler_params=_params(("parallel",), 32),
        name="attn_a_sample",
    )(qkv, k_cache, qkv, v_cache, qkv, u_rows, band, o_all)


def _a_bias_rows(rel_bias, q0):
    d = q0 + A_PAIR - 1 - np.arange(A_UW)
    return rel_bias[:, np.clip(d, -REL_CLIP, REL_CLIP) + REL_CLIP][:, None, :]


def _a_band_prompt():
    qc = LEFT_CHUNKS * CHUNK + np.arange(A_PAIR)[:, None]
    kc = np.arange(A_WIN)[None, :]
    vis = (kc // CHUNK <= qc // CHUNK) & (kc // CHUNK >= qc // CHUNK - LEFT_CHUNKS)
    return jnp.asarray(np.where(vis, 0.0, -np.inf).astype(np.float32))


def _a_band_sample(past, s_len, win):
    q_pos = PAST_LEN + np.arange(s_len)[:, None]
    k_pos = PAST_LEN - past + np.arange(win)[None, :]
    vis = ((k_pos >= 0) & (k_pos // CHUNK <= q_pos // CHUNK) & (k_pos // CHUNK >= q_pos // CHUNK - LEFT_CHUNKS)
           & (np.arange(win)[None, :] < past + s_len))
    return jnp.asarray(np.where(vis, 0.0, -np.inf).astype(np.float32))


B_XW = 4 * LANES


def _key_to_float(key):
    return lax.bitcast_convert_type(jnp.where(key >= 0, key, key ^ jnp.int32(0x7FFFFFFF)), F32)


KEY_NEG_INF = int(np.array(-np.inf, np.float32).view(np.int32)) ^ 0x7FFFFFFF


def _lane_fold(x):
    acc = x[:, 0:LANES]
    for t in range(1, x.shape[1] // LANES):
        acc = acc + x[:, t * LANES:(t + 1) * LANES]
    return acc


def _dsa_kernel(q_ref, qi_ref, wi_ref, k_ref, v_ref, ka_ref, kb_ref, *refs,
                nq, kblk, n_keys, q_pos0, q_step, n_sel, idx_bits):
    o_ref, sc_ref, msk_ref, qs_ref, m_ref, l_ref, acc_ref = refs[-7:]
    qbase = q_pos0 + pl.program_id(1) * q_step
    n_adm = jnp.minimum(((qbase + nq - 1) // CHUNK + 1) * CHUNK, n_keys)
    nb = (n_adm + kblk - 1) // kblk
    qpos = qbase + lax.broadcasted_iota(jnp.int32, (nq, kblk), 0)
    col0 = lax.broadcasted_iota(jnp.int32, (nq, kblk), 1)

    wsc = wi_ref[...] * (H_IDX ** -0.5 * D_IDX ** -0.5)

    def score_block(kb, carry):
        r = pl.multiple_of(kb * kblk, kblk)
        ka = ka_ref[pl.ds(r, kblk), :]
        kz = kb_ref[pl.ds(r, kblk), :]
        acc = jnp.zeros((nq, kblk), F32)
        for p in range(H_IDX // 2):
            q2 = qi_ref[:, p * LANES:(p + 1) * LANES]
            acc = acc + wsc[:, 2 * p:2 * p + 1] * jnp.maximum(_dot_nt(q2, ka), 0.0)
            acc = acc + wsc[:, 2 * p + 1:2 * p + 2] * jnp.maximum(_dot_nt(q2, kz), 0.0)
        col = col0 + kb * kblk
        adm = (col // CHUNK <= qpos // CHUNK) & (col < n_keys)
        sc_ref[kb] = jnp.where(adm, acc, NEG_INF)
        return carry

    lax.fori_loop(0, nb, score_block, 0)

    def count(pred_fn):
        def body(kb, c):
            return c + _lane_fold(jnp.where(pred_fn(kb, sc_ref[kb]), 1.0, 0.0))
        c = lax.fori_loop(0, nb, body, jnp.zeros((nq, LANES), F32))
        return jnp.sum(c, axis=-1, keepdims=True)

    def bit_step(bi, pre):
        cand = pre | jnp.left_shift(jnp.int32(1), 31 - bi)
        cand_s = cand ^ jnp.int32(INT_MIN)
        thr_c = _key_to_float(cand_s)
        cnt = count(lambda kb, sc: sc >= thr_c)
        return jnp.where((cnt >= n_sel) | (cand_s <= KEY_NEG_INF), cand, pre)

    pre = lax.fori_loop(0, 32, bit_step, jnp.zeros((nq, 1), jnp.int32))
    thr = _key_to_float(pre ^ jnp.int32(INT_MIN))
    need = n_sel - count(lambda kb, sc: sc > thr)
    n_ge = count(lambda kb, sc: sc >= thr)

    def idx_step(bi, lim):
        cand = lim | jnp.left_shift(jnp.int32(1), idx_bits - 1 - bi)
        cnt = count(lambda kb, sc: (sc == thr) & (col0 + kb * kblk < cand))
        return jnp.where(cnt < need, cand, lim)

    tied = jnp.max(jnp.where(thr > NEG_INF, n_ge, 0.0)) > n_sel
    lim = lax.cond(tied,
                   lambda: lax.fori_loop(0, idx_bits, idx_step, jnp.zeros((nq, 1), jnp.int32)),
                   lambda: jnp.full((nq, 1), 2 ** idx_bits, jnp.int32))

    def mask_block(kb, carry):
        sc = sc_ref[kb]
        sel = (sc > thr) | ((sc == thr) & (col0 + kb * kblk <= lim))
        sel = sel & (sc > NEG_INF) & (sc < float("inf"))
        msk_ref[kb] = jnp.where(sel, 0.0, NEG_INF)
        return carry

    lax.fori_loop(0, nb, mask_block, 0)

    group = H_B // KV_B
    gq = group * nq
    for h in range(H_B):
        qs_ref[h * nq:(h + 1) * nq, :] = q_ref[:, h * DH_B:(h + 1) * DH_B]
    m_ref[...] = jnp.full(m_ref.shape, NEG_INF, F32)
    l_ref[...] = jnp.zeros(l_ref.shape, F32)
    acc_ref[...] = jnp.zeros(acc_ref.shape, F32)

    def attend(kb, carry):
        r = pl.multiple_of(kb * kblk, kblk)
        msk = msk_ref[kb]
        s = [_dot_nt(qs_ref[g * gq:(g + 1) * gq, :], k_ref[pl.ds(r, kblk), g * DH_B:(g + 1) * DH_B])
             for g in range(KV_B)]
        for g in range(KV_B):
            ps = []
            for hq in range(group):
                rows = slice((g * group + hq) * nq, (g * group + hq + 1) * nq)
                sh = s[g][hq * nq:(hq + 1) * nq] + msk
                m_old = m_ref[rows]
                m_new = jnp.maximum(m_old, jnp.max(sh, axis=-1, keepdims=True))
                m_use = jnp.where(m_new == NEG_INF, 0.0, m_new)
                p = jnp.exp(sh - m_use)
                a = jnp.exp(m_old - m_use)
                m_ref[rows] = m_new
                l_ref[rows] = a * l_ref[rows] + jnp.sum(p, axis=-1, keepdims=True)
                acc_ref[rows] = a * acc_ref[rows]
                ps.append(p.astype(BF16))
            acc_ref[g * gq:(g + 1) * gq, :] += _dot(jnp.concatenate(ps, axis=0),
                                                    v_ref[pl.ds(r, kblk), g * DH_B:(g + 1) * DH_B])
        return carry

    lax.fori_loop(0, nb, attend, 0)
    for h in range(H_B):
        rows = slice(h * nq, (h + 1) * nq)
        o_ref[:, h * DH_B:(h + 1) * DH_B] = (acc_ref[rows] / l_ref[rows]).astype(BF16)


def _dsa(qq, x32, k, v, ka, kz, o_all, *, ka_col, kz_col, row0, n_batch, lp, nq, n_qblk, kblk, n_keys,
         q_pos0, q_step):
    nkb = lp // kblk
    n_sel = min(TOPK_MAX, n_keys // 4)
    rb = row0 // nq
    qspec = lambda width, colblk: pl.BlockSpec((nq, width), lambda b, i: (rb + b * n_qblk + i, colblk))
    kspec = lambda width, colblk: pl.BlockSpec((lp, width), lambda b, i: (b, colblk))
    kern = functools.partial(_dsa_kernel, nq=nq, kblk=kblk, n_keys=n_keys, q_pos0=q_pos0, q_step=q_step,
                             n_sel=n_sel, idx_bits=max(1, int(np.ceil(np.log2(lp)))))
    vmem = (2 * 2 * lp * (2 * KV_B * DH_B + 2 * LANES) + 2 * nkb * nq * kblk * 4
            + H_B * nq * (DH_B * 6 + 2 * LANES * 4) + 4 * nq * (qq.shape[1] + D_MODEL) * 2) // 2 ** 20 + 8
    operands = [qq, qq, x32, k, v, ka, kz]
    in_specs = [qspec(H_B * DH_B, 0), qspec(H_IDX * D_IDX, H_B * DH_B // (H_IDX * D_IDX)), qspec(LANES, 2),
                kspec(KV_B * DH_B, 0), kspec(KV_B * DH_B, 0), kspec(LANES, ka_col), kspec(LANES, kz_col)]
    aliases = {}
    if o_all is not None:
        operands.append(o_all)
        in_specs.append(pl.BlockSpec(memory_space=pl.ANY))
        aliases = {len(operands) - 1: 0}
    return pl.pallas_call(
        kern,
        grid=(n_batch, n_qblk),
        in_specs=in_specs,
        out_specs=pl.BlockSpec((nq, D_MODEL), lambda b, i: (rb + b * n_qblk + i, 0)),
        out_shape=jax.ShapeDtypeStruct((qq.shape[0], D_MODEL), BF16),
        input_output_aliases=aliases,
        scratch_shapes=[pltpu.VMEM((nkb, nq, kblk), F32),
                        pltpu.VMEM((nkb, nq, kblk), F32), pltpu.VMEM((H_B * nq, DH_B), BF16),
                        pltpu.VMEM((H_B * nq, 1), F32), pltpu.VMEM((H_B * nq, 1), F32),
                        pltpu.VMEM((H_B * nq, DH_B), F32)],
        compiler_params=_params(("parallel", "arbitrary"), vmem),
        name="dsa",
    )(*operands)


def _sample_keys_kernel(ck_ref, cv_ref, ci_ref, kn_ref, vn_ref, xn_ref, k_ref, v_ref, ka_ref, kz_ref,
                        *, past, s_len):
    end = past + s_len
    pad = k_ref.shape[0] - end
    k_ref[0:past, :] = ck_ref[...].astype(BF16)
    k_ref[past:end, :] = kn_ref[...]
    k_ref[end:, :] = jnp.zeros((pad, k_ref.shape[1]), BF16)
    v_ref[0:past, :] = cv_ref[...].astype(BF16)
    v_ref[past:end, :] = vn_ref[...]
    v_ref[end:, :] = jnp.zeros((pad, v_ref.shape[1]), BF16)
    ci = ci_ref[...]
    zero = jnp.zeros((past, LANES - D_IDX), F32)
    ka_ref[0:past, :] = jnp.concatenate([ci, zero], axis=1).astype(BF16)
    kz_ref[0:past, :] = jnp.concatenate([zero, ci], axis=1).astype(BF16)
    ka_ref[past:end, :] = xn_ref[:, 0:LANES]
    kz_ref[past:end, :] = xn_ref[:, LANES:2 * LANES]
    ka_ref[end:, :] = jnp.zeros((pad, LANES), BF16)
    kz_ref[end:, :] = jnp.zeros((pad, LANES), BF16)


def _sample_keys(cache_k, cache_v, cache_ki, k16, v16, x16, *, row0, n_batch, past, s_len, lp):
    rb = row0 // s_len
    nk = KV_B * DH_B
    old = lambda width: pl.BlockSpec((past, width), lambda b: (b, 0))
    new = lambda width: pl.BlockSpec((s_len, width), lambda b: (rb + b, 0))
    out = lambda width: pl.BlockSpec((lp, width), lambda b: (b, 0))
    return pl.pallas_call(
        functools.partial(_sample_keys_kernel, past=past, s_len=s_len),
        grid=(n_batch,),
        in_specs=[old(nk), old(nk), old(D_IDX), new(nk), new(nk), new(B_XW)],
        out_specs=[out(nk), out(nk), out(LANES), out(LANES)],
        out_shape=[jax.ShapeDtypeStruct((n_batch * lp, nk), BF16), jax.ShapeDtypeStruct((n_batch * lp, nk), BF16),
                   jax.ShapeDtypeStruct((n_batch * lp, LANES), BF16),
                   jax.ShapeDtypeStruct((n_batch * lp, LANES), BF16)],
        compiler_params=_params(("parallel",), 24),
        name="sample_keys",
    )(cache_k, cache_v, cache_ki, k16, v16, x16)


def _conv_kernel(u_ref, st_ref, w_ref, o_ref, ext_ref, *, t, cw, rows):
    j = pl.program_id(1)
    n_qk = C_QK_DIM // cw
    ext_ref[0:SUBLANES, :] = st_ref[0]
    ext_ref[SUBLANES:SUBLANES + t, :] = u_ref[...]
    w = w_ref[...]
    off = SUBLANES - (CONV_W - 1)
    scale = jnp.where(j < n_qk, DK_C ** -0.5, 1.0)

    def body(r, carry):
        base = pl.multiple_of(r * rows, rows)
        blk = ext_ref[pl.ds(base, rows + SUBLANES), :]
        acc = blk[SUBLANES:, :] * w[CONV_W - 1:CONV_W, :]
        for jj in range(CONV_W - 1):
            tap = pltpu.roll(blk, SUBLANES - off - jj, 0)[SUBLANES:, :]
            acc = acc + tap * w[jj:jj + 1, :]
        a = acc * _sigmoid(acc)
        normed = []
        for hh in range(cw // DK_C):
            ah = a[:, hh * DK_C:(hh + 1) * DK_C]
            normed.append(ah * (lax.rsqrt(jnp.sum(ah * ah, axis=-1, keepdims=True) + RMS_EPS) * scale))
        an = jnp.concatenate(normed, axis=-1) if len(normed) > 1 else normed[0]
        o_ref[pl.ds(base, rows), :] = jnp.where(j < 2 * n_qk, an, a)
        return carry

    lax.fori_loop(0, t // rows, body, 0)


def _conv(u_all, state8, conv_w, *, row0, n_batch, t, cw, rows):
    rb = row0 // t
    return pl.pallas_call(
        functools.partial(_conv_kernel, t=t, cw=cw, rows=rows),
        grid=(n_batch, C_CONV_DIM // cw),
        in_specs=[pl.BlockSpec((t, cw), lambda b, j: (rb + b, j)),
                  pl.BlockSpec((1, SUBLANES, cw), lambda b, j: (b, 0, j)),
                  pl.BlockSpec((CONV_W, cw), lambda b, j: (0, j))],
        out_specs=pl.BlockSpec((t, cw), lambda b, j: (b, j)),
        out_shape=jax.ShapeDtypeStruct((n_batch * t, C_CONV_DIM), F32),
        scratch_shapes=[pltpu.VMEM((t + SUBLANES, cw), F32)],
        compiler_params=_params(("parallel", "parallel"), 6 * t * cw * 4 // 2 ** 20 + 8),
        name="conv",
    )(u_all, state8, conv_w)


def _gates_kernel(a_ref, at_ref, b_ref, alog_ref, alogt_ref, dt_ref, dtt_ref, gc_ref, gr_ref, beta_ref, *, c, tb):
    def decay(a_raw, a_log, dt):
        x = a_raw + dt
        softplus = jnp.maximum(x, 0.0) + jnp.log1p(jnp.exp(-jnp.abs(x)))
        return -jnp.exp(a_log) * softplus

    def split3(x):
        p0 = x.astype(BF16)
        r = x - p0.astype(F32)
        p1 = r.astype(BF16)
        p2 = (r - p1.astype(F32)).astype(BF16)
        return p0, p1, p2

    i = lax.broadcasted_iota(jnp.int32, (tb, tb), 0)
    j = lax.broadcasted_iota(jnp.int32, (tb, tb), 1)
    same = (i // c) == (j // c)
    lower = jnp.where(same & (j <= i), 1.0, 0.0).astype(BF16)
    upper = jnp.where(same & (i <= j), 1.0, 0.0).astype(BF16)
    g = decay(a_ref[...], alog_ref[...], dt_ref[...])
    gt = decay(at_ref[...], alogt_ref[...], dtt_ref[...])
    g0, g1, g2 = split3(g)
    gc_ref[...] = _dot(lower, g0) + (_dot(lower, g1) + _dot(lower, g2))
    t0, t1, t2 = split3(gt)
    gr_ref[...] = _dot(t0, upper) + (_dot(t1, upper) + _dot(t2, upper))
    beta_ref[...] = _sigmoid(b_ref[...])


def _gates(a_raw, b_raw, a_log, dt_bias, c, tb):
    n = a_raw.shape[0]
    tok = pl.BlockSpec((tb, HV_C), lambda i: (i, 0))
    hed = pl.BlockSpec((HV_C, tb), lambda i: (0, i))
    row = pl.BlockSpec((1, HV_C), lambda i: (0, 0))
    colv = pl.BlockSpec((HV_C, 1), lambda i: (0, 0))
    return pl.pallas_call(
        functools.partial(_gates_kernel, c=c, tb=tb),
        grid=(n // tb,),
        in_specs=[tok, hed, tok, row, colv, row, colv],
        out_specs=[tok, hed, tok],
        out_shape=[jax.ShapeDtypeStruct((n, HV_C), F32), jax.ShapeDtypeStruct((HV_C, n), F32),
                   jax.ShapeDtypeStruct((n, HV_C), F32)],
        compiler_params=_params(("parallel",), 16),
        name="gates",
    )(a_raw, a_raw.T, b_raw, a_log.reshape(1, HV_C), a_log.reshape(HV_C, 1),
      dt_bias.reshape(1, HV_C), dt_bias.reshape(HV_C, 1))


def _delta_kernel(q_ref, k_ref, v_ref, z_ref, gc_ref, gr_ref, beta_ref, s0_ref, nw_ref, *refs, c, hg):
    o_ref, s_ref = refs[-2:]

    @pl.when(pl.program_id(2) == 0)
    def _():
        s_ref[...] = s0_ref[...]

    i = lax.broadcasted_iota(jnp.int32, (c, c), 0)
    j = lax.broadcasted_iota(jnp.int32, (c, c), 1)
    eye = jnp.where(i == j, 1.0, 0.0)
    rep = HV_C // HK_C
    heads = range(hg)
    gc_all, gr_all, beta_all = gc_ref[0, 0], gr_ref[0, 0], beta_ref[0, 0]
    q = [q_ref[:, (n // rep) * DK_C:(n // rep + 1) * DK_C] for n in heads]
    k = [k_ref[:, (n // rep) * DK_C:(n // rep + 1) * DK_C] for n in heads]
    v = [v_ref[:, n * DV_C:(n + 1) * DV_C] for n in heads]
    z = [z_ref[:, n * DV_C:(n + 1) * DV_C] for n in heads]
    s_old = [s_ref[n] for n in heads]
    gcol = [gc_all[:, n:n + 1] for n in heads]
    beta = [beta_all[:, n:n + 1] for n in heads]
    d_incl = [jnp.exp(jnp.where(i >= j, gcol[n] - gr_all[n:n + 1, :], NEG_INF)) for n in heads]
    kb = [k[n] * beta[n] for n in heads]
    kh = [k[n].astype(BF16) for n in heads]
    m = [_dot_nt(kb[n].astype(BF16), kh[n]) * jnp.where(i > j, d_incl[n], 0.0) for n in heads]
    t = [eye - jnp.where(i // 2 == j // 2, m[n], 0.0) for n in heads]
    s = 2
    while s < c:
        join = (i // (2 * s) == j // (2 * s)) & (i // s != j // s)
        tb = [t[n].astype(BF16) for n in heads]
        x = [_dot(tb[n], jnp.where(join, m[n], 0.0).astype(BF16)) for n in heads]
        t = [t[n] - _dot(x[n].astype(BF16), tb[n]) for n in heads]
        s *= 2
    e_g = [jnp.exp(gcol[n]) for n in heads]
    sol = [_dot(t[n].astype(BF16), jnp.concatenate([v[n] * beta[n], kb[n] * e_g[n]], axis=-1).astype(BF16))
           for n in heads]
    s_bf = [s_old[n].astype(BF16) for n in heads]
    ub = [(sol[n][:, :DV_C] - _dot(sol[n][:, DV_C:].astype(BF16), s_bf[n])).astype(BF16) for n in heads]
    qk = [(_dot_nt(q[n].astype(BF16), kh[n]) * d_incl[n]).astype(BF16) for n in heads]
    o = [_dot((q[n] * e_g[n]).astype(BF16), s_bf[n]) + _dot(qk[n], ub[n]) for n in heads]
    g_last = [gcol[n][c - 1:c, :] for n in heads]
    s_new = [s_old[n] * jnp.exp(g_last[n]) + _dot_tn((k[n] * jnp.exp(g_last[n] - gcol[n])).astype(BF16), ub[n])
             for n in heads]
    o = [o[n] * lax.rsqrt(jnp.mean(o[n] * o[n], axis=-1, keepdims=True) + RMS_EPS) * nw_ref[...] for n in heads]
    o = [(o[n] * (z[n] * _sigmoid(z[n]))).astype(BF16) for n in heads]
    for n in heads:
        s_ref[n] = s_new[n]
        o_ref[:, n * DV_C:(n + 1) * DV_C] = o[n]


def _delta(qkv_act, z_all, gc, gr, beta, s0, norm_w, o_all, *, row0, z_col0, n_batch, t, c, hg):
    nc = t // c
    nhg = HV_C // hg
    rep = HV_C // HK_C
    qw = hg // rep * DK_C
    rb = row0 // c
    vmem = 24
    operands = [qkv_act, qkv_act, qkv_act, z_all, gc, gr, beta, s0, norm_w.reshape(1, DV_C)]
    in_specs = [pl.BlockSpec((c, qw), lambda b, g, ci: (b * nc + ci, g)),
                pl.BlockSpec((c, qw), lambda b, g, ci: (b * nc + ci, C_QK_DIM // qw + g)),
                pl.BlockSpec((c, hg * DV_C), lambda b, g, ci: (b * nc + ci, 2 * C_QK_DIM // (hg * DV_C) + g)),
                pl.BlockSpec((c, hg * DV_C), lambda b, g, ci: (rb + b * nc + ci, z_col0 // (hg * DV_C) + g)),
                pl.BlockSpec((1, 1, c, hg), lambda b, g, ci: (b * nc + ci, g, 0, 0)),
                pl.BlockSpec((1, 1, hg, c), lambda b, g, ci: (b * nc + ci, g, 0, 0)),
                pl.BlockSpec((1, 1, c, hg), lambda b, g, ci: (b * nc + ci, g, 0, 0)),
                pl.BlockSpec((hg, DK_C, DV_C), lambda b, g, ci: (b * nhg + g, 0, 0)),
                pl.BlockSpec((1, DV_C), lambda b, g, ci: (0, 0))]
    aliases = {}
    if o_all is not None:
        operands.append(o_all)
        in_specs.append(pl.BlockSpec(memory_space=pl.ANY))
        aliases = {len(operands) - 1: 0}
    return pl.pallas_call(
        functools.partial(_delta_kernel, c=c, hg=hg),
        grid=(n_batch, nhg, nc),
        in_specs=in_specs,
        out_specs=[pl.BlockSpec((c, hg * DV_C), lambda b, g, ci: (rb + b * nc + ci, g)),
                   pl.BlockSpec((hg, DK_C, DV_C), lambda b, g, ci: (b * nhg + g, 0, 0))],
        out_shape=[jax.ShapeDtypeStruct((z_all.shape[0], C_V_DIM), BF16),
                   jax.ShapeDtypeStruct((n_batch * HV_C, DK_C, DV_C), F32)],
        input_output_aliases=aliases,
        compiler_params=_params(("parallel", "parallel", "arbitrary"), vmem),
        name="delta",
    )(*operands)


def _chunk_layout(x, n_chunks, c, nhg, hg, head_major):
    if head_major:
        return x.reshape(nhg, hg, n_chunks, c).transpose(2, 0, 1, 3)
    return x.reshape(n_chunks, c, nhg, hg).transpose(0, 2, 1, 3)


TM = 512
TM_MLP = 1088
TF_MLP = 512
C_HG = 8
B_QBLK = 128
B_KBLK = 512
B_KBLK_S = 384


def _pad_cols(w, width):
    return jnp.pad(w, ((0, 0), (0, width - w.shape[1])))


def kernel(x_prompt, x_sample, cache_a_k, cache_a_v, cache_b_k, cache_b_v, cache_b_kidx, state_c_conv,
           state_c_ssm, a_w_in, a_rel_bias, a_w_out, b_w_in, b_w_out, c_w_in, c_conv_w, c_a_log, c_dt_bias,
           c_norm_w, c_w_out, ln1_g, ln1_b, mlp_w1, mlp_w2, ln2_g, ln2_b):
    nb_p, t_p, _ = x_prompt.shape
    nb_s, t_s, _ = x_sample.shape
    mp = nb_p * t_p
    ms = nb_s * t_s
    y = jnp.concatenate([x_prompt.reshape(mp, D_MODEL), x_sample.reshape(ms, D_MODEL)], axis=0)
    outs = {n: [] for n in ("akp", "avp", "aks", "avs", "bkp", "bvp", "bip", "bks", "bvs", "bis",
                            "ccp", "csp", "ccs", "css")}
    for i in range(DEPTH):
        kind, j = i % N_MIXERS, i // N_MIXERS
        if kind == 0:
            assert t_p % A_QBLK == 0 and t_p >= LEFT_CHUNKS * CHUNK == A_QBLK
            qkv = _mm(y, a_w_in, j, TM, 1536, 3 * D_MODEL // 1536)
            past = cache_a_k.shape[2]
            win = -(-(past + t_s) // LANES) * LANES
            mix_in, k_tail, v_tail = _attn_a_prompt(qkv, _a_bias_rows(a_rel_bias[j], LEFT_CHUNKS * CHUNK),
                                                    _a_band_prompt(), _zeros(mp + ms, D_MODEL, TM), nb_p, t_p)
            mix_in, k_new, v_new = _attn_a_sample(qkv, cache_a_k.reshape(-1, D_MODEL), cache_a_v.reshape(-1, D_MODEL),
                                                  _a_bias_rows(a_rel_bias[j], past), _a_band_sample(past, t_s, win),
                                                  mix_in, j, past, mp, nb_s, t_s)
            outs["akp"].append(k_tail.reshape(nb_p, A_QBLK, H_A, DH_A))
            outs["avp"].append(v_tail.reshape(nb_p, A_QBLK, H_A, DH_A))
            outs["aks"].append(k_new.reshape(nb_s, t_s, H_A, DH_A))
            outs["avs"].append(v_new.reshape(nb_s, t_s, H_A, DH_A))
            w_out = a_w_out
        elif kind == 1:
            nq_, nk_, ni_ = H_B * DH_B, KV_B * DH_B, H_IDX * D_IDX
            w = b_w_in[j]
            w_ki = w[:, nq_ + 2 * nk_ + ni_:nq_ + 2 * nk_ + ni_ + D_IDX]
            w_wi = w[:, nq_ + 2 * nk_ + ni_ + D_IDX:]
            zc = lambda n: jnp.zeros((D_MODEL, n), F32)
            w_x = jnp.concatenate([w_ki, zc(LANES - D_IDX), zc(LANES - D_IDX), w_ki,
                                   w_wi, zc(2 * LANES - H_IDX)], axis=1)
            assert nq_ == 2 * ni_ and 2 * nk_ == ni_
            qq = _mm(y, b_w_in, j, TM, ni_, 3, (BF16,), col_of=lambda t: jnp.where(t < 2, t, t + 1),
                     n_scaled=2, scale=DH_B ** -0.5)
            k32, k16 = _mm(y, b_w_in, j, TM, nk_, 1, (F32, BF16), col_of=lambda t: nq_ // nk_)
            v32, v16 = _mm(y, b_w_in, j, TM, nk_, 1, (F32, BF16), col_of=lambda t: nq_ // nk_ + 1)
            x32, x16 = _mm(y, w_x[None], 0, TM, B_XW, 1, (F32, BF16))
            outs["bkp"].append(k32[:mp].reshape(nb_p, t_p, KV_B, DH_B))
            outs["bvp"].append(v32[:mp].reshape(nb_p, t_p, KV_B, DH_B))
            outs["bip"].append(x32[:mp, :D_IDX].reshape(nb_p, t_p, D_IDX))
            outs["bks"].append(k32[mp:].reshape(nb_s, t_s, KV_B, DH_B))
            outs["bvs"].append(v32[mp:].reshape(nb_s, t_s, KV_B, DH_B))
            outs["bis"].append(x32[mp:, :D_IDX].reshape(nb_s, t_s, D_IDX))
            mix_in = _dsa(qq, x32, k16, v16, x16, x16, _zeros(mp + ms, D_MODEL, TM), ka_col=0, kz_col=1, row0=0,
                          n_batch=nb_p, lp=t_p, nq=B_QBLK, n_qblk=t_p // B_QBLK, kblk=B_KBLK, n_keys=t_p,
                          q_pos0=0, q_step=B_QBLK)
            past = cache_b_k.shape[2]
            n_keys = past + t_s
            lp = -(-n_keys // B_KBLK_S) * B_KBLK_S
            k_s, v_s, ka_s, kz_s = _sample_keys(
                cache_b_k[j].reshape(nb_s * past, nk_), cache_b_v[j].reshape(nb_s * past, nk_),
                cache_b_kidx[j].reshape(nb_s * past, D_IDX), k16, v16, x16,
                row0=mp, n_batch=nb_s, past=past, s_len=t_s, lp=lp)
            mix_in = _dsa(qq, x32, k_s, v_s, ka_s, kz_s, mix_in, ka_col=0, kz_col=0, row0=mp, n_batch=nb_s, lp=lp,
                          nq=t_s, n_qblk=1, kblk=B_KBLK_S, n_keys=n_keys, q_pos0=PAST_LEN, q_step=0)
            w_out = b_w_out
        else:
            n_main = C_CONV_DIM + C_V_DIM
            proj = _mm(y, c_w_in, j, TM, 1536, n_main // 1536)
            gate_raw = _mm_x3(y, _pad_cols(c_w_in[j][:, n_main:], LANES), TM)
            b_raw, a_raw = gate_raw[:, :HV_C], gate_raw[:, HV_C:2 * HV_C]
            nhg = HV_C // C_HG
            mix_in = _zeros(mp + ms, C_V_DIM, TM)
            for (row0, nbt, tt, state, s0, names) in (
                    (0, nb_p, t_p, jnp.zeros((nb_p, CONV_W - 1, C_CONV_DIM), F32),
                     jnp.zeros((nb_p * HV_C, DK_C, DV_C), F32), ("ccp", "csp")),
                    (mp, nb_s, t_s, state_c_conv[j], state_c_ssm[j].reshape(nb_s * HV_C, DK_C, DV_C), ("ccs", "css"))):
                c = min(CHUNK, tt)
                n_rows = nbt * tt
                n_new = min(tt, CONV_W - 1)
                u_tail = jnp.stack([proj[row0 + (b + 1) * tt - n_new:row0 + (b + 1) * tt, :C_CONV_DIM]
                                    for b in range(nbt)])
                outs[names[0]].append(jnp.concatenate([state[:, n_new:], u_tail], axis=1))
                state8 = jnp.pad(state, ((0, 0), (SUBLANES - (CONV_W - 1), 0), (0, 0)))
                act = _conv(proj, state8, c_conv_w[j], row0=row0, n_batch=nbt, t=tt, cw=256, rows=min(tt, 256))
                gc, gr, beta = _gates(a_raw[row0:row0 + n_rows], b_raw[row0:row0 + n_rows], c_a_log[j], c_dt_bias[j],
                                      c, 512)
                n_chunks = n_rows // c
                mix_in, s_new = _delta(act, proj, _chunk_layout(gc, n_chunks, c, nhg, C_HG, False),
                                       _chunk_layout(gr, n_chunks, c, nhg, C_HG, True),
                                       _chunk_layout(beta, n_chunks, c, nhg, C_HG, False), s0, c_norm_w[j], mix_in,
                                       row0=row0, z_col0=C_CONV_DIM, n_batch=nbt, t=tt, c=c, hg=C_HG)
                outs[names[1]].append(s_new.reshape(nbt, HV_C, DK_C, DV_C))
            w_out = c_w_out
        y = _mm_ln(mix_in, w_out.astype(BF16), j, y, ln1_g[i], ln1_b[i], TM, 1024)
        y = _mlp_ln(y, mlp_w1, mlp_w2, i, ln2_g[i], ln2_b[i], TM_MLP, TF_MLP)
    st = lambda name: jnp.stack(outs[name])
    return (y[:mp].reshape(nb_p, t_p, D_MODEL), y[mp:].reshape(nb_s, t_s, D_MODEL),
            st("akp"), st("avp"), st("aks"), st("avs"),
            st("bkp"), st("bvp"), st("bip"), st("bks"), st("bvs"), st("bis"),
            st("ccp"), st("csp"), st("ccs"), st("css"))
```

```python
import functools

import numpy as np
import jax
import jax.numpy as jnp
from jax import lax
from jax.experimental import pallas as pl
from jax.experimental.pallas import tpu as pltpu

D_MODEL = 2048
DEPTH = 4
PAST_LEN = 1024
CHUNK = 64
N_MIXERS = 3
D_FF = 4 * D_MODEL
H_A = 16
DH_A = D_MODEL // H_A
LEFT_CHUNKS = 8
REL_CLIP = 128
H_B = 16
KV_B = 4
DH_B = D_MODEL // H_B
H_IDX = 16
D_IDX = 64
TOPK_MAX = 256
HK_C = 16
HV_C = 32
DK_C = 128
DV_C = 128
CONV_W = 4
C_QK_DIM = HK_C * DK_C
C_V_DIM = HV_C * DV_C
C_CONV_DIM = 2 * C_QK_DIM + C_V_DIM
ALPHA = (2.0 * DEPTH) ** 0.25
LN_EPS = 1e-5
RMS_EPS = 1e-6

LANES = 128
SUBLANES = 8
VMEM_CAP_MB = 56

F32 = jnp.float32
BF16 = jnp.bfloat16
NEG_INF = float("-inf")
LOG2E = 1.4426950408889634
INT_MIN = -(2 ** 31)


def _params(sem, vmem_mb):
    return pltpu.CompilerParams(dimension_semantics=sem,
                                vmem_limit_bytes=min(vmem_mb, VMEM_CAP_MB) * 1024 * 1024)


def _dot(a, b):
    return jnp.dot(a, b, preferred_element_type=F32)


def _dot_nt(a, b):
    return lax.dot_general(a, b, (((1,), (1,)), ((), ())), preferred_element_type=F32)


def _dot_tn(a, b):
    return lax.dot_general(a, b, (((0,), (0,)), ((), ())), preferred_element_type=F32)


def _split2(a):
    hi = a.astype(BF16)
    lo = (a - hi.astype(F32)).astype(BF16)
    return hi, lo


def _dot_x3(a, b):
    ah, al = _split2(a)
    bh, bl = _split2(b)
    return _dot(ah, bh) + (_dot(ah, bl) + _dot(al, bh))


def _layer_norm(z, g, b):
    mu = jnp.mean(z, axis=-1, keepdims=True)
    zc = z - mu
    var = jnp.mean(zc * zc, axis=-1, keepdims=True)
    return zc * lax.rsqrt(var + LN_EPS) * g + b


def _sigmoid(x):
    return 0.5 * jnp.tanh(0.5 * x) + 0.5


def _mm_kernel(x_ref, w_ref, *refs, n_scaled, scale):
    *o_refs, wb_ref = refs

    @pl.when(pl.program_id(1) == 0)
    def _():
        wb_ref[...] = w_ref[0].astype(BF16)

    res = _dot(x_ref[...].astype(BF16), wb_ref[...])
    if n_scaled:
        res = res * jnp.where(pl.program_id(0) < n_scaled, scale, 1.0)
    for o_ref in o_refs:
        o_ref[...] = res.astype(o_ref.dtype)


def _mm(x, w, layer, tm, tn, n_tiles, dtypes=(F32,), col_of=lambda j: j, n_scaled=0, scale=1.0):
    m, k = x.shape
    out_b = sum(jnp.dtype(d).itemsize for d in dtypes)
    vmem = (2 * tm * k * 4 + tm * k * 2 + 2 * k * tn * 4 + k * tn * 2 + 2 * tm * tn * out_b
            + tm * tn * 4) // 2 ** 20 + 4
    outs = pl.pallas_call(
        functools.partial(_mm_kernel, n_scaled=n_scaled, scale=scale),
        grid=(n_tiles, m // tm),
        in_specs=[pl.BlockSpec((tm, k), lambda j, i: (i, 0)),
                  pl.BlockSpec((1, k, tn), lambda j, i: (layer, 0, col_of(j)))],
        out_specs=[pl.BlockSpec((tm, tn), lambda j, i: (i, j)) for _ in dtypes],
        out_shape=[jax.ShapeDtypeStruct((m, n_tiles * tn), d) for d in dtypes],
        scratch_shapes=[pltpu.VMEM((k, tn), BF16)],
        compiler_params=_params(("parallel", "arbitrary"), vmem),
        name="mm",
    )(x, w)
    return outs[0] if len(dtypes) == 1 else outs


def _mm_x3_kernel(x_ref, w_ref, o_ref):
    o_ref[...] = _dot_x3(x_ref[...], w_ref[...])


def _mm_x3(x, w, tm):
    m, k = x.shape
    n = w.shape[1]
    return pl.pallas_call(
        _mm_x3_kernel,
        grid=(m // tm,),
        in_specs=[pl.BlockSpec((tm, k), lambda i: (i, 0)),
                  pl.BlockSpec((k, n), lambda i: (0, 0))],
        out_specs=pl.BlockSpec((tm, n), lambda i: (i, 0)),
        out_shape=jax.ShapeDtypeStruct((m, n), F32),
        compiler_params=_params(("parallel",), 24),
        name="mm_x3",
    )(x, w)


def _mm_ln_kernel(x_ref, w_ref, r_ref, g_ref, b_ref, o_ref, acc_ref, *, rows):
    k = pl.program_id(1)

    @pl.when(k == 0)
    def _():
        acc_ref[...] = jnp.zeros_like(acc_ref)

    acc_ref[...] += _dot(x_ref[...], w_ref[0])

    @pl.when(k == pl.num_programs(1) - 1)
    def _():
        def norm_rows(r, carry):
            sl = pl.ds(pl.multiple_of(r * rows, rows), rows)
            o_ref[sl, :] = _layer_norm(ALPHA * r_ref[sl, :] + acc_ref[sl, :], g_ref[...], b_ref[...])
            return carry

        lax.fori_loop(0, o_ref.shape[0] // rows, norm_rows, 0)


def _mm_ln(x, w, layer, resid, g, b, tm, tk):
    m, k = x.shape
    d = w.shape[2]
    vmem = (2 * tm * tk * 2 + 2 * tk * d * 2 + 6 * tm * d * 4) // 2 ** 20 + 4
    return pl.pallas_call(
        functools.partial(_mm_ln_kernel, rows=tm // 4),
        grid=(m // tm, k // tk),
        in_specs=[pl.BlockSpec((tm, tk), lambda i, kk: (i, kk)),
                  pl.BlockSpec((1, tk, d), lambda i, kk: (layer, kk, 0)),
                  pl.BlockSpec((tm, d), lambda i, kk: (i, 0)),
                  pl.BlockSpec((1, d), lambda i, kk: (0, 0)),
                  pl.BlockSpec((1, d), lambda i, kk: (0, 0))],
        out_specs=pl.BlockSpec((tm, d), lambda i, kk: (i, 0)),
        out_shape=jax.ShapeDtypeStruct((m, d), F32),
        scratch_shapes=[pltpu.VMEM((tm, d), F32)],
        compiler_params=_params(("parallel", "arbitrary"), vmem),
        name="mm_ln",
    )(x, w, resid, g.reshape(1, d), b.reshape(1, d))


def _join_kernel(a_ref, b_ref, o_ref, *, n_a):
    i = pl.program_id(0)

    @pl.when(i < n_a)
    def _():
        o_ref[...] = a_ref[...]

    @pl.when(i >= n_a)
    def _():
        o_ref[...] = b_ref[...]


def _join_rows(a, b, tm):
    (ma, d), mb = a.shape, b.shape[0]
    n_a = ma // tm
    return pl.pallas_call(
        functools.partial(_join_kernel, n_a=n_a),
        grid=(n_a + mb // tm,),
        in_specs=[pl.BlockSpec((tm, d), lambda i: (jnp.minimum(i, n_a - 1), 0)),
                  pl.BlockSpec((tm, d), lambda i: (jnp.maximum(i - n_a, 0), 0))],
        out_specs=pl.BlockSpec((tm, d), lambda i: (i, 0)),
        out_shape=jax.ShapeDtypeStruct((ma + mb, d), a.dtype),
        compiler_params=_params(("arbitrary",), 6 * tm * d * a.dtype.itemsize // 2 ** 20 + 4),
        name="join_rows",
    )(a, b)


def _split_kernel(x_ref, a_ref, b_ref, *, n_a):
    i = pl.program_id(0)

    @pl.when(i < n_a)
    def _():
        a_ref[...] = x_ref[...]

    @pl.when(i >= n_a)
    def _():
        b_ref[...] = x_ref[...]


def _split_rows(x, ma, tm):
    m, d = x.shape
    n_a = ma // tm
    return pl.pallas_call(
        functools.partial(_split_kernel, n_a=n_a),
        grid=(m // tm,),
        in_specs=[pl.BlockSpec((tm, d), lambda i: (i, 0))],
        out_specs=[pl.BlockSpec((tm, d), lambda i: (jnp.minimum(i, n_a - 1), 0)),
                   pl.BlockSpec((tm, d), lambda i: (jnp.maximum(i - n_a, 0), 0))],
        out_shape=[jax.ShapeDtypeStruct((ma, d), x.dtype), jax.ShapeDtypeStruct((m - ma, d), x.dtype)],
        compiler_params=_params(("arbitrary",), 6 * tm * d * x.dtype.itemsize // 2 ** 20 + 4),
        name="split_rows",
    )(x)


def _fill_kernel(o_ref):
    o_ref[...] = jnp.zeros(o_ref.shape, o_ref.dtype)


def _zeros(m, d, tm):
    return pl.pallas_call(
        _fill_kernel,
        grid=(m // tm,),
        out_specs=pl.BlockSpec((tm, d), lambda i: (i, 0)),
        out_shape=jax.ShapeDtypeStruct((m, d), BF16),
        compiler_params=_params(("parallel",), 2 * tm * d * 2 // 2 ** 20 + 4),
        name="fill",
    )()


def _mlp_ln_kernel(y_ref, w1_ref, w2_ref, g_ref, b_ref, o_ref, xb_ref, *, rows):
    f = pl.program_id(1)

    @pl.when(f == 0)
    def _():
        xb_ref[...] = y_ref[...].astype(BF16)
        o_ref[...] = jnp.zeros_like(o_ref)

    h = jnp.maximum(_dot(xb_ref[...], w1_ref[0].astype(BF16)), 0.0)
    o_ref[...] += _dot((h * h).astype(BF16), w2_ref[0].astype(BF16))

    @pl.when(f == pl.num_programs(1) - 1)
    def _():
        def norm_rows(r, carry):
            sl = pl.ds(pl.multiple_of(r * rows, rows), rows)
            o_ref[sl, :] = _layer_norm(ALPHA * y_ref[sl, :] + o_ref[sl, :], g_ref[...], b_ref[...])
            return carry

        lax.fori_loop(0, o_ref.shape[0] // rows, norm_rows, 0)


def _mlp_ln(y, w1, w2, layer, g, b, tm, tf):
    m, d = y.shape
    f = w1.shape[2]
    vmem = (2 * tm * d * 4 + tm * d * 2 + 4 * d * tf * 4 + 2 * d * tf * 2 + 3 * tm * tf * 4) // 2 ** 20 + 12
    once = pl.Buffered(1)
    return pl.pallas_call(
        functools.partial(_mlp_ln_kernel, rows=tm // 8),
        grid=(m // tm, f // tf),
        in_specs=[pl.BlockSpec((tm, d), lambda i, ff: (i, 0), pipeline_mode=once),
                  pl.BlockSpec((1, d, tf), lambda i, ff: (layer, 0, ff)),
                  pl.BlockSpec((1, tf, d), lambda i, ff: (layer, ff, 0)),
                  pl.BlockSpec((1, d), lambda i, ff: (0, 0)),
                  pl.BlockSpec((1, d), lambda i, ff: (0, 0))],
        out_specs=pl.BlockSpec((tm, d), lambda i, ff: (i, 0), pipeline_mode=once),
        out_shape=jax.ShapeDtypeStruct((m, d), F32),
        scratch_shapes=[pltpu.VMEM((tm, d), BF16)],
        compiler_params=_params(("parallel", "arbitrary"), vmem),
        name="mlp_ln",
    )(y, w1, w2, g.reshape(1, d), b.reshape(1, d))


A_QBLK = 512
A_PAIR = 2 * CHUNK
A_WIN = (LEFT_CHUNKS + 2) * CHUNK
A_UW = A_WIN + A_PAIR


def _softmax_pv(s2, v):
    m = jnp.max(s2, axis=-1, keepdims=True)
    p = jnp.exp2(s2 - m)
    l = jnp.sum(p, axis=-1, keepdims=True)
    return _dot(p.astype(BF16), v) / l


def _bias_table(u_row, band, rows):
    u = jnp.broadcast_to(u_row * LOG2E, (rows, A_UW))
    return pltpu.roll(u, A_UW - A_PAIR + 1, 1, stride=1, stride_axis=0)[:, :band.shape[1]] + band


def _attn_a_prompt_kernel(q_ref, kp_ref, kc_ref, vp_ref, vc_ref, u_ref, band_ref, o_hbm_ref, o_ref, ko_ref, vo_ref,
                          kw_ref, vw_ref):
    del o_hbm_ref
    i = pl.program_id(2)

    @pl.when(i == pl.num_programs(2) - 1)
    def _():
        ko_ref[...] = kc_ref[...]
        vo_ref[...] = vc_ref[...]

    kw_ref[0:A_QBLK, :] = kp_ref[...].astype(BF16)
    kw_ref[A_QBLK:2 * A_QBLK, :] = kc_ref[...].astype(BF16)
    vw_ref[0:A_QBLK, :] = vp_ref[...].astype(BF16)
    vw_ref[A_QBLK:2 * A_QBLK, :] = vc_ref[...].astype(BF16)
    bias = _bias_table(u_ref[0], band_ref[...], A_PAIR)
    col = lax.broadcasted_iota(jnp.int32, (A_PAIR, A_WIN), 1)
    pairs = range(A_QBLK // A_PAIR)
    q = [(q_ref[c * A_PAIR:(c + 1) * A_PAIR, :] * (DH_A ** -0.5 * LOG2E)).astype(BF16) for c in pairs]
    s = [_dot_nt(q[c], kw_ref[c * A_PAIR:c * A_PAIR + A_WIN, :]) + bias for c in pairs]
    s = [jnp.where(col + (i * A_QBLK - A_QBLK + c * A_PAIR) >= 0, s[c], NEG_INF) for c in pairs]
    o = [_softmax_pv(s[c], vw_ref[c * A_PAIR:c * A_PAIR + A_WIN, :]).astype(BF16) for c in pairs]
    for c in pairs:
        o_ref[c * A_PAIR:(c + 1) * A_PAIR, :] = o[c]


def _attn_a_prompt(qkv, u_rows, band, o_all, n_batch, t):
    nblk = t // A_QBLK

    def spec(col0, prev):
        if prev:
            return pl.BlockSpec((A_QBLK, DH_A), lambda b, h, i: (b * nblk + jnp.maximum(i - 1, 0), col0 + h))
        return pl.BlockSpec((A_QBLK, DH_A), lambda b, h, i: (b * nblk + i, col0 + h))

    tail = pl.BlockSpec((A_QBLK, DH_A), lambda b, h, i: (b, h))
    return pl.pallas_call(
        _attn_a_prompt_kernel,
        grid=(n_batch, H_A, nblk),
        in_specs=[spec(0, False), spec(H_A, True), spec(H_A, False), spec(2 * H_A, True), spec(2 * H_A, False),
                  pl.BlockSpec((1, 1, A_UW), lambda b, h, i: (h, 0, 0)),
                  pl.BlockSpec((A_PAIR, A_WIN), lambda b, h, i: (0, 0)),
                  pl.BlockSpec(memory_space=pl.ANY)],
        out_specs=[pl.BlockSpec((A_QBLK, DH_A), lambda b, h, i: (b * nblk + i, h)), tail, tail],
        out_shape=[jax.ShapeDtypeStruct(o_all.shape, BF16),
                   jax.ShapeDtypeStruct((n_batch * A_QBLK, D_MODEL), F32),
                   jax.ShapeDtypeStruct((n_batch * A_QBLK, D_MODEL), F32)],
        input_output_aliases={7: 0},
        scratch_shapes=[pltpu.VMEM((2 * A_QBLK, DH_A), BF16), pltpu.VMEM((2 * A_QBLK, DH_A), BF16)],
        compiler_params=_params(("parallel", "parallel", "arbitrary"), 16),
        name="attn_a_prompt",
    )(qkv, qkv, qkv, qkv, qkv, u_rows, band, o_all)


def _attn_a_sample_kernel(q_ref, kc_ref, kn_ref, vc_ref, vn_ref, u_ref, band_ref, o_hbm_ref, o_ref, ko_ref, vo_ref,
                          kw_ref, vw_ref, *, past, s_len):
    del o_hbm_ref
    ko_ref[...] = kn_ref[...]
    vo_ref[...] = vn_ref[...]
    win = kw_ref.shape[0]
    pad = win - past - s_len
    for h in range(H_A):
        cols = slice(h * DH_A, (h + 1) * DH_A)
        kw_ref[0:past, :] = kc_ref[:, cols].astype(BF16)
        kw_ref[past:past + s_len, :] = kn_ref[:, cols].astype(BF16)
        kw_ref[past + s_len:, :] = jnp.zeros((pad, DH_A), BF16)
        vw_ref[0:past, :] = vc_ref[:, cols].astype(BF16)
        vw_ref[past:past + s_len, :] = vn_ref[:, cols].astype(BF16)
        vw_ref[past + s_len:, :] = jnp.zeros((pad, DH_A), BF16)
        q = (q_ref[:, cols] * (DH_A ** -0.5 * LOG2E)).astype(BF16)
        s = _dot_nt(q, kw_ref[...]) + _bias_table(u_ref[h], band_ref[...], s_len)
        o_ref[:, cols] = _softmax_pv(s, vw_ref[...]).astype(BF16)


def _attn_a_sample(qkv, k_cache, v_cache, u_rows, band, o_all, layer, past, row0, n_batch, s_len):
    win = band.shape[1]
    rb = row0 // s_len
    new = lambda colblk: pl.BlockSpec((s_len, D_MODEL), lambda b: (rb + b, colblk))
    old = pl.BlockSpec((past, D_MODEL), lambda b: (layer * n_batch + b, 0))
    rows = pl.BlockSpec((s_len, D_MODEL), lambda b: (b, 0))
    return pl.pallas_call(
        functools.partial(_attn_a_sample_kernel, past=past, s_len=s_len),
        grid=(n_batch,),
        in_specs=[new(0), old, new(1), old, new(2),
                  pl.BlockSpec((H_A, 1, A_UW), lambda b: (0, 0, 0)),
                  pl.BlockSpec((s_len, win), lambda b: (0, 0)),
                  pl.BlockSpec(memory_space=pl.ANY)],
        out_specs=[pl.BlockSpec((s_len, D_MODEL), lambda b: (rb + b, 0)), rows, rows],
        out_shape=[jax.ShapeDtypeStruct(o_all.shape, BF16),
                   jax.ShapeDtypeStruct((n_batch * s_len, D_MODEL), F32),
                   jax.ShapeDtypeStruct((n_batch * s_len, D_MODEL), F32)],
        input_output_aliases={7: 0},
        scratch_shapes=[pltpu.VMEM((win, DH_A), BF16), pltpu.VMEM((win, DH_A), BF16)],
        compiler_params=_params(("parallel",), 32),
        name="attn_a_sample",
    )(qkv, k_cache, qkv, v_cache, qkv, u_rows, band, o_all)


def _a_bias_rows(rel_bias, q0):
    d = q0 + A_PAIR - 1 - np.arange(A_UW)
    return rel_bias[:, np.clip(d, -REL_CLIP, REL_CLIP) + REL_CLIP][:, None, :]


def _a_band_prompt():
    qc = LEFT_CHUNKS * CHUNK + np.arange(A_PAIR)[:, None]
    kc = np.arange(A_WIN)[None, :]
    vis = (kc // CHUNK <= qc // CHUNK) & (kc // CHUNK >= qc // CHUNK - LEFT_CHUNKS)
    return jnp.asarray(np.where(vis, 0.0, -np.inf).astype(np.float32))


def _a_band_sample(past, s_len, win):
    q_pos = PAST_LEN + np.arange(s_len)[:, None]
    k_pos = PAST_LEN - past + np.arange(win)[None, :]
    vis = ((k_pos >= 0) & (k_pos // CHUNK <= q_pos // CHUNK) & (k_pos // CHUNK >= q_pos // CHUNK - LEFT_CHUNKS)
           & (np.arange(win)[None, :] < past + s_len))
    return jnp.asarray(np.where(vis, 0.0, -np.inf).astype(np.float32))


B_XW = 4 * LANES


def _key_to_float(key):
    return lax.bitcast_convert_type(jnp.where(key >= 0, key, key ^ jnp.int32(0x7FFFFFFF)), F32)


KEY_NEG_INF = int(np.array(-np.inf, np.float32).view(np.int32)) ^ 0x7FFFFFFF


def _lane_fold(x):
    acc = x[:, 0:LANES]
    for t in range(1, x.shape[1] // LANES):
        acc = acc + x[:, t * LANES:(t + 1) * LANES]
    return acc


def _dsa_kernel(q_ref, qi_ref, wi_ref, k_ref, v_ref, ka_ref, kb_ref, *refs,
                nq, kblk, n_keys, q_pos0, q_step, n_sel, idx_bits):
    o_ref, sc_ref, msk_ref, qs_ref, m_ref, l_ref, acc_ref = refs[-7:]
    qbase = q_pos0 + pl.program_id(1) * q_step
    n_adm = jnp.minimum(((qbase + nq - 1) // CHUNK + 1) * CHUNK, n_keys)
    nb = (n_adm + kblk - 1) // kblk
    qpos = qbase + lax.broadcasted_iota(jnp.int32, (nq, kblk), 0)
    col0 = lax.broadcasted_iota(jnp.int32, (nq, kblk), 1)

    wsc = wi_ref[...] * (H_IDX ** -0.5 * D_IDX ** -0.5)

    def score_block(kb, carry):
        r = pl.multiple_of(kb * kblk, kblk)
        ka = ka_ref[pl.ds(r, kblk), :]
        kz = kb_ref[pl.ds(r, kblk), :]
        acc = jnp.zeros((nq, kblk), F32)
        for p in range(H_IDX // 2):
            q2 = qi_ref[:, p * LANES:(p + 1) * LANES]
            acc = acc + wsc[:, 2 * p:2 * p + 1] * jnp.maximum(_dot_nt(q2, ka), 0.0)
            acc = acc + wsc[:, 2 * p + 1:2 * p + 2] * jnp.maximum(_dot_nt(q2, kz), 0.0)
        col = col0 + kb * kblk
        adm = (col // CHUNK <= qpos // CHUNK) & (col < n_keys)
        sc_ref[kb] = jnp.where(adm, acc, NEG_INF)
        return carry

    lax.fori_loop(0, nb, score_block, 0)

    def count(pred_fn):
        def body(kb, c):
            return c + _lane_fold(jnp.where(pred_fn(kb, sc_ref[kb]), 1.0, 0.0))
        c = lax.fori_loop(0, nb, body, jnp.zeros((nq, LANES), F32))
        return jnp.sum(c, axis=-1, keepdims=True)

    def bit_step(bi, pre):
        cand = pre | jnp.left_shift(jnp.int32(1), 31 - bi)
        cand_s = cand ^ jnp.int32(INT_MIN)
        thr_c = _key_to_float(cand_s)
        cnt = count(lambda kb, sc: sc >= thr_c)
        return jnp.where((cnt >= n_sel) | (cand_s <= KEY_NEG_INF), cand, pre)

    pre = lax.fori_loop(0, 32, bit_step, jnp.zeros((nq, 1), jnp.int32))
    thr = _key_to_float(pre ^ jnp.int32(INT_MIN))
    need = n_sel - count(lambda kb, sc: sc > thr)
    n_ge = count(lambda kb, sc: sc >= thr)

    def idx_step(bi, lim):
        cand = lim | jnp.left_shift(jnp.int32(1), idx_bits - 1 - bi)
        cnt = count(lambda kb, sc: (sc == thr) & (col0 + kb * kblk < cand))
        return jnp.where(cnt < need, cand, lim)

    tied = jnp.max(jnp.where(thr > NEG_INF, n_ge, 0.0)) > n_sel
    lim = lax.cond(tied,
                   lambda: lax.fori_loop(0, idx_bits, idx_step, jnp.zeros((nq, 1), jnp.int32)),
                   lambda: jnp.full((nq, 1), 2 ** idx_bits, jnp.int32))

    def mask_block(kb, carry):
        sc = sc_ref[kb]
        sel = (sc > thr) | ((sc == thr) & (col0 + kb * kblk <= lim))
        sel = sel & (sc > NEG_INF) & (sc < float("inf"))
        msk_ref[kb] = jnp.where(sel, 0.0, NEG_INF)
        return carry

    lax.fori_loop(0, nb, mask_block, 0)

    group = H_B // KV_B
    gq = group * nq
    for h in range(H_B):
        qs_ref[h * nq:(h + 1) * nq, :] = q_ref[:, h * DH_B:(h + 1) * DH_B]
    m_ref[...] = jnp.full(m_ref.shape, NEG_INF, F32)
    l_ref[...] = jnp.zeros(l_ref.shape, F32)
    acc_ref[...] = jnp.zeros(acc_ref.shape, F32)

    def attend(kb, carry):
        r = pl.multiple_of(kb * kblk, kblk)
        msk = msk_ref[kb]
        s = [_dot_nt(qs_ref[g * gq:(g + 1) * gq, :], k_ref[pl.ds(r, kblk), g * DH_B:(g + 1) * DH_B])
             for g in range(KV_B)]
        for g in range(KV_B):
            ps = []
            for hq in range(group):
                rows = slice((g * group + hq) * nq, (g * group + hq + 1) * nq)
                sh = s[g][hq * nq:(hq + 1) * nq] + msk
                m_old = m_ref[rows]
                m_new = jnp.maximum(m_old, jnp.max(sh, axis=-1, keepdims=True))
                m_use = jnp.where(m_new == NEG_INF, 0.0, m_new)
                p = jnp.exp2(sh - m_use)
                a = jnp.exp2(m_old - m_use)
                m_ref[rows] = m_new
                l_ref[rows] = a * l_ref[rows] + jnp.sum(p, axis=-1, keepdims=True)
                acc_ref[rows] = a * acc_ref[rows]
                ps.append(p.astype(BF16))
            acc_ref[g * gq:(g + 1) * gq, :] += _dot(jnp.concatenate(ps, axis=0),
                                                    v_ref[pl.ds(r, kblk), g * DH_B:(g + 1) * DH_B])
        return carry

    lax.fori_loop(0, nb, attend, 0)
    for h in range(H_B):
        rows = slice(h * nq, (h + 1) * nq)
        o_ref[:, h * DH_B:(h + 1) * DH_B] = (acc_ref[rows] / l_ref[rows]).astype(BF16)


def _dsa(qq, x32, k, v, ka, kz, o_all, *, ka_col, kz_col, row0, n_batch, lp, nq, n_qblk, kblk, n_keys,
         q_pos0, q_step):
    nkb = lp // kblk
    n_sel = min(TOPK_MAX, n_keys // 4)
    rb = row0 // nq
    qspec = lambda width, colblk: pl.BlockSpec((nq, width), lambda b, i: (rb + b * n_qblk + i, colblk))
    kspec = lambda width, colblk: pl.BlockSpec((lp, width), lambda b, i: (b, colblk))
    kern = functools.partial(_dsa_kernel, nq=nq, kblk=kblk, n_keys=n_keys, q_pos0=q_pos0, q_step=q_step,
                             n_sel=n_sel, idx_bits=max(1, int(np.ceil(np.log2(lp)))))
    vmem = (2 * 2 * lp * (2 * KV_B * DH_B + 2 * LANES) + 2 * nkb * nq * kblk * 4
            + H_B * nq * (DH_B * 6 + 2 * LANES * 4) + 4 * nq * (qq.shape[1] + D_MODEL) * 2) // 2 ** 20 + 8
    operands = [qq, qq, x32, k, v, ka, kz]
    in_specs = [qspec(H_B * DH_B, 0), qspec(H_IDX * D_IDX, H_B * DH_B // (H_IDX * D_IDX)), qspec(LANES, 2),
                kspec(KV_B * DH_B, 0), kspec(KV_B * DH_B, 0), kspec(LANES, ka_col), kspec(LANES, kz_col)]
    aliases = {}
    if o_all is not None:
        operands.append(o_all)
        in_specs.append(pl.BlockSpec(memory_space=pl.ANY))
        aliases = {len(operands) - 1: 0}
    return pl.pallas_call(
        kern,
        grid=(n_batch, n_qblk),
        in_specs=in_specs,
        out_specs=pl.BlockSpec((nq, D_MODEL), lambda b, i: (rb + b * n_qblk + i, 0)),
        out_shape=jax.ShapeDtypeStruct((qq.shape[0], D_MODEL), BF16),
        input_output_aliases=aliases,
        scratch_shapes=[pltpu.VMEM((nkb, nq, kblk), F32),
                        pltpu.VMEM((nkb, nq, kblk), F32), pltpu.VMEM((H_B * nq, DH_B), BF16),
                        pltpu.VMEM((H_B * nq, 1), F32), pltpu.VMEM((H_B * nq, 1), F32),
                        pltpu.VMEM((H_B * nq, DH_B), F32)],
        compiler_params=_params(("parallel", "arbitrary"), vmem),
        name="dsa",
    )(*operands)


def _sample_keys_kernel(ck_ref, cv_ref, ci_ref, kn_ref, vn_ref, xn_ref, k_ref, v_ref, ka_ref, kz_ref,
                        *, past, s_len):
    end = past + s_len
    pad = k_ref.shape[0] - end
    k_ref[0:past, :] = ck_ref[...].astype(BF16)
    k_ref[past:end, :] = kn_ref[...]
    k_ref[end:, :] = jnp.zeros((pad, k_ref.shape[1]), BF16)
    v_ref[0:past, :] = cv_ref[...].astype(BF16)
    v_ref[past:end, :] = vn_ref[...]
    v_ref[end:, :] = jnp.zeros((pad, v_ref.shape[1]), BF16)
    ci = ci_ref[...]
    zero = jnp.zeros((past, LANES - D_IDX), F32)
    ka_ref[0:past, :] = jnp.concatenate([ci, zero], axis=1).astype(BF16)
    kz_ref[0:past, :] = jnp.concatenate([zero, ci], axis=1).astype(BF16)
    ka_ref[past:end, :] = xn_ref[:, 0:LANES]
    kz_ref[past:end, :] = xn_ref[:, LANES:2 * LANES]
    ka_ref[end:, :] = jnp.zeros((pad, LANES), BF16)
    kz_ref[end:, :] = jnp.zeros((pad, LANES), BF16)


def _sample_keys(cache_k, cache_v, cache_ki, k16, v16, x16, *, row0, n_batch, past, s_len, lp):
    rb = row0 // s_len
    nk = KV_B * DH_B
    old = lambda width: pl.BlockSpec((past, width), lambda b: (b, 0))
    new = lambda width: pl.BlockSpec((s_len, width), lambda b: (rb + b, 0))
    out = lambda width: pl.BlockSpec((lp, width), lambda b: (b, 0))
    return pl.pallas_call(
        functools.partial(_sample_keys_kernel, past=past, s_len=s_len),
        grid=(n_batch,),
        in_specs=[old(nk), old(nk), old(D_IDX), new(nk), new(nk), new(B_XW)],
        out_specs=[out(nk), out(nk), out(LANES), out(LANES)],
        out_shape=[jax.ShapeDtypeStruct((n_batch * lp, nk), BF16), jax.ShapeDtypeStruct((n_batch * lp, nk), BF16),
                   jax.ShapeDtypeStruct((n_batch * lp, LANES), BF16),
                   jax.ShapeDtypeStruct((n_batch * lp, LANES), BF16)],
        compiler_params=_params(("parallel",), 24),
        name="sample_keys",
    )(cache_k, cache_v, cache_ki, k16, v16, x16)


def _conv_kernel(u_ref, st_ref, w_ref, o_ref, ext_ref, *, t, cw, rows):
    j = pl.program_id(1)
    n_qk = C_QK_DIM // cw
    ext_ref[0:SUBLANES, :] = st_ref[0]
    ext_ref[SUBLANES:SUBLANES + t, :] = u_ref[...]
    w = w_ref[...]
    off = SUBLANES - (CONV_W - 1)
    scale = jnp.where(j < n_qk, DK_C ** -0.5, 1.0)

    def body(r, carry):
        base = pl.multiple_of(r * rows, rows)
        blk = ext_ref[pl.ds(base, rows + SUBLANES), :]
        acc = blk[SUBLANES:, :] * w[CONV_W - 1:CONV_W, :]
        for jj in range(CONV_W - 1):
            tap = pltpu.roll(blk, SUBLANES - off - jj, 0)[SUBLANES:, :]
            acc = acc + tap * w[jj:jj + 1, :]
        a = acc * _sigmoid(acc)
        normed = []
        for hh in range(cw // DK_C):
            ah = a[:, hh * DK_C:(hh + 1) * DK_C]
            normed.append(ah * (lax.rsqrt(jnp.sum(ah * ah, axis=-1, keepdims=True) + RMS_EPS) * scale))
        an = jnp.concatenate(normed, axis=-1) if len(normed) > 1 else normed[0]
        o_ref[pl.ds(base, rows), :] = jnp.where(j < 2 * n_qk, an, a)
        return carry

    lax.fori_loop(0, t // rows, body, 0)


def _conv(u_all, state8, conv_w, *, row0, n_batch, t, cw, rows):
    rb = row0 // t
    return pl.pallas_call(
        functools.partial(_conv_kernel, t=t, cw=cw, rows=rows),
        grid=(n_batch, C_CONV_DIM // cw),
        in_specs=[pl.BlockSpec((t, cw), lambda b, j: (rb + b, j)),
                  pl.BlockSpec((1, SUBLANES, cw), lambda b, j: (b, 0, j)),
                  pl.BlockSpec((CONV_W, cw), lambda b, j: (0, j))],
        out_specs=pl.BlockSpec((t, cw), lambda b, j: (b, j)),
        out_shape=jax.ShapeDtypeStruct((n_batch * t, C_CONV_DIM), F32),
        scratch_shapes=[pltpu.VMEM((t + SUBLANES, cw), F32)],
        compiler_params=_params(("parallel", "parallel"), 6 * t * cw * 4 // 2 ** 20 + 8),
        name="conv",
    )(u_all, state8, conv_w)


def _gates_kernel(a_ref, at_ref, b_ref, alog_ref, alogt_ref, dt_ref, dtt_ref, gc_ref, gr_ref, beta_ref, *, c, tb):
    def decay(a_raw, a_log, dt):
        x = a_raw + dt
        softplus = jnp.maximum(x, 0.0) + jnp.log1p(jnp.exp(-jnp.abs(x)))
        return -jnp.exp(a_log) * softplus

    def split3(x):
        p0 = x.astype(BF16)
        r = x - p0.astype(F32)
        p1 = r.astype(BF16)
        p2 = (r - p1.astype(F32)).astype(BF16)
        return p0, p1, p2

    i = lax.broadcasted_iota(jnp.int32, (tb, tb), 0)
    j = lax.broadcasted_iota(jnp.int32, (tb, tb), 1)
    same = (i // c) == (j // c)
    lower = jnp.where(same & (j <= i), 1.0, 0.0).astype(BF16)
    upper = jnp.where(same & (i <= j), 1.0, 0.0).astype(BF16)
    g = decay(a_ref[...], alog_ref[...], dt_ref[...])
    gt = decay(at_ref[...], alogt_ref[...], dtt_ref[...])
    g0, g1, g2 = split3(g)
    gc_ref[...] = _dot(lower, g0) + (_dot(lower, g1) + _dot(lower, g2))
    t0, t1, t2 = split3(gt)
    gr_ref[...] = _dot(t0, upper) + (_dot(t1, upper) + _dot(t2, upper))
    beta_ref[...] = _sigmoid(b_ref[...])


def _gates(a_raw, b_raw, a_log, dt_bias, c, tb):
    n = a_raw.shape[0]
    tok = pl.BlockSpec((tb, HV_C), lambda i: (i, 0))
    hed = pl.BlockSpec((HV_C, tb), lambda i: (0, i))
    row = pl.BlockSpec((1, HV_C), lambda i: (0, 0))
    colv = pl.BlockSpec((HV_C, 1), lambda i: (0, 0))
    return pl.pallas_call(
        functools.partial(_gates_kernel, c=c, tb=tb),
        grid=(n // tb,),
        in_specs=[tok, hed, tok, row, colv, row, colv],
        out_specs=[tok, hed, tok],
        out_shape=[jax.ShapeDtypeStruct((n, HV_C), F32), jax.ShapeDtypeStruct((HV_C, n), F32),
                   jax.ShapeDtypeStruct((n, HV_C), F32)],
        compiler_params=_params(("parallel",), 16),
        name="gates",
    )(a_raw, a_raw.T, b_raw, a_log.reshape(1, HV_C), a_log.reshape(HV_C, 1),
      dt_bias.reshape(1, HV_C), dt_bias.reshape(HV_C, 1))


def _delta_kernel(q_ref, k_ref, v_ref, z_ref, gc_ref, gr_ref, beta_ref, s0_ref, nw_ref, *refs, c, hg):
    o_ref, s_ref = refs[-2:]

    @pl.when(pl.program_id(2) == 0)
    def _():
        s_ref[...] = s0_ref[...]

    i = lax.broadcasted_iota(jnp.int32, (c, c), 0)
    j = lax.broadcasted_iota(jnp.int32, (c, c), 1)
    eye = jnp.where(i == j, 1.0, 0.0)
    rep = HV_C // HK_C
    heads = range(hg)
    gc_all, gr_all, beta_all = gc_ref[0, 0], gr_ref[0, 0], beta_ref[0, 0]
    q = [q_ref[:, (n // rep) * DK_C:(n // rep + 1) * DK_C] for n in heads]
    k = [k_ref[:, (n // rep) * DK_C:(n // rep + 1) * DK_C] for n in heads]
    v = [v_ref[:, n * DV_C:(n + 1) * DV_C] for n in heads]
    z = [z_ref[:, n * DV_C:(n + 1) * DV_C] for n in heads]
    s_old = [s_ref[n] for n in heads]
    gcol = [gc_all[:, n:n + 1] for n in heads]
    beta = [beta_all[:, n:n + 1] for n in heads]
    d_incl = [jnp.exp(jnp.where(i >= j, gcol[n] - gr_all[n:n + 1, :], NEG_INF)) for n in heads]
    kb = [k[n] * beta[n] for n in heads]
    kh = [k[n].astype(BF16) for n in heads]
    m = [_dot_nt(kb[n].astype(BF16), kh[n]) * jnp.where(i > j, d_incl[n], 0.0) for n in heads]
    t = [eye - jnp.where(i // 2 == j // 2, m[n], 0.0) for n in heads]
    s = 2
    while s < c:
        join = (i // (2 * s) == j // (2 * s)) & (i // s != j // s)
        tb = [t[n].astype(BF16) for n in heads]
        x = [_dot(tb[n], jnp.where(join, m[n], 0.0).astype(BF16)) for n in heads]
        t = [t[n] - _dot(x[n].astype(BF16), tb[n]) for n in heads]
        s *= 2
    e_g = [jnp.exp(gcol[n]) for n in heads]
    sol = [_dot(t[n].astype(BF16), jnp.concatenate([v[n] * beta[n], kb[n] * e_g[n]], axis=-1).astype(BF16))
           for n in heads]
    s_bf = [s_old[n].astype(BF16) for n in heads]
    ub = [(sol[n][:, :DV_C] - _dot(sol[n][:, DV_C:].astype(BF16), s_bf[n])).astype(BF16) for n in heads]
    qk = [(_dot_nt(q[n].astype(BF16), kh[n]) * d_incl[n]).astype(BF16) for n in heads]
    o = [_dot((q[n] * e_g[n]).astype(BF16), s_bf[n]) + _dot(qk[n], ub[n]) for n in heads]
    g_last = [gcol[n][c - 1:c, :] for n in heads]
    s_new = [s_old[n] * jnp.exp(g_last[n]) + _dot_tn((k[n] * jnp.exp(g_last[n] - gcol[n])).astype(BF16), ub[n])
             for n in heads]
    o = [o[n] * lax.rsqrt(jnp.mean(o[n] * o[n], axis=-1, keepdims=True) + RMS_EPS) * nw_ref[...] for n in heads]
    o = [(o[n] * (z[n] * _sigmoid(z[n]))).astype(BF16) for n in heads]
    for n in heads:
        s_ref[n] = s_new[n]
        o_ref[:, n * DV_C:(n + 1) * DV_C] = o[n]


def _delta(qkv_act, z_all, gc, gr, beta, s0, norm_w, o_all, *, row0, z_col0, n_batch, t, c, hg):
    nc = t // c
    nhg = HV_C // hg
    rep = HV_C // HK_C
    qw = hg // rep * DK_C
    rb = row0 // c
    vmem = 24
    operands = [qkv_act, qkv_act, qkv_act, z_all, gc, gr, beta, s0, norm_w.reshape(1, DV_C)]
    in_specs = [pl.BlockSpec((c, qw), lambda b, g, ci: (b * nc + ci, g)),
                pl.BlockSpec((c, qw), lambda b, g, ci: (b * nc + ci, C_QK_DIM // qw + g)),
                pl.BlockSpec((c, hg * DV_C), lambda b, g, ci: (b * nc + ci, 2 * C_QK_DIM // (hg * DV_C) + g)),
                pl.BlockSpec((c, hg * DV_C), lambda b, g, ci: (rb + b * nc + ci, z_col0 // (hg * DV_C) + g)),
                pl.BlockSpec((1, 1, c, hg), lambda b, g, ci: (b * nc + ci, g, 0, 0)),
                pl.BlockSpec((1, 1, hg, c), lambda b, g, ci: (b * nc + ci, g, 0, 0)),
                pl.BlockSpec((1, 1, c, hg), lambda b, g, ci: (b * nc + ci, g, 0, 0)),
                pl.BlockSpec((hg, DK_C, DV_C), lambda b, g, ci: (b * nhg + g, 0, 0)),
                pl.BlockSpec((1, DV_C), lambda b, g, ci: (0, 0))]
    aliases = {}
    if o_all is not None:
        operands.append(o_all)
        in_specs.append(pl.BlockSpec(memory_space=pl.ANY))
        aliases = {len(operands) - 1: 0}
    return pl.pallas_call(
        functools.partial(_delta_kernel, c=c, hg=hg),
        grid=(n_batch, nhg, nc),
        in_specs=in_specs,
        out_specs=[pl.BlockSpec((c, hg * DV_C), lambda b, g, ci: (rb + b * nc + ci, g)),
                   pl.BlockSpec((hg, DK_C, DV_C), lambda b, g, ci: (b * nhg + g, 0, 0))],
        out_shape=[jax.ShapeDtypeStruct((z_all.shape[0], C_V_DIM), BF16),
                   jax.ShapeDtypeStruct((n_batch * HV_C, DK_C, DV_C), F32)],
        input_output_aliases=aliases,
        compiler_params=_params(("parallel", "parallel", "arbitrary"), vmem),
        name="delta",
    )(*operands)


def _chunk_layout(x, n_chunks, c, nhg, hg, head_major):
    if head_major:
        return x.reshape(nhg, hg, n_chunks, c).transpose(2, 0, 1, 3)
    return x.reshape(n_chunks, c, nhg, hg).transpose(0, 2, 1, 3)


TM = 512
TM_MLP = 1088
TF_MLP = 512
C_HG = 8
B_QBLK = 256
B_KBLK = 512
B_KBLK_S = 384


def _pad_cols(w, width):
    return jnp.pad(w, ((0, 0), (0, width - w.shape[1])))


def kernel(x_prompt, x_sample, cache_a_k, cache_a_v, cache_b_k, cache_b_v, cache_b_kidx, state_c_conv,
           state_c_ssm, a_w_in, a_rel_bias, a_w_out, b_w_in, b_w_out, c_w_in, c_conv_w, c_a_log, c_dt_bias,
           c_norm_w, c_w_out, ln1_g, ln1_b, mlp_w1, mlp_w2, ln2_g, ln2_b):
    nb_p, t_p, _ = x_prompt.shape
    nb_s, t_s, _ = x_sample.shape
    mp = nb_p * t_p
    ms = nb_s * t_s
    y = _join_rows(x_prompt.reshape(mp, D_MODEL), x_sample.reshape(ms, D_MODEL), TM)
    outs = {n: [] for n in ("akp", "avp", "aks", "avs", "bkp", "bvp", "bip", "bks", "bvs", "bis",
                            "ccp", "csp", "ccs", "css")}
    for i in range(DEPTH):
        kind, j = i % N_MIXERS, i // N_MIXERS
        if kind == 0:
            assert t_p % A_QBLK == 0 and t_p >= LEFT_CHUNKS * CHUNK == A_QBLK
            qkv = _mm(y, a_w_in, j, TM, 1536, 3 * D_MODEL // 1536)
            past = cache_a_k.shape[2]
            win = -(-(past + t_s) // LANES) * LANES
            mix_in, k_tail, v_tail = _attn_a_prompt(qkv, _a_bias_rows(a_rel_bias[j], LEFT_CHUNKS * CHUNK),
                                                    _a_band_prompt(), _zeros(mp + ms, D_MODEL, TM), nb_p, t_p)
            mix_in, k_new, v_new = _attn_a_sample(qkv, cache_a_k.reshape(-1, D_MODEL), cache_a_v.reshape(-1, D_MODEL),
                                                  _a_bias_rows(a_rel_bias[j], past), _a_band_sample(past, t_s, win),
                                                  mix_in, j, past, mp, nb_s, t_s)
            outs["akp"].append(k_tail.reshape(nb_p, A_QBLK, H_A, DH_A))
            outs["avp"].append(v_tail.reshape(nb_p, A_QBLK, H_A, DH_A))
            outs["aks"].append(k_new.reshape(nb_s, t_s, H_A, DH_A))
            outs["avs"].append(v_new.reshape(nb_s, t_s, H_A, DH_A))
            w_out = a_w_out
        elif kind == 1:
            nq_, nk_, ni_ = H_B * DH_B, KV_B * DH_B, H_IDX * D_IDX
            w = b_w_in[j]
            w_ki = w[:, nq_ + 2 * nk_ + ni_:nq_ + 2 * nk_ + ni_ + D_IDX]
            w_wi = w[:, nq_ + 2 * nk_ + ni_ + D_IDX:]
            zc = lambda n: jnp.zeros((D_MODEL, n), F32)
            w_x = jnp.concatenate([w_ki, zc(LANES - D_IDX), zc(LANES - D_IDX), w_ki,
                                   w_wi, zc(2 * LANES - H_IDX)], axis=1)
            assert nq_ == 2 * ni_ and 2 * nk_ == ni_
            qq = _mm(y, b_w_in, j, TM, ni_, 3, (BF16,), col_of=lambda t: jnp.where(t < 2, t, t + 1),
                     n_scaled=2, scale=DH_B ** -0.5 * LOG2E)
            k32, k16 = _mm(y, b_w_in, j, TM, nk_, 1, (F32, BF16), col_of=lambda t: nq_ // nk_)
            v32, v16 = _mm(y, b_w_in, j, TM, nk_, 1, (F32, BF16), col_of=lambda t: nq_ // nk_ + 1)
            x32, x16 = _mm(y, w_x[None], 0, TM, B_XW, 1, (F32, BF16))
            k_p, k_s = _split_rows(k32, mp, TM)
            v_p, v_s = _split_rows(v32, mp, TM)
            outs["bkp"].append(k_p.reshape(nb_p, t_p, KV_B, DH_B))
            outs["bvp"].append(v_p.reshape(nb_p, t_p, KV_B, DH_B))
            outs["bip"].append(x32[:mp, :D_IDX].reshape(nb_p, t_p, D_IDX))
            outs["bks"].append(k_s.reshape(nb_s, t_s, KV_B, DH_B))
            outs["bvs"].append(v_s.reshape(nb_s, t_s, KV_B, DH_B))
            outs["bis"].append(x32[mp:, :D_IDX].reshape(nb_s, t_s, D_IDX))
            mix_in = _dsa(qq, x32, k16, v16, x16, x16, _zeros(mp + ms, D_MODEL, TM), ka_col=0, kz_col=1, row0=0,
                          n_batch=nb_p, lp=t_p, nq=B_QBLK, n_qblk=t_p // B_QBLK, kblk=B_KBLK, n_keys=t_p,
                          q_pos0=0, q_step=B_QBLK)
            past = cache_b_k.shape[2]
            n_keys = past + t_s
            lp = -(-n_keys // B_KBLK_S) * B_KBLK_S
            k_s, v_s, ka_s, kz_s = _sample_keys(
                cache_b_k[j].reshape(nb_s * past, nk_), cache_b_v[j].reshape(nb_s * past, nk_),
                cache_b_kidx[j].reshape(nb_s * past, D_IDX), k16, v16, x16,
                row0=mp, n_batch=nb_s, past=past, s_len=t_s, lp=lp)
            mix_in = _dsa(qq, x32, k_s, v_s, ka_s, kz_s, mix_in, ka_col=0, kz_col=0, row0=mp, n_batch=nb_s, lp=lp,
                          nq=t_s, n_qblk=1, kblk=B_KBLK_S, n_keys=n_keys, q_pos0=PAST_LEN, q_step=0)
            w_out = b_w_out
        else:
            n_main = C_CONV_DIM + C_V_DIM
            proj = _mm(y, c_w_in, j, TM, 1536, n_main // 1536)
            gate_raw = _mm_x3(y, _pad_cols(c_w_in[j][:, n_main:], LANES), TM)
            b_raw, a_raw = gate_raw[:, :HV_C], gate_raw[:, HV_C:2 * HV_C]
            nhg = HV_C // C_HG
            mix_in = _zeros(mp + ms, C_V_DIM, TM)
            for (row0, nbt, tt, state, s0, names) in (
                    (0, nb_p, t_p, jnp.zeros((nb_p, CONV_W - 1, C_CONV_DIM), F32),
                     jnp.zeros((nb_p * HV_C, DK_C, DV_C), F32), ("ccp", "csp")),
                    (mp, nb_s, t_s, state_c_conv[j], state_c_ssm[j].reshape(nb_s * HV_C, DK_C, DV_C), ("ccs", "css"))):
                c = min(CHUNK, tt)
                n_rows = nbt * tt
                n_new = min(tt, CONV_W - 1)
                u_tail = jnp.stack([proj[row0 + (b + 1) * tt - n_new:row0 + (b + 1) * tt, :C_CONV_DIM]
                                    for b in range(nbt)])
                outs[names[0]].append(jnp.concatenate([state[:, n_new:], u_tail], axis=1))
                state8 = jnp.pad(state, ((0, 0), (SUBLANES - (CONV_W - 1), 0), (0, 0)))
                act = _conv(proj, state8, c_conv_w[j], row0=row0, n_batch=nbt, t=tt, cw=256, rows=min(tt, 256))
                gc, gr, beta = _gates(a_raw[row0:row0 + n_rows], b_raw[row0:row0 + n_rows], c_a_log[j], c_dt_bias[j],
                                      c, 512)
                n_chunks = n_rows // c
                mix_in, s_new = _delta(act, proj, _chunk_layout(gc, n_chunks, c, nhg, C_HG, False),
                                       _chunk_layout(gr, n_chunks, c, nhg, C_HG, True),
                                       _chunk_layout(beta, n_chunks, c, nhg, C_HG, False), s0, c_norm_w[j], mix_in,
                                       row0=row0, z_col0=C_CONV_DIM, n_batch=nbt, t=tt, c=c, hg=C_HG)
                outs[names[1]].append(s_new.reshape(nbt, HV_C, DK_C, DV_C))
            w_out = c_w_out
        y = _mm_ln(mix_in, w_out.astype(BF16), j, y, ln1_g[i], ln1_b[i], TM, 1024)
        y = _mlp_ln(y, mlp_w1, mlp_w2, i, ln2_g[i], ln2_b[i], TM_MLP, TF_MLP)
    st = lambda name: jnp.stack(outs[name])
    y_p, y_s = _split_rows(y, mp, TM)
    return (y_p.reshape(nb_p, t_p, D_MODEL), y_s.reshape(nb_s, t_s, D_MODEL),
            st("akp"), st("avp"), st("aks"), st("avs"),
            st("bkp"), st("bvp"), st("bip"), st("bks"), st("bvs"), st("bis"),
            st("ccp"), st("csp"), st("ccs"), st("css"))
```

```python
import functools

import numpy as np
import jax
import jax.numpy as jnp
from jax import lax
from jax.experimental import pallas as pl
from jax.experimental.pallas import tpu as pltpu

D_MODEL = 2048
DEPTH = 4
PAST_LEN = 1024
CHUNK = 64
N_MIXERS = 3
D_FF = 4 * D_MODEL
H_A = 16
DH_A = D_MODEL // H_A
LEFT_CHUNKS = 8
REL_CLIP = 128
H_B = 16
KV_B = 4
DH_B = D_MODEL // H_B
H_IDX = 16
D_IDX = 64
TOPK_MAX = 256
HK_C = 16
HV_C = 32
DK_C = 128
DV_C = 128
CONV_W = 4
C_QK_DIM = HK_C * DK_C
C_V_DIM = HV_C * DV_C
C_CONV_DIM = 2 * C_QK_DIM + C_V_DIM
ALPHA = (2.0 * DEPTH) ** 0.25
LN_EPS = 1e-5
RMS_EPS = 1e-6

LANES = 128
SUBLANES = 8
VMEM_CAP_MB = 56

F32 = jnp.float32
BF16 = jnp.bfloat16
NEG_INF = float("-inf")
LOG2E = 1.4426950408889634
INT_MIN = -(2 ** 31)


def _params(sem, vmem_mb):
    return pltpu.CompilerParams(dimension_semantics=sem,
                                vmem_limit_bytes=min(vmem_mb, VMEM_CAP_MB) * 1024 * 1024)


def _dot(a, b):
    return jnp.dot(a, b, preferred_element_type=F32)


def _dot_nt(a, b):
    return lax.dot_general(a, b, (((1,), (1,)), ((), ())), preferred_element_type=F32)


def _dot_tn(a, b):
    return lax.dot_general(a, b, (((0,), (0,)), ((), ())), preferred_element_type=F32)


def _split2(a):
    hi = a.astype(BF16)
    lo = (a - hi.astype(F32)).astype(BF16)
    return hi, lo


def _dot_x3(a, b):
    ah, al = _split2(a)
    bh, bl = _split2(b)
    return _dot(ah, bh) + (_dot(ah, bl) + _dot(al, bh))


def _layer_norm(z, g, b):
    mu = jnp.mean(z, axis=-1, keepdims=True)
    zc = z - mu
    var = jnp.mean(zc * zc, axis=-1, keepdims=True)
    return zc * lax.rsqrt(var + LN_EPS) * g + b


def _sigmoid(x):
    return 0.5 * jnp.tanh(0.5 * x) + 0.5


def _mm_kernel(x_ref, w_ref, *refs, n_scaled, scale):
    *o_refs, wb_ref = refs

    @pl.when(pl.program_id(1) == 0)
    def _():
        wb_ref[...] = w_ref[0].astype(BF16)

    res = _dot(x_ref[...].astype(BF16), wb_ref[...])
    if n_scaled:
        res = res * jnp.where(pl.program_id(0) < n_scaled, scale, 1.0)
    for o_ref in o_refs:
        o_ref[...] = res.astype(o_ref.dtype)


def _mm(x, w, layer, tm, tn, n_tiles, dtypes=(F32,), col_of=lambda j: j, n_scaled=0, scale=1.0):
    m, k = x.shape
    out_b = sum(jnp.dtype(d).itemsize for d in dtypes)
    vmem = (2 * tm * k * 4 + tm * k * 2 + 2 * k * tn * 4 + k * tn * 2 + 2 * tm * tn * out_b
            + tm * tn * 4) // 2 ** 20 + 4
    outs = pl.pallas_call(
        functools.partial(_mm_kernel, n_scaled=n_scaled, scale=scale),
        grid=(n_tiles, m // tm),
        in_specs=[pl.BlockSpec((tm, k), lambda j, i: (i, 0)),
                  pl.BlockSpec((1, k, tn), lambda j, i: (layer, 0, col_of(j)))],
        out_specs=[pl.BlockSpec((tm, tn), lambda j, i: (i, j)) for _ in dtypes],
        out_shape=[jax.ShapeDtypeStruct((m, n_tiles * tn), d) for d in dtypes],
        scratch_shapes=[pltpu.VMEM((k, tn), BF16)],
        compiler_params=_params(("parallel", "arbitrary"), vmem),
        name="mm",
    )(x, w)
    return outs[0] if len(dtypes) == 1 else outs


def _mm_x3_kernel(x_ref, w_ref, o_ref):
    o_ref[...] = _dot_x3(x_ref[...], w_ref[...])


def _mm_x3(x, w, tm):
    m, k = x.shape
    n = w.shape[1]
    return pl.pallas_call(
        _mm_x3_kernel,
        grid=(m // tm,),
        in_specs=[pl.BlockSpec((tm, k), lambda i: (i, 0)),
                  pl.BlockSpec((k, n), lambda i: (0, 0))],
        out_specs=pl.BlockSpec((tm, n), lambda i: (i, 0)),
        out_shape=jax.ShapeDtypeStruct((m, n), F32),
        compiler_params=_params(("parallel",), 24),
        name="mm_x3",
    )(x, w)


def _mm_ln_kernel(x_ref, w_ref, r_ref, g_ref, b_ref, o_ref, acc_ref, *, rows):
    k = pl.program_id(1)

    @pl.when(k == 0)
    def _():
        acc_ref[...] = jnp.zeros_like(acc_ref)

    acc_ref[...] += _dot(x_ref[...], w_ref[0])

    @pl.when(k == pl.num_programs(1) - 1)
    def _():
        def norm_rows(r, carry):
            sl = pl.ds(pl.multiple_of(r * rows, rows), rows)
            o_ref[sl, :] = _layer_norm(ALPHA * r_ref[sl, :] + acc_ref[sl, :], g_ref[...], b_ref[...])
            return carry

        lax.fori_loop(0, o_ref.shape[0] // rows, norm_rows, 0)


def _mm_ln(x, w, layer, resid, g, b, tm, tk):
    m, k = x.shape
    d = w.shape[2]
    vmem = (2 * tm * tk * 2 + 2 * tk * d * 2 + 6 * tm * d * 4) // 2 ** 20 + 4
    return pl.pallas_call(
        functools.partial(_mm_ln_kernel, rows=tm // 4),
        grid=(m // tm, k // tk),
        in_specs=[pl.BlockSpec((tm, tk), lambda i, kk: (i, kk)),
                  pl.BlockSpec((1, tk, d), lambda i, kk: (layer, kk, 0)),
                  pl.BlockSpec((tm, d), lambda i, kk: (i, 0)),
                  pl.BlockSpec((1, d), lambda i, kk: (0, 0)),
                  pl.BlockSpec((1, d), lambda i, kk: (0, 0))],
        out_specs=pl.BlockSpec((tm, d), lambda i, kk: (i, 0)),
        out_shape=jax.ShapeDtypeStruct((m, d), F32),
        scratch_shapes=[pltpu.VMEM((tm, d), F32)],
        compiler_params=_params(("parallel", "arbitrary"), vmem),
        name="mm_ln",
    )(x, w, resid, g.reshape(1, d), b.reshape(1, d))


def _join_kernel(a_ref, b_ref, o_ref, *, n_a):
    i = pl.program_id(0)

    @pl.when(i < n_a)
    def _():
        o_ref[...] = a_ref[...]

    @pl.when(i >= n_a)
    def _():
        o_ref[...] = b_ref[...]


def _join_rows(a, b, tm):
    (ma, d), mb = a.shape, b.shape[0]
    n_a = ma // tm
    return pl.pallas_call(
        functools.partial(_join_kernel, n_a=n_a),
        grid=(n_a + mb // tm,),
        in_specs=[pl.BlockSpec((tm, d), lambda i: (jnp.minimum(i, n_a - 1), 0)),
                  pl.BlockSpec((tm, d), lambda i: (jnp.maximum(i - n_a, 0), 0))],
        out_specs=pl.BlockSpec((tm, d), lambda i: (i, 0)),
        out_shape=jax.ShapeDtypeStruct((ma + mb, d), a.dtype),
        compiler_params=_params(("arbitrary",), 6 * tm * d * a.dtype.itemsize // 2 ** 20 + 4),
        name="join_rows",
    )(a, b)


def _split_kernel(x_ref, a_ref, b_ref, *, n_a):
    i = pl.program_id(0)

    @pl.when(i < n_a)
    def _():
        a_ref[...] = x_ref[...]

    @pl.when(i >= n_a)
    def _():
        b_ref[...] = x_ref[...]


def _split_rows(x, ma, tm):
    m, d = x.shape
    n_a = ma // tm
    return pl.pallas_call(
        functools.partial(_split_kernel, n_a=n_a),
        grid=(m // tm,),
        in_specs=[pl.BlockSpec((tm, d), lambda i: (i, 0))],
        out_specs=[pl.BlockSpec((tm, d), lambda i: (jnp.minimum(i, n_a - 1), 0)),
                   pl.BlockSpec((tm, d), lambda i: (jnp.maximum(i - n_a, 0), 0))],
        out_shape=[jax.ShapeDtypeStruct((ma, d), x.dtype), jax.ShapeDtypeStruct((m - ma, d), x.dtype)],
        compiler_params=_params(("arbitrary",), 6 * tm * d * x.dtype.itemsize // 2 ** 20 + 4),
        name="split_rows",
    )(x)


def _fill_kernel(o_ref):
    o_ref[...] = jnp.zeros(o_ref.shape, o_ref.dtype)


def _zeros(m, d, tm):
    return pl.pallas_call(
        _fill_kernel,
        grid=(m // tm,),
        out_specs=pl.BlockSpec((tm, d), lambda i: (i, 0)),
        out_shape=jax.ShapeDtypeStruct((m, d), BF16),
        compiler_params=_params(("parallel",), 2 * tm * d * 2 // 2 ** 20 + 4),
        name="fill",
    )()


def _mlp_ln_kernel(y_ref, w1_ref, w2_ref, g_ref, b_ref, o_ref, xb_ref, *, rows):
    f = pl.program_id(1)

    @pl.when(f == 0)
    def _():
        xb_ref[...] = y_ref[...].astype(BF16)
        o_ref[...] = jnp.zeros_like(o_ref)

    h = jnp.maximum(_dot(xb_ref[...], w1_ref[0].astype(BF16)), 0.0)
    o_ref[...] += _dot((h * h).astype(BF16), w2_ref[0].astype(BF16))

    @pl.when(f == pl.num_programs(1) - 1)
    def _():
        def norm_rows(r, carry):
            sl = pl.ds(pl.multiple_of(r * rows, rows), rows)
            o_ref[sl, :] = _layer_norm(ALPHA * y_ref[sl, :] + o_ref[sl, :], g_ref[...], b_ref[...])
            return carry

        lax.fori_loop(0, o_ref.shape[0] // rows, norm_rows, 0)


def _mlp_ln(y, w1, w2, layer, g, b, tm, tf):
    m, d = y.shape
    f = w1.shape[2]
    vmem = (2 * tm * d * 4 + tm * d * 2 + 4 * d * tf * 4 + 2 * d * tf * 2 + 3 * tm * tf * 4) // 2 ** 20 + 12
    once = pl.Buffered(1)
    return pl.pallas_call(
        functools.partial(_mlp_ln_kernel, rows=tm // 8),
        grid=(m // tm, f // tf),
        in_specs=[pl.BlockSpec((tm, d), lambda i, ff: (i, 0), pipeline_mode=once),
                  pl.BlockSpec((1, d, tf), lambda i, ff: (layer, 0, ff)),
                  pl.BlockSpec((1, tf, d), lambda i, ff: (layer, ff, 0)),
                  pl.BlockSpec((1, d), lambda i, ff: (0, 0)),
                  pl.BlockSpec((1, d), lambda i, ff: (0, 0))],
        out_specs=pl.BlockSpec((tm, d), lambda i, ff: (i, 0), pipeline_mode=once),
        out_shape=jax.ShapeDtypeStruct((m, d), F32),
        scratch_shapes=[pltpu.VMEM((tm, d), BF16)],
        compiler_params=_params(("parallel", "arbitrary"), vmem),
        name="mlp_ln",
    )(y, w1, w2, g.reshape(1, d), b.reshape(1, d))


A_QBLK = 512
A_PAIR = 2 * CHUNK
A_WIN = (LEFT_CHUNKS + 2) * CHUNK
A_UW = A_WIN + A_PAIR


def _softmax_pv(s2, v):
    m = jnp.max(s2, axis=-1, keepdims=True)
    p = jnp.exp2(s2 - m)
    l = jnp.sum(p, axis=-1, keepdims=True)
    return _dot(p.astype(BF16), v) / l


def _bias_table(u_row, band, rows):
    u = jnp.broadcast_to(u_row * LOG2E, (rows, A_UW))
    return pltpu.roll(u, A_UW - A_PAIR + 1, 1, stride=1, stride_axis=0)[:, :band.shape[1]] + band


def _attn_a_prompt_kernel(q_ref, kp_ref, kc_ref, vp_ref, vc_ref, u_ref, band_ref, o_hbm_ref, o_ref, ko_ref, vo_ref,
                          kw_ref, vw_ref):
    del o_hbm_ref
    i = pl.program_id(2)

    @pl.when(i == pl.num_programs(2) - 1)
    def _():
        ko_ref[...] = kc_ref[...]
        vo_ref[...] = vc_ref[...]

    kw_ref[0:A_QBLK, :] = kp_ref[...].astype(BF16)
    kw_ref[A_QBLK:2 * A_QBLK, :] = kc_ref[...].astype(BF16)
    vw_ref[0:A_QBLK, :] = vp_ref[...].astype(BF16)
    vw_ref[A_QBLK:2 * A_QBLK, :] = vc_ref[...].astype(BF16)
    bias = _bias_table(u_ref[0], band_ref[...], A_PAIR)
    col = lax.broadcasted_iota(jnp.int32, (A_PAIR, A_WIN), 1)
    pairs = range(A_QBLK // A_PAIR)
    q = [(q_ref[c * A_PAIR:(c + 1) * A_PAIR, :] * (DH_A ** -0.5 * LOG2E)).astype(BF16) for c in pairs]
    s = [_dot_nt(q[c], kw_ref[c * A_PAIR:c * A_PAIR + A_WIN, :]) + bias for c in pairs]
    s = [jnp.where(col + (i * A_QBLK - A_QBLK + c * A_PAIR) >= 0, s[c], NEG_INF) for c in pairs]
    o = [_softmax_pv(s[c], vw_ref[c * A_PAIR:c * A_PAIR + A_WIN, :]).astype(BF16) for c in pairs]
    for c in pairs:
        o_ref[c * A_PAIR:(c + 1) * A_PAIR, :] = o[c]


def _attn_a_prompt(qkv, u_rows, band, o_all, n_batch, t):
    nblk = t // A_QBLK

    def spec(col0, prev):
        if prev:
            return pl.BlockSpec((A_QBLK, DH_A), lambda b, h, i: (b * nblk + jnp.maximum(i - 1, 0), col0 + h))
        return pl.BlockSpec((A_QBLK, DH_A), lambda b, h, i: (b * nblk + i, col0 + h))

    tail = pl.BlockSpec((A_QBLK, DH_A), lambda b, h, i: (b, h))
    return pl.pallas_call(
        _attn_a_prompt_kernel,
        grid=(n_batch, H_A, nblk),
        in_specs=[spec(0, False), spec(H_A, True), spec(H_A, False), spec(2 * H_A, True), spec(2 * H_A, False),
                  pl.BlockSpec((1, 1, A_UW), lambda b, h, i: (h, 0, 0)),
                  pl.BlockSpec((A_PAIR, A_WIN), lambda b, h, i: (0, 0)),
                  pl.BlockSpec(memory_space=pl.ANY)],
        out_specs=[pl.BlockSpec((A_QBLK, DH_A), lambda b, h, i: (b * nblk + i, h)), tail, tail],
        out_shape=[jax.ShapeDtypeStruct(o_all.shape, BF16),
                   jax.ShapeDtypeStruct((n_batch * A_QBLK, D_MODEL), F32),
                   jax.ShapeDtypeStruct((n_batch * A_QBLK, D_MODEL), F32)],
        input_output_aliases={7: 0},
        scratch_shapes=[pltpu.VMEM((2 * A_QBLK, DH_A), BF16), pltpu.VMEM((2 * A_QBLK, DH_A), BF16)],
        compiler_params=_params(("parallel", "parallel", "arbitrary"), 16),
        name="attn_a_prompt",
    )(qkv, qkv, qkv, qkv, qkv, u_rows, band, o_all)


def _attn_a_sample_kernel(q_ref, kc_ref, kn_ref, vc_ref, vn_ref, u_ref, band_ref, o_hbm_ref, o_ref, ko_ref, vo_ref,
                          kw_ref, vw_ref, *, past, s_len):
    del o_hbm_ref
    ko_ref[...] = kn_ref[...]
    vo_ref[...] = vn_ref[...]
    win = kw_ref.shape[0]
    pad = win - past - s_len
    for h in range(H_A):
        cols = slice(h * DH_A, (h + 1) * DH_A)
        kw_ref[0:past, :] = kc_ref[:, cols].astype(BF16)
        kw_ref[past:past + s_len, :] = kn_ref[:, cols].astype(BF16)
        kw_ref[past + s_len:, :] = jnp.zeros((pad, DH_A), BF16)
        vw_ref[0:past, :] = vc_ref[:, cols].astype(BF16)
        vw_ref[past:past + s_len, :] = vn_ref[:, cols].astype(BF16)
        vw_ref[past + s_len:, :] = jnp.zeros((pad, DH_A), BF16)
        q = (q_ref[:, cols] * (DH_A ** -0.5 * LOG2E)).astype(BF16)
        s = _dot_nt(q, kw_ref[...]) + _bias_table(u_ref[h], band_ref[...], s_len)
        o_ref[:, cols] = _softmax_pv(s, vw_ref[...]).astype(BF16)


def _attn_a_sample(qkv, k_cache, v_cache, u_rows, band, o_all, layer, past, row0, n_batch, s_len):
    win = band.shape[1]
    rb = row0 // s_len
    new = lambda colblk: pl.BlockSpec((s_len, D_MODEL), lambda b: (rb + b, colblk))
    old = pl.BlockSpec((past, D_MODEL), lambda b: (layer * n_batch + b, 0))
    rows = pl.BlockSpec((s_len, D_MODEL), lambda b: (b, 0))
    return pl.pallas_call(
        functools.partial(_attn_a_sample_kernel, past=past, s_len=s_len),
        grid=(n_batch,),
        in_specs=[new(0), old, new(1), old, new(2),
                  pl.BlockSpec((H_A, 1, A_UW), lambda b: (0, 0, 0)),
                  pl.BlockSpec((s_len, win), lambda b: (0, 0)),
                  pl.BlockSpec(memory_space=pl.ANY)],
        out_specs=[pl.BlockSpec((s_len, D_MODEL), lambda b: (rb + b, 0)), rows, rows],
        out_shape=[jax.ShapeDtypeStruct(o_all.shape, BF16),
                   jax.ShapeDtypeStruct((n_batch * s_len, D_MODEL), F32),
                   jax.ShapeDtypeStruct((n_batch * s_len, D_MODEL), F32)],
        input_output_aliases={7: 0},
        scratch_shapes=[pltpu.VMEM((win, DH_A), BF16), pltpu.VMEM((win, DH_A), BF16)],
        compiler_params=_params(("parallel",), 32),
        name="attn_a_sample",
    )(qkv, k_cache, qkv, v_cache, qkv, u_rows, band, o_all)


def _a_bias_rows(rel_bias, q0):
    d = q0 + A_PAIR - 1 - np.arange(A_UW)
    return rel_bias[:, np.clip(d, -REL_CLIP, REL_CLIP) + REL_CLIP][:, None, :]


def _a_band_prompt():
    qc = LEFT_CHUNKS * CHUNK + np.arange(A_PAIR)[:, None]
    kc = np.arange(A_WIN)[None, :]
    vis = (kc // CHUNK <= qc // CHUNK) & (kc // CHUNK >= qc // CHUNK - LEFT_CHUNKS)
    return jnp.asarray(np.where(vis, 0.0, -np.inf).astype(np.float32))


def _a_band_sample(past, s_len, win):
    q_pos = PAST_LEN + np.arange(s_len)[:, None]
    k_pos = PAST_LEN - past + np.arange(win)[None, :]
    vis = ((k_pos >= 0) & (k_pos // CHUNK <= q_pos // CHUNK) & (k_pos // CHUNK >= q_pos // CHUNK - LEFT_CHUNKS)
           & (np.arange(win)[None, :] < past + s_len))
    return jnp.asarray(np.where(vis, 0.0, -np.inf).astype(np.float32))


B_XW = 4 * LANES


def _key_to_float(key):
    return lax.bitcast_convert_type(jnp.where(key >= 0, key, key ^ jnp.int32(0x7FFFFFFF)), F32)


KEY_NEG_INF = int(np.array(-np.inf, np.float32).view(np.int32)) ^ 0x7FFFFFFF


def _lane_fold(x):
    acc = x[:, 0:LANES]
    for t in range(1, x.shape[1] // LANES):
        acc = acc + x[:, t * LANES:(t + 1) * LANES]
    return acc


def _dsa_kernel(q_ref, qi_ref, wi_ref, k_ref, v_ref, ka_ref, kb_ref, *refs,
                nq, kblk, n_keys, q_pos0, q_step, n_sel, idx_bits):
    o_ref, sc_ref, msk_ref, qs_ref, m_ref, l_ref, acc_ref = refs[-7:]
    qbase = q_pos0 + pl.program_id(1) * q_step
    n_adm = jnp.minimum(((qbase + nq - 1) // CHUNK + 1) * CHUNK, n_keys)
    nb = (n_adm + kblk - 1) // kblk
    qpos = qbase + lax.broadcasted_iota(jnp.int32, (nq, kblk), 0)
    col0 = lax.broadcasted_iota(jnp.int32, (nq, kblk), 1)

    wsc = wi_ref[...] * (H_IDX ** -0.5 * D_IDX ** -0.5)

    def score_block(kb, carry):
        r = pl.multiple_of(kb * kblk, kblk)
        ka = ka_ref[pl.ds(r, kblk), :]
        kz = kb_ref[pl.ds(r, kblk), :]
        acc = jnp.zeros((nq, kblk), F32)
        for p in range(H_IDX // 2):
            q2 = qi_ref[:, p * LANES:(p + 1) * LANES]
            acc = acc + wsc[:, 2 * p:2 * p + 1] * jnp.maximum(_dot_nt(q2, ka), 0.0)
            acc = acc + wsc[:, 2 * p + 1:2 * p + 2] * jnp.maximum(_dot_nt(q2, kz), 0.0)
        col = col0 + kb * kblk
        adm = (col // CHUNK <= qpos // CHUNK) & (col < n_keys)
        sc_ref[kb] = jnp.where(adm, acc, NEG_INF)
        return carry

    lax.fori_loop(0, nb, score_block, 0)

    def count(pred_fn):
        def body(kb, c):
            return c + _lane_fold(jnp.where(pred_fn(kb, sc_ref[kb]), 1.0, 0.0))
        c = lax.fori_loop(0, nb, body, jnp.zeros((nq, LANES), F32))
        return jnp.sum(c, axis=-1, keepdims=True)

    def bit_step(bi, pre):
        cand = pre | jnp.left_shift(jnp.int32(1), 31 - bi)
        cand_s = cand ^ jnp.int32(INT_MIN)
        thr_c = _key_to_float(cand_s)
        cnt = count(lambda kb, sc: sc >= thr_c)
        return jnp.where((cnt >= n_sel) | (cand_s <= KEY_NEG_INF), cand, pre)

    pre = lax.fori_loop(0, 32, bit_step, jnp.zeros((nq, 1), jnp.int32))
    thr = _key_to_float(pre ^ jnp.int32(INT_MIN))
    need = n_sel - count(lambda kb, sc: sc > thr)
    n_ge = count(lambda kb, sc: sc >= thr)

    def idx_step(bi, lim):
        cand = lim | jnp.left_shift(jnp.int32(1), idx_bits - 1 - bi)
        cnt = count(lambda kb, sc: (sc == thr) & (col0 + kb * kblk < cand))
        return jnp.where(cnt < need, cand, lim)

    tied = jnp.max(jnp.where(thr > NEG_INF, n_ge, 0.0)) > n_sel
    lim = lax.cond(tied,
                   lambda: lax.fori_loop(0, idx_bits, idx_step, jnp.zeros((nq, 1), jnp.int32)),
                   lambda: jnp.full((nq, 1), 2 ** idx_bits, jnp.int32))

    def mask_block(kb, carry):
        sc = sc_ref[kb]
        sel = (sc > thr) | ((sc == thr) & (col0 + kb * kblk <= lim))
        sel = sel & (sc > NEG_INF) & (sc < float("inf"))
        msk_ref[kb] = jnp.where(sel, 0.0, NEG_INF)
        return carry

    lax.fori_loop(0, nb, mask_block, 0)

    group = H_B // KV_B
    gq = group * nq
    for h in range(H_B):
        qs_ref[h * nq:(h + 1) * nq, :] = q_ref[:, h * DH_B:(h + 1) * DH_B]
    m_ref[...] = jnp.full(m_ref.shape, NEG_INF, F32)
    l_ref[...] = jnp.zeros(l_ref.shape, F32)
    acc_ref[...] = jnp.zeros(acc_ref.shape, F32)

    def attend(kb, carry):
        r = pl.multiple_of(kb * kblk, kblk)
        msk = msk_ref[kb]
        s = [_dot_nt(qs_ref[g * gq:(g + 1) * gq, :], k_ref[pl.ds(r, kblk), g * DH_B:(g + 1) * DH_B])
             for g in range(KV_B)]
        for g in range(KV_B):
            ps = []
            for hq in range(group):
                rows = slice((g * group + hq) * nq, (g * group + hq + 1) * nq)
                sh = s[g][hq * nq:(hq + 1) * nq] + msk
                m_old = m_ref[rows]
                m_new = jnp.maximum(m_old, jnp.max(sh, axis=-1, keepdims=True))
                m_use = jnp.where(m_new == NEG_INF, 0.0, m_new)
                p = jnp.exp2(sh - m_use)
                a = jnp.exp2(m_old - m_use)
                m_ref[rows] = m_new
                l_ref[rows] = a * l_ref[rows] + jnp.sum(p, axis=-1, keepdims=True)
                acc_ref[rows] = a * acc_ref[rows]
                ps.append(p.astype(BF16))
            acc_ref[g * gq:(g + 1) * gq, :] += _dot(jnp.concatenate(ps, axis=0),
                                                    v_ref[pl.ds(r, kblk), g * DH_B:(g + 1) * DH_B])
        return carry

    lax.fori_loop(0, nb, attend, 0)
    for h in range(H_B):
        rows = slice(h * nq, (h + 1) * nq)
        o_ref[:, h * DH_B:(h + 1) * DH_B] = (acc_ref[rows] / l_ref[rows]).astype(BF16)


def _dsa(qq, x32, k, v, ka, kz, o_all, *, ka_col, kz_col, row0, n_batch, lp, nq, n_qblk, kblk, n_keys,
         q_pos0, q_step):
    nkb = lp // kblk
    n_sel = min(TOPK_MAX, n_keys // 4)
    rb = row0 // nq
    qspec = lambda width, colblk: pl.BlockSpec((nq, width), lambda b, i: (rb + b * n_qblk + i, colblk))
    kspec = lambda width, colblk: pl.BlockSpec((lp, width), lambda b, i: (b, colblk))
    kern = functools.partial(_dsa_kernel, nq=nq, kblk=kblk, n_keys=n_keys, q_pos0=q_pos0, q_step=q_step,
                             n_sel=n_sel, idx_bits=max(1, int(np.ceil(np.log2(lp)))))
    vmem = (2 * 2 * lp * (2 * KV_B * DH_B + 2 * LANES) + 2 * nkb * nq * kblk * 4
            + H_B * nq * (DH_B * 6 + 2 * LANES * 4) + 4 * nq * (qq.shape[1] + D_MODEL) * 2) // 2 ** 20 + 8
    operands = [qq, qq, x32, k, v, ka, kz]
    in_specs = [qspec(H_B * DH_B, 0), qspec(H_IDX * D_IDX, H_B * DH_B // (H_IDX * D_IDX)), qspec(LANES, 2),
                kspec(KV_B * DH_B, 0), kspec(KV_B * DH_B, 0), kspec(LANES, ka_col), kspec(LANES, kz_col)]
    aliases = {}
    if o_all is not None:
        operands.append(o_all)
        in_specs.append(pl.BlockSpec(memory_space=pl.ANY))
        aliases = {len(operands) - 1: 0}
    return pl.pallas_call(
        kern,
        grid=(n_batch, n_qblk),
        in_specs=in_specs,
        out_specs=pl.BlockSpec((nq, D_MODEL), lambda b, i: (rb + b * n_qblk + i, 0)),
        out_shape=jax.ShapeDtypeStruct((qq.shape[0], D_MODEL), BF16),
        input_output_aliases=aliases,
        scratch_shapes=[pltpu.VMEM((nkb, nq, kblk), F32),
                        pltpu.VMEM((nkb, nq, kblk), F32), pltpu.VMEM((H_B * nq, DH_B), BF16),
                        pltpu.VMEM((H_B * nq, 1), F32), pltpu.VMEM((H_B * nq, 1), F32),
                        pltpu.VMEM((H_B * nq, DH_B), F32)],
        compiler_params=_params(("parallel", "arbitrary"), vmem),
        name="dsa",
    )(*operands)


def _sample_keys_kernel(ck_ref, cv_ref, ci_ref, kn_ref, vn_ref, xn_ref, k_ref, v_ref, ka_ref, kz_ref,
                        *, past, s_len):
    end = past + s_len
    pad = k_ref.shape[0] - end
    k_ref[0:past, :] = ck_ref[...].astype(BF16)
    k_ref[past:end, :] = kn_ref[...]
    k_ref[end:, :] = jnp.zeros((pad, k_ref.shape[1]), BF16)
    v_ref[0:past, :] = cv_ref[...].astype(BF16)
    v_ref[past:end, :] = vn_ref[...]
    v_ref[end:, :] = jnp.zeros((pad, v_ref.shape[1]), BF16)
    ci = ci_ref[...]
    zero = jnp.zeros((past, LANES - D_IDX), F32)
    ka_ref[0:past, :] = jnp.concatenate([ci, zero], axis=1).astype(BF16)
    kz_ref[0:past, :] = jnp.concatenate([zero, ci], axis=1).astype(BF16)
    ka_ref[past:end, :] = xn_ref[:, 0:LANES]
    kz_ref[past:end, :] = xn_ref[:, LANES:2 * LANES]
    ka_ref[end:, :] = jnp.zeros((pad, LANES), BF16)
    kz_ref[end:, :] = jnp.zeros((pad, LANES), BF16)


def _sample_keys(cache_k, cache_v, cache_ki, k16, v16, x16, *, row0, n_batch, past, s_len, lp):
    rb = row0 // s_len
    nk = KV_B * DH_B
    old = lambda width: pl.BlockSpec((past, width), lambda b: (b, 0))
    new = lambda width: pl.BlockSpec((s_len, width), lambda b: (rb + b, 0))
    out = lambda width: pl.BlockSpec((lp, width), lambda b: (b, 0))
    return pl.pallas_call(
        functools.partial(_sample_keys_kernel, past=past, s_len=s_len),
        grid=(n_batch,),
        in_specs=[old(nk), old(nk), old(D_IDX), new(nk), new(nk), new(B_XW)],
        out_specs=[out(nk), out(nk), out(LANES), out(LANES)],
        out_shape=[jax.ShapeDtypeStruct((n_batch * lp, nk), BF16), jax.ShapeDtypeStruct((n_batch * lp, nk), BF16),
                   jax.ShapeDtypeStruct((n_batch * lp, LANES), BF16),
                   jax.ShapeDtypeStruct((n_batch * lp, LANES), BF16)],
        compiler_params=_params(("parallel",), 24),
        name="sample_keys",
    )(cache_k, cache_v, cache_ki, k16, v16, x16)


CONV_PW = 2 * DK_C


def _conv_kernel(u_ref, st_ref, w_ref, o_ref, ext_ref, *, t, cw, rows):
    j = pl.program_id(1)
    ext_ref[0:SUBLANES, :] = st_ref[0]
    ext_ref[SUBLANES:SUBLANES + t, :] = u_ref[...]
    off = SUBLANES - (CONV_W - 1)
    for p in range(cw // CONV_PW):
        cols = slice(p * CONV_PW, (p + 1) * CONV_PW)
        col0 = j * cw + p * CONV_PW
        scale = jnp.where(col0 < C_QK_DIM, DK_C ** -0.5, 1.0)
        w = w_ref[:, cols]

        def body(r, carry, cols=cols, col0=col0, scale=scale, w=w):
            base = pl.multiple_of(r * rows, rows)
            blk = ext_ref[pl.ds(base, rows + SUBLANES), cols]
            acc = blk[SUBLANES:, :] * w[CONV_W - 1:CONV_W, :]
            for jj in range(CONV_W - 1):
                tap = pltpu.roll(blk, SUBLANES - off - jj, 0)[SUBLANES:, :]
                acc = acc + tap * w[jj:jj + 1, :]
            a = acc * _sigmoid(acc)
            normed = []
            for hh in range(CONV_PW // DK_C):
                ah = a[:, hh * DK_C:(hh + 1) * DK_C]
                normed.append(ah * (lax.rsqrt(jnp.sum(ah * ah, axis=-1, keepdims=True) + RMS_EPS) * scale))
            o_ref[pl.ds(base, rows), cols] = jnp.where(col0 < 2 * C_QK_DIM, jnp.concatenate(normed, axis=-1), a)
            return carry

        lax.fori_loop(0, t // rows, body, 0)


def _conv(u_all, state8, conv_w, *, row0, n_batch, t, cw, rows):
    rb = row0 // t
    return pl.pallas_call(
        functools.partial(_conv_kernel, t=t, cw=cw, rows=rows),
        grid=(n_batch, C_CONV_DIM // cw),
        in_specs=[pl.BlockSpec((t, cw), lambda b, j: (rb + b, j)),
                  pl.BlockSpec((1, SUBLANES, cw), lambda b, j: (b, 0, j)),
                  pl.BlockSpec((CONV_W, cw), lambda b, j: (0, j))],
        out_specs=pl.BlockSpec((t, cw), lambda b, j: (b, j)),
        out_shape=jax.ShapeDtypeStruct((n_batch * t, C_CONV_DIM), F32),
        scratch_shapes=[pltpu.VMEM((t + SUBLANES, cw), F32)],
        compiler_params=_params(("parallel", "parallel"), 6 * t * cw * 4 // 2 ** 20 + 8),
        name="conv",
    )(u_all, state8, conv_w)


def _gates_kernel(a_ref, at_ref, b_ref, alog_ref, alogt_ref, dt_ref, dtt_ref, gc_ref, gr_ref, beta_ref, *, c, tb):
    def decay(a_raw, a_log, dt):
        x = a_raw + dt
        softplus = jnp.maximum(x, 0.0) + jnp.log1p(jnp.exp(-jnp.abs(x)))
        return -jnp.exp(a_log) * softplus

    def split3(x):
        p0 = x.astype(BF16)
        r = x - p0.astype(F32)
        p1 = r.astype(BF16)
        p2 = (r - p1.astype(F32)).astype(BF16)
        return p0, p1, p2

    i = lax.broadcasted_iota(jnp.int32, (tb, tb), 0)
    j = lax.broadcasted_iota(jnp.int32, (tb, tb), 1)
    same = (i // c) == (j // c)
    lower = jnp.where(same & (j <= i), 1.0, 0.0).astype(BF16)
    upper = jnp.where(same & (i <= j), 1.0, 0.0).astype(BF16)
    g = decay(a_ref[...], alog_ref[...], dt_ref[...])
    gt = decay(at_ref[...], alogt_ref[...], dtt_ref[...])
    g0, g1, g2 = split3(g)
    gc_ref[...] = _dot(lower, g0) + (_dot(lower, g1) + _dot(lower, g2))
    t0, t1, t2 = split3(gt)
    gr_ref[...] = _dot(t0, upper) + (_dot(t1, upper) + _dot(t2, upper))
    beta_ref[...] = _sigmoid(b_ref[...])


def _gates(a_raw, b_raw, a_log, dt_bias, c, tb):
    n = a_raw.shape[0]
    tok = pl.BlockSpec((tb, HV_C), lambda i: (i, 0))
    hed = pl.BlockSpec((HV_C, tb), lambda i: (0, i))
    row = pl.BlockSpec((1, HV_C), lambda i: (0, 0))
    colv = pl.BlockSpec((HV_C, 1), lambda i: (0, 0))
    return pl.pallas_call(
        functools.partial(_gates_kernel, c=c, tb=tb),
        grid=(n // tb,),
        in_specs=[tok, hed, tok, row, colv, row, colv],
        out_specs=[tok, hed, tok],
        out_shape=[jax.ShapeDtypeStruct((n, HV_C), F32), jax.ShapeDtypeStruct((HV_C, n), F32),
                   jax.ShapeDtypeStruct((n, HV_C), F32)],
        compiler_params=_params(("parallel",), 16),
        name="gates",
    )(a_raw, a_raw.T, b_raw, a_log.reshape(1, HV_C), a_log.reshape(HV_C, 1),
      dt_bias.reshape(1, HV_C), dt_bias.reshape(HV_C, 1))


def _delta_kernel(q_ref, k_ref, v_ref, z_ref, gc_ref, gr_ref, beta_ref, s0_ref, nw_ref, *refs, c, hg):
    o_ref, s_ref = refs[-2:]

    @pl.when(pl.program_id(2) == 0)
    def _():
        s_ref[...] = s0_ref[...]

    assert HV_C // HK_C == 2
    c2 = 2 * c
    i = lax.broadcasted_iota(jnp.int32, (c2, c2), 0)
    j = lax.broadcasted_iota(jnp.int32, (c2, c2), 1)
    eye = jnp.where(i == j, 1.0, 0.0)
    same = (i // c) == (j // c)
    top = lax.broadcasted_iota(jnp.int32, (c2, 1), 0) < c
    pairs = range(hg // 2)
    stack = lambda a, b: jnp.concatenate([a, b], axis=0)

    def per_head(x):
        return jnp.concatenate([jnp.where(top, x, 0.0), jnp.where(top, 0.0, x)], axis=1)

    gc_all, gr_all, beta_all = gc_ref[0, 0], gr_ref[0, 0], beta_ref[0, 0]
    q1 = [q_ref[:, n * DK_C:(n + 1) * DK_C] for n in pairs]
    k1 = [k_ref[:, n * DK_C:(n + 1) * DK_C] for n in pairs]
    q = [stack(q1[n], q1[n]) for n in pairs]
    k = [stack(k1[n], k1[n]) for n in pairs]
    v = [stack(v_ref[:, 2 * n * DV_C:(2 * n + 1) * DV_C], v_ref[:, (2 * n + 1) * DV_C:(2 * n + 2) * DV_C]) for n in pairs]
    z = [stack(z_ref[:, 2 * n * DV_C:(2 * n + 1) * DV_C], z_ref[:, (2 * n + 1) * DV_C:(2 * n + 2) * DV_C]) for n in pairs]
    s_old = [(s_ref[2 * n], s_ref[2 * n + 1]) for n in pairs]
    gcol = [stack(gc_all[:, 2 * n:2 * n + 1], gc_all[:, 2 * n + 1:2 * n + 2]) for n in pairs]
    grow = [jnp.concatenate([gr_all[2 * n:2 * n + 1, :], gr_all[2 * n + 1:2 * n + 2, :]], axis=1) for n in pairs]
    beta = [stack(beta_all[:, 2 * n:2 * n + 1], beta_all[:, 2 * n + 1:2 * n + 2]) for n in pairs]
    d_incl = [jnp.exp(jnp.where(same & (i >= j), gcol[n] - grow[n], NEG_INF)) for n in pairs]
    kh = [k[n].astype(BF16) for n in pairs]
    m = [beta[n] * _dot_nt(kh[n], kh[n]) * jnp.where(i > j, d_incl[n], 0.0) for n in pairs]
    t = [eye - jnp.where(i // 2 == j // 2, m[n], 0.0) for n in pairs]
    s = 2
    while s < c:
        join = (i // (2 * s) == j // (2 * s)) & (i // s != j // s)
        tb = [t[n].astype(BF16) for n in pairs]
        x = [_dot(tb[n], jnp.where(join, m[n], 0.0).astype(BF16)) for n in pairs]
        t = [t[n] - _dot(x[n].astype(BF16), tb[n]) for n in pairs]
        s *= 2
    e_g = [jnp.exp(gcol[n]) for n in pairs]
    sol = [_dot(t[n].astype(BF16),
                jnp.concatenate([v[n] * beta[n], k[n] * (beta[n] * e_g[n])], axis=-1).astype(BF16)) for n in pairs]
    s_bf = [stack(s_old[n][0], s_old[n][1]).astype(BF16) for n in pairs]
    u = [sol[n][:, :DV_C] - _dot(per_head(sol[n][:, DV_C:]).astype(BF16), s_bf[n]) for n in pairs]
    ub = [u[n].astype(BF16) for n in pairs]
    qk = [_dot_nt(q[n].astype(BF16), kh[n]) * d_incl[n] for n in pairs]
    o = [_dot(jnp.concatenate([per_head(q[n] * e_g[n]), qk[n]], axis=1).astype(BF16), stack(s_bf[n], ub[n]))
         for n in pairs]
    g_last = [stack(jnp.broadcast_to(gcol[n][c - 1:c, :], (c, 1)), jnp.broadcast_to(gcol[n][c2 - 1:c2, :], (c, 1)))
              for n in pairs]
    ds = [_dot_tn((k[n] * jnp.exp(g_last[n] - gcol[n])).astype(BF16), per_head(u[n]).astype(BF16)) for n in pairs]
    s_new = [(s_old[n][0] * jnp.exp(gcol[n][c - 1:c, :]) + ds[n][:, :DV_C],
              s_old[n][1] * jnp.exp(gcol[n][c2 - 1:c2, :]) + ds[n][:, DV_C:]) for n in pairs]
    o = [o[n] * lax.rsqrt(jnp.mean(o[n] * o[n], axis=-1, keepdims=True) + RMS_EPS) * nw_ref[...] for n in pairs]
    o = [(o[n] * (z[n] * _sigmoid(z[n]))).astype(BF16) for n in pairs]
    for n in pairs:
        s_ref[2 * n] = s_new[n][0]
        s_ref[2 * n + 1] = s_new[n][1]
        o_ref[:, 2 * n * DV_C:(2 * n + 1) * DV_C] = o[n][:c]
        o_ref[:, (2 * n + 1) * DV_C:(2 * n + 2) * DV_C] = o[n][c:]


def _delta(qkv_act, z_all, gc, gr, beta, s0, norm_w, o_all, *, row0, z_col0, n_batch, t, c, hg):
    nc = t // c
    nhg = HV_C // hg
    rep = HV_C // HK_C
    qw = hg // rep * DK_C
    rb = row0 // c
    vmem = 24
    operands = [qkv_act, qkv_act, qkv_act, z_all, gc, gr, beta, s0, norm_w.reshape(1, DV_C)]
    in_specs = [pl.BlockSpec((c, qw), lambda b, g, ci: (b * nc + ci, g)),
                pl.BlockSpec((c, qw), lambda b, g, ci: (b * nc + ci, C_QK_DIM // qw + g)),
                pl.BlockSpec((c, hg * DV_C), lambda b, g, ci: (b * nc + ci, 2 * C_QK_DIM // (hg * DV_C) + g)),
                pl.BlockSpec((c, hg * DV_C), lambda b, g, ci: (rb + b * nc + ci, z_col0 // (hg * DV_C) + g)),
                pl.BlockSpec((1, 1, c, hg), lambda b, g, ci: (b * nc + ci, g, 0, 0)),
                pl.BlockSpec((1, 1, hg, c), lambda b, g, ci: (b * nc + ci, g, 0, 0)),
                pl.BlockSpec((1, 1, c, hg), lambda b, g, ci: (b * nc + ci, g, 0, 0)),
                pl.BlockSpec((hg, DK_C, DV_C), lambda b, g, ci: (b * nhg + g, 0, 0)),
                pl.BlockSpec((1, DV_C), lambda b, g, ci: (0, 0))]
    aliases = {}
    if o_all is not None:
        operands.append(o_all)
        in_specs.append(pl.BlockSpec(memory_space=pl.ANY))
        aliases = {len(operands) - 1: 0}
    return pl.pallas_call(
        functools.partial(_delta_kernel, c=c, hg=hg),
        grid=(n_batch, nhg, nc),
        in_specs=in_specs,
        out_specs=[pl.BlockSpec((c, hg * DV_C), lambda b, g, ci: (rb + b * nc + ci, g)),
                   pl.BlockSpec((hg, DK_C, DV_C), lambda b, g, ci: (b * nhg + g, 0, 0))],
        out_shape=[jax.ShapeDtypeStruct((z_all.shape[0], C_V_DIM), BF16),
                   jax.ShapeDtypeStruct((n_batch * HV_C, DK_C, DV_C), F32)],
        input_output_aliases=aliases,
        compiler_params=_params(("parallel", "parallel", "arbitrary"), vmem),
        name="delta",
    )(*operands)


def _chunk_layout(x, n_chunks, c, nhg, hg, head_major):
    if head_major:
        return x.reshape(nhg, hg, n_chunks, c).transpose(2, 0, 1, 3)
    return x.reshape(n_chunks, c, nhg, hg).transpose(0, 2, 1, 3)


TM = 512
TM_MLP = 1088
TF_MLP = 512
C_HG = 16
B_QBLK = 256
B_KBLK = 512
B_KBLK_S = 384


def _pad_cols(w, width):
    return jnp.pad(w, ((0, 0), (0, width - w.shape[1])))


def kernel(x_prompt, x_sample, cache_a_k, cache_a_v, cache_b_k, cache_b_v, cache_b_kidx, state_c_conv,
           state_c_ssm, a_w_in, a_rel_bias, a_w_out, b_w_in, b_w_out, c_w_in, c_conv_w, c_a_log, c_dt_bias,
           c_norm_w, c_w_out, ln1_g, ln1_b, mlp_w1, mlp_w2, ln2_g, ln2_b):
    nb_p, t_p, _ = x_prompt.shape
    nb_s, t_s, _ = x_sample.shape
    mp = nb_p * t_p
    ms = nb_s * t_s
    y = _join_rows(x_prompt.reshape(mp, D_MODEL), x_sample.reshape(ms, D_MODEL), TM)
    outs = {n: [] for n in ("akp", "avp", "aks", "avs", "bkp", "bvp", "bip", "bks", "bvs", "bis",
                            "ccp", "csp", "ccs", "css")}
    for i in range(DEPTH):
        kind, j = i % N_MIXERS, i // N_MIXERS
        if kind == 0:
            assert t_p % A_QBLK == 0 and t_p >= LEFT_CHUNKS * CHUNK == A_QBLK
            qkv = _mm(y, a_w_in, j, TM, 1536, 3 * D_MODEL // 1536)
            past = cache_a_k.shape[2]
            win = -(-(past + t_s) // LANES) * LANES
            mix_in, k_tail, v_tail = _attn_a_prompt(qkv, _a_bias_rows(a_rel_bias[j], LEFT_CHUNKS * CHUNK),
                                                    _a_band_prompt(), _zeros(mp + ms, D_MODEL, TM), nb_p, t_p)
            mix_in, k_new, v_new = _attn_a_sample(qkv, cache_a_k.reshape(-1, D_MODEL), cache_a_v.reshape(-1, D_MODEL),
                                                  _a_bias_rows(a_rel_bias[j], past), _a_band_sample(past, t_s, win),
                                                  mix_in, j, past, mp, nb_s, t_s)
            outs["akp"].append(k_tail.reshape(nb_p, A_QBLK, H_A, DH_A))
            outs["avp"].append(v_tail.reshape(nb_p, A_QBLK, H_A, DH_A))
            outs["aks"].append(k_new.reshape(nb_s, t_s, H_A, DH_A))
            outs["avs"].append(v_new.reshape(nb_s, t_s, H_A, DH_A))
            w_out = a_w_out
        elif kind == 1:
            nq_, nk_, ni_ = H_B * DH_B, KV_B * DH_B, H_IDX * D_IDX
            w = b_w_in[j]
            w_ki = w[:, nq_ + 2 * nk_ + ni_:nq_ + 2 * nk_ + ni_ + D_IDX]
            w_wi = w[:, nq_ + 2 * nk_ + ni_ + D_IDX:]
            zc = lambda n: jnp.zeros((D_MODEL, n), F32)
            w_x = jnp.concatenate([w_ki, zc(LANES - D_IDX), zc(LANES - D_IDX), w_ki,
                                   w_wi, zc(2 * LANES - H_IDX)], axis=1)
            assert nq_ == 2 * ni_ and 2 * nk_ == ni_
            qq = _mm(y, b_w_in, j, TM, ni_, 3, (BF16,), col_of=lambda t: jnp.where(t < 2, t, t + 1),
                     n_scaled=2, scale=DH_B ** -0.5 * LOG2E)
            k32, k16 = _mm(y, b_w_in, j, TM, nk_, 1, (F32, BF16), col_of=lambda t: nq_ // nk_)
            v32, v16 = _mm(y, b_w_in, j, TM, nk_, 1, (F32, BF16), col_of=lambda t: nq_ // nk_ + 1)
            x32, x16 = _mm(y, w_x[None], 0, TM, B_XW, 1, (F32, BF16))
            k_p, k_s = _split_rows(k32, mp, TM)
            v_p, v_s = _split_rows(v32, mp, TM)
            outs["bkp"].append(k_p.reshape(nb_p, t_p, KV_B, DH_B))
            outs["bvp"].append(v_p.reshape(nb_p, t_p, KV_B, DH_B))
            outs["bip"].append(x32[:mp, :D_IDX].reshape(nb_p, t_p, D_IDX))
            outs["bks"].append(k_s.reshape(nb_s, t_s, KV_B, DH_B))
            outs["bvs"].append(v_s.reshape(nb_s, t_s, KV_B, DH_B))
            outs["bis"].append(x32[mp:, :D_IDX].reshape(nb_s, t_s, D_IDX))
            mix_in = _dsa(qq, x32, k16, v16, x16, x16, _zeros(mp + ms, D_MODEL, TM), ka_col=0, kz_col=1, row0=0,
                          n_batch=nb_p, lp=t_p, nq=B_QBLK, n_qblk=t_p // B_QBLK, kblk=B_KBLK, n_keys=t_p,
                          q_pos0=0, q_step=B_QBLK)
            past = cache_b_k.shape[2]
            n_keys = past + t_s
            lp = -(-n_keys // B_KBLK_S) * B_KBLK_S
            k_s, v_s, ka_s, kz_s = _sample_keys(
                cache_b_k[j].reshape(nb_s * past, nk_), cache_b_v[j].reshape(nb_s * past, nk_),
                cache_b_kidx[j].reshape(nb_s * past, D_IDX), k16, v16, x16,
                row0=mp, n_batch=nb_s, past=past, s_len=t_s, lp=lp)
            mix_in = _dsa(qq, x32, k_s, v_s, ka_s, kz_s, mix_in, ka_col=0, kz_col=0, row0=mp, n_batch=nb_s, lp=lp,
                          nq=t_s, n_qblk=1, kblk=B_KBLK_S, n_keys=n_keys, q_pos0=PAST_LEN, q_step=0)
            w_out = b_w_out
        else:
            n_main = C_CONV_DIM + C_V_DIM
            proj = _mm(y, c_w_in, j, TM, 1536, n_main // 1536)
            gate_raw = _mm_x3(y, _pad_cols(c_w_in[j][:, n_main:], LANES), TM)
            b_raw, a_raw = gate_raw[:, :HV_C], gate_raw[:, HV_C:2 * HV_C]
            nhg = HV_C // C_HG
            mix_in = _zeros(mp + ms, C_V_DIM, TM)
            for (row0, nbt, tt, state, s0, names) in (
                    (0, nb_p, t_p, jnp.zeros((nb_p, CONV_W - 1, C_CONV_DIM), F32),
                     jnp.zeros((nb_p * HV_C, DK_C, DV_C), F32), ("ccp", "csp")),
                    (mp, nb_s, t_s, state_c_conv[j], state_c_ssm[j].reshape(nb_s * HV_C, DK_C, DV_C), ("ccs", "css"))):
                c = min(CHUNK, tt)
                n_rows = nbt * tt
                n_new = min(tt, CONV_W - 1)
                u_tail = jnp.stack([proj[row0 + (b + 1) * tt - n_new:row0 + (b + 1) * tt, :C_CONV_DIM]
                                    for b in range(nbt)])
                outs[names[0]].append(jnp.concatenate([state[:, n_new:], u_tail], axis=1))
                state8 = jnp.pad(state, ((0, 0), (SUBLANES - (CONV_W - 1), 0), (0, 0)))
                act = _conv(proj, state8, c_conv_w[j], row0=row0, n_batch=nbt, t=tt,
                            cw=CONV_PW if tt > 256 else C_CONV_DIM, rows=min(tt, 256))
                gc, gr, beta = _gates(a_raw[row0:row0 + n_rows], b_raw[row0:row0 + n_rows], c_a_log[j], c_dt_bias[j],
                                      c, 512)
                n_chunks = n_rows // c
                mix_in, s_new = _delta(act, proj, _chunk_layout(gc, n_chunks, c, nhg, C_HG, False),
                                       _chunk_layout(gr, n_chunks, c, nhg, C_HG, True),
                                       _chunk_layout(beta, n_chunks, c, nhg, C_HG, False), s0, c_norm_w[j], mix_in,
                                       row0=row0, z_col0=C_CONV_DIM, n_batch=nbt, t=tt, c=c, hg=C_HG)
                outs[names[1]].append(s_new.reshape(nbt, HV_C, DK_C, DV_C))
            w_out = c_w_out
        y = _mm_ln(mix_in, w_out.astype(BF16), j, y, ln1_g[i], ln1_b[i], TM, 1024)
        y = _mlp_ln(y, mlp_w1, mlp_w2, i, ln2_g[i], ln2_b[i], TM_MLP, TF_MLP)
    st = lambda name: jnp.stack(outs[name])
    y_p, y_s = _split_rows(y, mp, TM)
    return (y_p.reshape(nb_p, t_p, D_MODEL), y_s.reshape(nb_s, t_s, D_MODEL),
            st("akp"), st("avp"), st("aks"), st("avs"),
            st("bkp"), st("bvp"), st("bip"), st("bks"), st("bvs"), st("bis"),
            st("ccp"), st("csp"), st("ccs"), st("css"))
```

```python
import functools

import numpy as np
import jax
import jax.numpy as jnp
from jax import lax
from jax.experimental import pallas as pl
from jax.experimental.pallas import tpu as pltpu

D_MODEL = 2048
DEPTH = 4
PAST_LEN = 1024
CHUNK = 64
N_MIXERS = 3
D_FF = 4 * D_MODEL
H_A = 16
DH_A = D_MODEL // H_A
LEFT_CHUNKS = 8
REL_CLIP = 128
H_B = 16
KV_B = 4
DH_B = D_MODEL // H_B
H_IDX = 16
D_IDX = 64
TOPK_MAX = 256
HK_C = 16
HV_C = 32
DK_C = 128
DV_C = 128
CONV_W = 4
C_QK_DIM = HK_C * DK_C
C_V_DIM = HV_C * DV_C
C_CONV_DIM = 2 * C_QK_DIM + C_V_DIM
ALPHA = (2.0 * DEPTH) ** 0.25
LN_EPS = 1e-5
RMS_EPS = 1e-6

LANES = 128
SUBLANES = 8
VMEM_CAP_MB = 56

F32 = jnp.float32
BF16 = jnp.bfloat16
NEG_INF = float("-inf")
LOG2E = 1.4426950408889634
INT_MIN = -(2 ** 31)


def _params(sem, vmem_mb):
    return pltpu.CompilerParams(dimension_semantics=sem,
                                vmem_limit_bytes=min(vmem_mb, VMEM_CAP_MB) * 1024 * 1024)


def _dot(a, b):
    return jnp.dot(a, b, preferred_element_type=F32)


def _dot_nt(a, b):
    return lax.dot_general(a, b, (((1,), (1,)), ((), ())), preferred_element_type=F32)


def _dot_tn(a, b):
    return lax.dot_general(a, b, (((0,), (0,)), ((), ())), preferred_element_type=F32)


def _split2(a):
    hi = a.astype(BF16)
    lo = (a - hi.astype(F32)).astype(BF16)
    return hi, lo


def _dot_x3(a, b):
    ah, al = _split2(a)
    bh, bl = _split2(b)
    return _dot(ah, bh) + (_dot(ah, bl) + _dot(al, bh))


def _layer_norm(z, g, b):
    mu = jnp.mean(z, axis=-1, keepdims=True)
    zc = z - mu
    var = jnp.mean(zc * zc, axis=-1, keepdims=True)
    return zc * lax.rsqrt(var + LN_EPS) * g + b


def _sigmoid(x):
    return 0.5 * jnp.tanh(0.5 * x) + 0.5


def _mm_kernel(x_ref, w_ref, *refs, n_scaled, scale):
    *o_refs, wb_ref = refs

    @pl.when(pl.program_id(1) == 0)
    def _():
        wb_ref[...] = w_ref[0].astype(BF16)

    res = _dot(x_ref[...].astype(BF16), wb_ref[...])
    if n_scaled:
        res = res * jnp.where(pl.program_id(0) < n_scaled, scale, 1.0)
    for o_ref in o_refs:
        o_ref[...] = res.astype(o_ref.dtype)


def _mm(x, w, layer, tm, tn, n_tiles, dtypes=(F32,), col_of=lambda j: j, n_scaled=0, scale=1.0):
    m, k = x.shape
    out_b = sum(jnp.dtype(d).itemsize for d in dtypes)
    vmem = (2 * tm * k * 4 + tm * k * 2 + 2 * k * tn * 4 + k * tn * 2 + 2 * tm * tn * out_b
            + tm * tn * 4) // 2 ** 20 + 4
    outs = pl.pallas_call(
        functools.partial(_mm_kernel, n_scaled=n_scaled, scale=scale),
        grid=(n_tiles, m // tm),
        in_specs=[pl.BlockSpec((tm, k), lambda j, i: (i, 0)),
                  pl.BlockSpec((1, k, tn), lambda j, i: (layer, 0, col_of(j)))],
        out_specs=[pl.BlockSpec((tm, tn), lambda j, i: (i, j)) for _ in dtypes],
        out_shape=[jax.ShapeDtypeStruct((m, n_tiles * tn), d) for d in dtypes],
        scratch_shapes=[pltpu.VMEM((k, tn), BF16)],
        compiler_params=_params(("parallel", "arbitrary"), vmem),
        name="mm",
    )(x, w)
    return outs[0] if len(dtypes) == 1 else outs


def _mm_x3_kernel(x_ref, w_ref, o_ref):
    o_ref[...] = _dot_x3(x_ref[...], w_ref[...])


def _mm_x3(x, w, tm):
    m, k = x.shape
    n = w.shape[1]
    return pl.pallas_call(
        _mm_x3_kernel,
        grid=(m // tm,),
        in_specs=[pl.BlockSpec((tm, k), lambda i: (i, 0)),
                  pl.BlockSpec((k, n), lambda i: (0, 0))],
        out_specs=pl.BlockSpec((tm, n), lambda i: (i, 0)),
        out_shape=jax.ShapeDtypeStruct((m, n), F32),
        compiler_params=_params(("parallel",), 24),
        name="mm_x3",
    )(x, w)


def _mm_ln_kernel(x_ref, w_ref, r_ref, g_ref, b_ref, o_ref, acc_ref, *, rows):
    k = pl.program_id(1)

    @pl.when(k == 0)
    def _():
        acc_ref[...] = jnp.zeros_like(acc_ref)

    acc_ref[...] += _dot(x_ref[...], w_ref[0])

    @pl.when(k == pl.num_programs(1) - 1)
    def _():
        def norm_rows(r, carry):
            sl = pl.ds(pl.multiple_of(r * rows, rows), rows)
            o_ref[sl, :] = _layer_norm(ALPHA * r_ref[sl, :] + acc_ref[sl, :], g_ref[...], b_ref[...])
            return carry

        lax.fori_loop(0, o_ref.shape[0] // rows, norm_rows, 0)


def _mm_ln(x, w, layer, resid, g, b, tm, tk):
    m, k = x.shape
    d = w.shape[2]
    vmem = (2 * tm * tk * 2 + 2 * tk * d * 2 + 6 * tm * d * 4) // 2 ** 20 + 4
    return pl.pallas_call(
        functools.partial(_mm_ln_kernel, rows=tm // 4),
        grid=(m // tm, k // tk),
        in_specs=[pl.BlockSpec((tm, tk), lambda i, kk: (i, kk)),
                  pl.BlockSpec((1, tk, d), lambda i, kk: (layer, kk, 0)),
                  pl.BlockSpec((tm, d), lambda i, kk: (i, 0)),
                  pl.BlockSpec((1, d), lambda i, kk: (0, 0)),
                  pl.BlockSpec((1, d), lambda i, kk: (0, 0))],
        out_specs=pl.BlockSpec((tm, d), lambda i, kk: (i, 0)),
        out_shape=jax.ShapeDtypeStruct((m, d), F32),
        scratch_shapes=[pltpu.VMEM((tm, d), F32)],
        compiler_params=_params(("parallel", "arbitrary"), vmem),
        name="mm_ln",
    )(x, w, resid, g.reshape(1, d), b.reshape(1, d))


def _join_kernel(a_ref, b_ref, o_ref, *, n_a):
    i = pl.program_id(0)

    @pl.when(i < n_a)
    def _():
        o_ref[...] = a_ref[...]

    @pl.when(i >= n_a)
    def _():
        o_ref[...] = b_ref[...]


def _join_rows(a, b, tm):
    (ma, d), mb = a.shape, b.shape[0]
    n_a = ma // tm
    return pl.pallas_call(
        functools.partial(_join_kernel, n_a=n_a),
        grid=(n_a + mb // tm,),
        in_specs=[pl.BlockSpec((tm, d), lambda i: (jnp.minimum(i, n_a - 1), 0)),
                  pl.BlockSpec((tm, d), lambda i: (jnp.maximum(i - n_a, 0), 0))],
        out_specs=pl.BlockSpec((tm, d), lambda i: (i, 0)),
        out_shape=jax.ShapeDtypeStruct((ma + mb, d), a.dtype),
        compiler_params=_params(("arbitrary",), 6 * tm * d * a.dtype.itemsize // 2 ** 20 + 4),
        name="join_rows",
    )(a, b)


def _split_kernel(x_ref, a_ref, b_ref, *, n_a):
    i = pl.program_id(0)

    @pl.when(i < n_a)
    def _():
        a_ref[...] = x_ref[...]

    @pl.when(i >= n_a)
    def _():
        b_ref[...] = x_ref[...]


def _split_rows(x, ma, tm):
    m, d = x.shape
    n_a = ma // tm
    return pl.pallas_call(
        functools.partial(_split_kernel, n_a=n_a),
        grid=(m // tm,),
        in_specs=[pl.BlockSpec((tm, d), lambda i: (i, 0))],
        out_specs=[pl.BlockSpec((tm, d), lambda i: (jnp.minimum(i, n_a - 1), 0)),
                   pl.BlockSpec((tm, d), lambda i: (jnp.maximum(i - n_a, 0), 0))],
        out_shape=[jax.ShapeDtypeStruct((ma, d), x.dtype), jax.ShapeDtypeStruct((m - ma, d), x.dtype)],
        compiler_params=_params(("arbitrary",), 6 * tm * d * x.dtype.itemsize // 2 ** 20 + 4),
        name="split_rows",
    )(x)


def _fill_kernel(o_ref):
    o_ref[...] = jnp.zeros(o_ref.shape, o_ref.dtype)


def _zeros(m, d, tm):
    return pl.pallas_call(
        _fill_kernel,
        grid=(m // tm,),
        out_specs=pl.BlockSpec((tm, d), lambda i: (i, 0)),
        out_shape=jax.ShapeDtypeStruct((m, d), BF16),
        compiler_params=_params(("parallel",), 2 * tm * d * 2 // 2 ** 20 + 4),
        name="fill",
    )()


def _mlp_ln_kernel(y_ref, w1_ref, w2_ref, g_ref, b_ref, o_ref, o16_ref, xb_ref, *, rows):
    f = pl.program_id(1)

    @pl.when(f == 0)
    def _():
        xb_ref[...] = y_ref[...].astype(BF16)
        o_ref[...] = jnp.zeros_like(o_ref)

    h = jnp.maximum(_dot(xb_ref[...], w1_ref[0].astype(BF16)), 0.0)
    o_ref[...] += _dot((h * h).astype(BF16), w2_ref[0].astype(BF16))

    @pl.when(f == pl.num_programs(1) - 1)
    def _():
        def norm_rows(r, carry):
            sl = pl.ds(pl.multiple_of(r * rows, rows), rows)
            out = _layer_norm(ALPHA * y_ref[sl, :] + o_ref[sl, :], g_ref[...], b_ref[...])
            o_ref[sl, :] = out
            o16_ref[sl, :] = out.astype(BF16)
            return carry

        lax.fori_loop(0, o_ref.shape[0] // rows, norm_rows, 0)


def _mlp_ln(y, w1, w2, layer, g, b, tm, tf):
    m, d = y.shape
    f = w1.shape[2]
    vmem = (3 * tm * d * 4 + 2 * tm * d * 2 + 4 * d * tf * 4 + 2 * d * tf * 2 + 3 * tm * tf * 4) // 2 ** 20 + 12
    once = pl.Buffered(1)
    return pl.pallas_call(
        functools.partial(_mlp_ln_kernel, rows=tm // 4),
        grid=(m // tm, f // tf),
        in_specs=[pl.BlockSpec((tm, d), lambda i, ff: (i, 0)),
                  pl.BlockSpec((1, d, tf), lambda i, ff: (layer, 0, ff)),
                  pl.BlockSpec((1, tf, d), lambda i, ff: (layer, ff, 0)),
                  pl.BlockSpec((1, d), lambda i, ff: (0, 0)),
                  pl.BlockSpec((1, d), lambda i, ff: (0, 0))],
        out_specs=[pl.BlockSpec((tm, d), lambda i, ff: (i, 0), pipeline_mode=once),
                   pl.BlockSpec((tm, d), lambda i, ff: (i, 0), pipeline_mode=once)],
        out_shape=[jax.ShapeDtypeStruct((m, d), F32), jax.ShapeDtypeStruct((m, d), BF16)],
        scratch_shapes=[pltpu.VMEM((tm, d), BF16)],
        compiler_params=_params(("parallel", "arbitrary"), vmem),
        name="mlp_ln",
    )(y, w1, w2, g.reshape(1, d), b.reshape(1, d))


A_QBLK = 512
A_PAIR = 2 * CHUNK
A_WIN = (LEFT_CHUNKS + 2) * CHUNK
A_UW = A_WIN + A_PAIR


def _softmax_pv(s2, v):
    m = jnp.max(s2, axis=-1, keepdims=True)
    p = jnp.exp2(s2 - m)
    l = jnp.sum(p, axis=-1, keepdims=True)
    return _dot(p.astype(BF16), v) / l


def _bias_table(u_row, band, rows):
    u = jnp.broadcast_to(u_row * LOG2E, (rows, A_UW))
    return pltpu.roll(u, A_UW - A_PAIR + 1, 1, stride=1, stride_axis=0)[:, :band.shape[1]] + band


def _attn_a_prompt_kernel(q_ref, kp_ref, kc_ref, vp_ref, vc_ref, u_ref, band_ref, o_hbm_ref, o_ref, ko_ref, vo_ref,
                          kw_ref, vw_ref):
    del o_hbm_ref
    i = pl.program_id(2)

    @pl.when(i == pl.num_programs(2) - 1)
    def _():
        ko_ref[...] = kc_ref[...]
        vo_ref[...] = vc_ref[...]

    kw_ref[0:A_QBLK, :] = kp_ref[...].astype(BF16)
    kw_ref[A_QBLK:2 * A_QBLK, :] = kc_ref[...].astype(BF16)
    vw_ref[0:A_QBLK, :] = vp_ref[...].astype(BF16)
    vw_ref[A_QBLK:2 * A_QBLK, :] = vc_ref[...].astype(BF16)
    bias = _bias_table(u_ref[0], band_ref[...], A_PAIR)
    col = lax.broadcasted_iota(jnp.int32, (A_PAIR, A_WIN), 1)
    pairs = range(A_QBLK // A_PAIR)
    q = [(q_ref[c * A_PAIR:(c + 1) * A_PAIR, :] * (DH_A ** -0.5 * LOG2E)).astype(BF16) for c in pairs]
    s = [_dot_nt(q[c], kw_ref[c * A_PAIR:c * A_PAIR + A_WIN, :]) + bias for c in pairs]
    s = [jnp.where(col + (i * A_QBLK - A_QBLK + c * A_PAIR) >= 0, s[c], NEG_INF) for c in pairs]
    o = [_softmax_pv(s[c], vw_ref[c * A_PAIR:c * A_PAIR + A_WIN, :]).astype(BF16) for c in pairs]
    for c in pairs:
        o_ref[c * A_PAIR:(c + 1) * A_PAIR, :] = o[c]


def _attn_a_prompt(qkv, u_rows, band, o_all, n_batch, t):
    nblk = t // A_QBLK

    def spec(col0, prev):
        if prev:
            return pl.BlockSpec((A_QBLK, DH_A), lambda b, h, i: (b * nblk + jnp.maximum(i - 1, 0), col0 + h))
        return pl.BlockSpec((A_QBLK, DH_A), lambda b, h, i: (b * nblk + i, col0 + h))

    tail = pl.BlockSpec((A_QBLK, DH_A), lambda b, h, i: (b, h))
    return pl.pallas_call(
        _attn_a_prompt_kernel,
        grid=(n_batch, H_A, nblk),
        in_specs=[spec(0, False), spec(H_A, True), spec(H_A, False), spec(2 * H_A, True), spec(2 * H_A, False),
                  pl.BlockSpec((1, 1, A_UW), lambda b, h, i: (h, 0, 0)),
                  pl.BlockSpec((A_PAIR, A_WIN), lambda b, h, i: (0, 0)),
                  pl.BlockSpec(memory_space=pl.ANY)],
        out_specs=[pl.BlockSpec((A_QBLK, DH_A), lambda b, h, i: (b * nblk + i, h)), tail, tail],
        out_shape=[jax.ShapeDtypeStruct(o_all.shape, BF16),
                   jax.ShapeDtypeStruct((n_batch * A_QBLK, D_MODEL), F32),
                   jax.ShapeDtypeStruct((n_batch * A_QBLK, D_MODEL), F32)],
        input_output_aliases={7: 0},
        scratch_shapes=[pltpu.VMEM((2 * A_QBLK, DH_A), BF16), pltpu.VMEM((2 * A_QBLK, DH_A), BF16)],
        compiler_params=_params(("parallel", "parallel", "arbitrary"), 16),
        name="attn_a_prompt",
    )(qkv, qkv, qkv, qkv, qkv, u_rows, band, o_all)


def _attn_a_sample_kernel(q_ref, kc_ref, kn_ref, vc_ref, vn_ref, u_ref, band_ref, o_hbm_ref, o_ref, ko_ref, vo_ref,
                          kw_ref, vw_ref, *, past, s_len):
    del o_hbm_ref
    ko_ref[...] = kn_ref[...]
    vo_ref[...] = vn_ref[...]
    win = kw_ref.shape[0]
    pad = win - past - s_len
    for h in range(H_A):
        cols = slice(h * DH_A, (h + 1) * DH_A)
        kw_ref[0:past, :] = kc_ref[:, cols].astype(BF16)
        kw_ref[past:past + s_len, :] = kn_ref[:, cols].astype(BF16)
        kw_ref[past + s_len:, :] = jnp.zeros((pad, DH_A), BF16)
        vw_ref[0:past, :] = vc_ref[:, cols].astype(BF16)
        vw_ref[past:past + s_len, :] = vn_ref[:, cols].astype(BF16)
        vw_ref[past + s_len:, :] = jnp.zeros((pad, DH_A), BF16)
        q = (q_ref[:, cols] * (DH_A ** -0.5 * LOG2E)).astype(BF16)
        s = _dot_nt(q, kw_ref[...]) + _bias_table(u_ref[h], band_ref[...], s_len)
        o_ref[:, cols] = _softmax_pv(s, vw_ref[...]).astype(BF16)


def _attn_a_sample(qkv, k_cache, v_cache, u_rows, band, o_all, layer, past, row0, n_batch, s_len):
    win = band.shape[1]
    rb = row0 // s_len
    new = lambda colblk: pl.BlockSpec((s_len, D_MODEL), lambda b: (rb + b, colblk))
    old = pl.BlockSpec((past, D_MODEL), lambda b: (layer * n_batch + b, 0))
    rows = pl.BlockSpec((s_len, D_MODEL), lambda b: (b, 0))
    return pl.pallas_call(
        functools.partial(_attn_a_sample_kernel, past=past, s_len=s_len),
        grid=(n_batch,),
        in_specs=[new(0), old, new(1), old, new(2),
                  pl.BlockSpec((H_A, 1, A_UW), lambda b: (0, 0, 0)),
                  pl.BlockSpec((s_len, win), lambda b: (0, 0)),
                  pl.BlockSpec(memory_space=pl.ANY)],
        out_specs=[pl.BlockSpec((s_len, D_MODEL), lambda b: (rb + b, 0)), rows, rows],
        out_shape=[jax.ShapeDtypeStruct(o_all.shape, BF16),
                   jax.ShapeDtypeStruct((n_batch * s_len, D_MODEL), F32),
                   jax.ShapeDtypeStruct((n_batch * s_len, D_MODEL), F32)],
        input_output_aliases={7: 0},
        scratch_shapes=[pltpu.VMEM((win, DH_A), BF16), pltpu.VMEM((win, DH_A), BF16)],
        compiler_params=_params(("parallel",), 32),
        name="attn_a_sample",
    )(qkv, k_cache, qkv, v_cache, qkv, u_rows, band, o_all)


def _a_bias_rows(rel_bias, q0):
    d = q0 + A_PAIR - 1 - np.arange(A_UW)
    return rel_bias[:, np.clip(d, -REL_CLIP, REL_CLIP) + REL_CLIP][:, None, :]


def _a_band_prompt():
    qc = LEFT_CHUNKS * CHUNK + np.arange(A_PAIR)[:, None]
    kc = np.arange(A_WIN)[None, :]
    vis = (kc // CHUNK <= qc // CHUNK) & (kc // CHUNK >= qc // CHUNK - LEFT_CHUNKS)
    return jnp.asarray(np.where(vis, 0.0, -np.inf).astype(np.float32))


def _a_band_sample(past, s_len, win):
    q_pos = PAST_LEN + np.arange(s_len)[:, None]
    k_pos = PAST_LEN - past + np.arange(win)[None, :]
    vis = ((k_pos >= 0) & (k_pos // CHUNK <= q_pos // CHUNK) & (k_pos // CHUNK >= q_pos // CHUNK - LEFT_CHUNKS)
           & (np.arange(win)[None, :] < past + s_len))
    return jnp.asarray(np.where(vis, 0.0, -np.inf).astype(np.float32))


B_XW = 4 * LANES


def _key_to_float(key):
    return lax.bitcast_convert_type(jnp.where(key >= 0, key, key ^ jnp.int32(0x7FFFFFFF)), F32)


KEY_NEG_INF = int(np.array(-np.inf, np.float32).view(np.int32)) ^ 0x7FFFFFFF


def _lane_fold(x):
    acc = x[:, 0:LANES]
    for t in range(1, x.shape[1] // LANES):
        acc = acc + x[:, t * LANES:(t + 1) * LANES]
    return acc


def _dsa_kernel(q_ref, qi_ref, wi_ref, k_ref, v_ref, ka_ref, kb_ref, *refs,
                nq, kblk, n_keys, q_pos0, q_step, n_sel, idx_bits):
    o_ref, sc_ref, msk_ref, qs_ref, m_ref, l_ref, acc_ref = refs[-7:]
    qbase = q_pos0 + pl.program_id(1) * q_step
    n_adm = jnp.minimum(((qbase + nq - 1) // CHUNK + 1) * CHUNK, n_keys)
    nb = (n_adm + kblk - 1) // kblk
    qpos = qbase + lax.broadcasted_iota(jnp.int32, (nq, kblk), 0)
    col0 = lax.broadcasted_iota(jnp.int32, (nq, kblk), 1)

    wsc = wi_ref[...] * (H_IDX ** -0.5 * D_IDX ** -0.5)

    def score_block(kb, carry):
        r = pl.multiple_of(kb * kblk, kblk)
        ka = ka_ref[pl.ds(r, kblk), :]
        kz = kb_ref[pl.ds(r, kblk), :]
        acc = jnp.zeros((nq, kblk), F32)
        for p in range(H_IDX // 2):
            q2 = qi_ref[:, p * LANES:(p + 1) * LANES]
            acc = acc + wsc[:, 2 * p:2 * p + 1] * jnp.maximum(_dot_nt(q2, ka), 0.0)
            acc = acc + wsc[:, 2 * p + 1:2 * p + 2] * jnp.maximum(_dot_nt(q2, kz), 0.0)
        col = col0 + kb * kblk
        adm = (col // CHUNK <= qpos // CHUNK) & (col < n_keys)
        sc_ref[kb] = jnp.where(adm, acc, NEG_INF)
        return carry

    lax.fori_loop(0, nb, score_block, 0)

    def count(pred_fn):
        def body(kb, c):
            return c + _lane_fold(jnp.where(pred_fn(kb, sc_ref[kb]), 1.0, 0.0))
        c = lax.fori_loop(0, nb, body, jnp.zeros((nq, LANES), F32))
        return jnp.sum(c, axis=-1, keepdims=True)

    def bit_step(bi, pre):
        cand = pre | jnp.left_shift(jnp.int32(1), 31 - bi)
        cand_s = cand ^ jnp.int32(INT_MIN)
        thr_c = _key_to_float(cand_s)
        cnt = count(lambda kb, sc: sc >= thr_c)
        return jnp.where((cnt >= n_sel) | (cand_s <= KEY_NEG_INF), cand, pre)

    pre = lax.fori_loop(0, 32, bit_step, jnp.zeros((nq, 1), jnp.int32))
    thr = _key_to_float(pre ^ jnp.int32(INT_MIN))
    need = n_sel - count(lambda kb, sc: sc > thr)
    n_ge = count(lambda kb, sc: sc >= thr)

    def idx_step(bi, lim):
        cand = lim | jnp.left_shift(jnp.int32(1), idx_bits - 1 - bi)
        cnt = count(lambda kb, sc: (sc == thr) & (col0 + kb * kblk < cand))
        return jnp.where(cnt < need, cand, lim)

    tied = jnp.max(jnp.where(thr > NEG_INF, n_ge, 0.0)) > n_sel
    lim = lax.cond(tied,
                   lambda: lax.fori_loop(0, idx_bits, idx_step, jnp.zeros((nq, 1), jnp.int32)),
                   lambda: jnp.full((nq, 1), 2 ** idx_bits, jnp.int32))

    def mask_block(kb, carry):
        sc = sc_ref[kb]
        sel = (sc > thr) | ((sc == thr) & (col0 + kb * kblk <= lim))
        sel = sel & (sc > NEG_INF) & (sc < float("inf"))
        msk_ref[kb] = jnp.where(sel, 0.0, NEG_INF)
        return carry

    lax.fori_loop(0, nb, mask_block, 0)

    group = H_B // KV_B
    gq = group * nq
    for h in range(H_B):
        qs_ref[h * nq:(h + 1) * nq, :] = q_ref[:, h * DH_B:(h + 1) * DH_B]
    m_ref[...] = jnp.full(m_ref.shape, NEG_INF, F32)
    l_ref[...] = jnp.zeros(l_ref.shape, F32)
    acc_ref[...] = jnp.zeros(acc_ref.shape, F32)

    def attend(kb, carry):
        r = pl.multiple_of(kb * kblk, kblk)
        msk = msk_ref[kb]
        s = [_dot_nt(qs_ref[g * gq:(g + 1) * gq, :], k_ref[pl.ds(r, kblk), g * DH_B:(g + 1) * DH_B])
             for g in range(KV_B)]
        for g in range(KV_B):
            ps = []
            for hq in range(group):
                rows = slice((g * group + hq) * nq, (g * group + hq + 1) * nq)
                sh = s[g][hq * nq:(hq + 1) * nq] + msk
                m_old = m_ref[rows]
                m_new = jnp.maximum(m_old, jnp.max(sh, axis=-1, keepdims=True))
                m_use = jnp.where(m_new == NEG_INF, 0.0, m_new)
                p = jnp.exp2(sh - m_use)
                a = jnp.exp2(m_old - m_use)
                m_ref[rows] = m_new
                l_ref[rows] = a * l_ref[rows] + jnp.sum(p, axis=-1, keepdims=True)
                acc_ref[rows] = a * acc_ref[rows]
                ps.append(p.astype(BF16))
            acc_ref[g * gq:(g + 1) * gq, :] += _dot(jnp.concatenate(ps, axis=0),
                                                    v_ref[pl.ds(r, kblk), g * DH_B:(g + 1) * DH_B])
        return carry

    lax.fori_loop(0, nb, attend, 0)
    for h in range(H_B):
        rows = slice(h * nq, (h + 1) * nq)
        o_ref[:, h * DH_B:(h + 1) * DH_B] = (acc_ref[rows] / l_ref[rows]).astype(BF16)


def _dsa(qq, x32, k, v, ka, kz, o_all, *, ka_col, kz_col, row0, n_batch, lp, nq, n_qblk, kblk, n_keys,
         q_pos0, q_step):
    nkb = lp // kblk
    n_sel = min(TOPK_MAX, n_keys // 4)
    rb = row0 // nq
    qspec = lambda width, colblk: pl.BlockSpec((nq, width), lambda b, i: (rb + b * n_qblk + i, colblk))
    kspec = lambda width, colblk: pl.BlockSpec((lp, width), lambda b, i: (b, colblk))
    kern = functools.partial(_dsa_kernel, nq=nq, kblk=kblk, n_keys=n_keys, q_pos0=q_pos0, q_step=q_step,
                             n_sel=n_sel, idx_bits=max(1, int(np.ceil(np.log2(lp)))))
    vmem = (2 * 2 * lp * (2 * KV_B * DH_B + 2 * LANES) + 2 * nkb * nq * kblk * 4
            + H_B * nq * (DH_B * 6 + 2 * LANES * 4) + 4 * nq * (qq.shape[1] + D_MODEL) * 2) // 2 ** 20 + 8
    operands = [qq, qq, x32, k, v, ka, kz]
    in_specs = [qspec(H_B * DH_B, 0), qspec(H_IDX * D_IDX, H_B * DH_B // (H_IDX * D_IDX)), qspec(LANES, 2),
                kspec(KV_B * DH_B, 0), kspec(KV_B * DH_B, 0), kspec(LANES, ka_col), kspec(LANES, kz_col)]
    aliases = {}
    if o_all is not None:
        operands.append(o_all)
        in_specs.append(pl.BlockSpec(memory_space=pl.ANY))
        aliases = {len(operands) - 1: 0}
    return pl.pallas_call(
        kern,
        grid=(n_batch, n_qblk),
        in_specs=in_specs,
        out_specs=pl.BlockSpec((nq, D_MODEL), lambda b, i: (rb + b * n_qblk + i, 0)),
        out_shape=jax.ShapeDtypeStruct((qq.shape[0], D_MODEL), BF16),
        input_output_aliases=aliases,
        scratch_shapes=[pltpu.VMEM((nkb, nq, kblk), F32),
                        pltpu.VMEM((nkb, nq, kblk), F32), pltpu.VMEM((H_B * nq, DH_B), BF16),
                        pltpu.VMEM((H_B * nq, 1), F32), pltpu.VMEM((H_B * nq, 1), F32),
                        pltpu.VMEM((H_B * nq, DH_B), F32)],
        compiler_params=_params(("parallel", "arbitrary"), vmem),
        name="dsa",
    )(*operands)


def _sample_keys_kernel(ck_ref, cv_ref, ci_ref, kn_ref, vn_ref, xn_ref, k_ref, v_ref, ka_ref, kz_ref,
                        *, past, s_len):
    end = past + s_len
    pad = k_ref.shape[0] - end
    k_ref[0:past, :] = ck_ref[...].astype(BF16)
    k_ref[past:end, :] = kn_ref[...]
    k_ref[end:, :] = jnp.zeros((pad, k_ref.shape[1]), BF16)
    v_ref[0:past, :] = cv_ref[...].astype(BF16)
    v_ref[past:end, :] = vn_ref[...]
    v_ref[end:, :] = jnp.zeros((pad, v_ref.shape[1]), BF16)
    ci = ci_ref[...]
    zero = jnp.zeros((past, LANES - D_IDX), F32)
    ka_ref[0:past, :] = jnp.concatenate([ci, zero], axis=1).astype(BF16)
    kz_ref[0:past, :] = jnp.concatenate([zero, ci], axis=1).astype(BF16)
    ka_ref[past:end, :] = xn_ref[:, 0:LANES]
    kz_ref[past:end, :] = xn_ref[:, LANES:2 * LANES]
    ka_ref[end:, :] = jnp.zeros((pad, LANES), BF16)
    kz_ref[end:, :] = jnp.zeros((pad, LANES), BF16)


def _sample_keys(cache_k, cache_v, cache_ki, k16, v16, x16, *, row0, n_batch, past, s_len, lp):
    rb = row0 // s_len
    nk = KV_B * DH_B
    old = lambda width: pl.BlockSpec((past, width), lambda b: (b, 0))
    new = lambda width: pl.BlockSpec((s_len, width), lambda b: (rb + b, 0))
    out = lambda width: pl.BlockSpec((lp, width), lambda b: (b, 0))
    return pl.pallas_call(
        functools.partial(_sample_keys_kernel, past=past, s_len=s_len),
        grid=(n_batch,),
        in_specs=[old(nk), old(nk), old(D_IDX), new(nk), new(nk), new(B_XW)],
        out_specs=[out(nk), out(nk), out(LANES), out(LANES)],
        out_shape=[jax.ShapeDtypeStruct((n_batch * lp, nk), BF16), jax.ShapeDtypeStruct((n_batch * lp, nk), BF16),
                   jax.ShapeDtypeStruct((n_batch * lp, LANES), BF16),
                   jax.ShapeDtypeStruct((n_batch * lp, LANES), BF16)],
        compiler_params=_params(("parallel",), 24),
        name="sample_keys",
    )(cache_k, cache_v, cache_ki, k16, v16, x16)


CONV_PW = 2 * DK_C


def _conv_kernel(u_ref, st_ref, w_ref, o_ref, ext_ref, *, t, cw, rows):
    j = pl.program_id(1)
    ext_ref[0:SUBLANES, :] = st_ref[0]
    ext_ref[SUBLANES:SUBLANES + t, :] = u_ref[...]
    off = SUBLANES - (CONV_W - 1)
    for p in range(cw // CONV_PW):
        cols = slice(p * CONV_PW, (p + 1) * CONV_PW)
        col0 = j * cw + p * CONV_PW
        scale = jnp.where(col0 < C_QK_DIM, DK_C ** -0.5, 1.0)
        w = w_ref[:, cols]

        def body(r, carry, cols=cols, col0=col0, scale=scale, w=w):
            base = pl.multiple_of(r * rows, rows)
            blk = ext_ref[pl.ds(base, rows + SUBLANES), cols]
            acc = blk[SUBLANES:, :] * w[CONV_W - 1:CONV_W, :]
            for jj in range(CONV_W - 1):
                tap = pltpu.roll(blk, SUBLANES - off - jj, 0)[SUBLANES:, :]
                acc = acc + tap * w[jj:jj + 1, :]
            a = acc * _sigmoid(acc)
            normed = []
            for hh in range(CONV_PW // DK_C):
                ah = a[:, hh * DK_C:(hh + 1) * DK_C]
                normed.append(ah * (lax.rsqrt(jnp.sum(ah * ah, axis=-1, keepdims=True) + RMS_EPS) * scale))
            o_ref[pl.ds(base, rows), cols] = jnp.where(col0 < 2 * C_QK_DIM, jnp.concatenate(normed, axis=-1), a)
            return carry

        lax.fori_loop(0, t // rows, body, 0)


def _conv(u_all, state8, conv_w, *, row0, n_batch, t, cw, rows):
    rb = row0 // t
    return pl.pallas_call(
        functools.partial(_conv_kernel, t=t, cw=cw, rows=rows),
        grid=(n_batch, C_CONV_DIM // cw),
        in_specs=[pl.BlockSpec((t, cw), lambda b, j: (rb + b, j)),
                  pl.BlockSpec((1, SUBLANES, cw), lambda b, j: (b, 0, j)),
                  pl.BlockSpec((CONV_W, cw), lambda b, j: (0, j))],
        out_specs=pl.BlockSpec((t, cw), lambda b, j: (b, j)),
        out_shape=jax.ShapeDtypeStruct((n_batch * t, C_CONV_DIM), F32),
        scratch_shapes=[pltpu.VMEM((t + SUBLANES, cw), F32)],
        compiler_params=_params(("parallel", "parallel"), 6 * t * cw * 4 // 2 ** 20 + 8),
        name="conv",
    )(u_all, state8, conv_w)


def _gates_kernel(a_ref, at_ref, b_ref, alog_ref, alogt_ref, dt_ref, dtt_ref, gc_ref, gr_ref, beta_ref, *, c, tb):
    def decay(a_raw, a_log, dt):
        x = a_raw + dt
        softplus = jnp.maximum(x, 0.0) + jnp.log1p(jnp.exp(-jnp.abs(x)))
        return -jnp.exp(a_log) * softplus

    def split3(x):
        p0 = x.astype(BF16)
        r = x - p0.astype(F32)
        p1 = r.astype(BF16)
        p2 = (r - p1.astype(F32)).astype(BF16)
        return p0, p1, p2

    i = lax.broadcasted_iota(jnp.int32, (tb, tb), 0)
    j = lax.broadcasted_iota(jnp.int32, (tb, tb), 1)
    same = (i // c) == (j // c)
    lower = jnp.where(same & (j <= i), 1.0, 0.0).astype(BF16)
    upper = jnp.where(same & (i <= j), 1.0, 0.0).astype(BF16)
    g = decay(a_ref[...], alog_ref[...], dt_ref[...])
    gt = decay(at_ref[...], alogt_ref[...], dtt_ref[...])
    g0, g1, g2 = split3(g)
    gc_ref[...] = _dot(lower, g0) + (_dot(lower, g1) + _dot(lower, g2))
    t0, t1, t2 = split3(gt)
    gr_ref[...] = _dot(t0, upper) + (_dot(t1, upper) + _dot(t2, upper))
    beta_ref[...] = _sigmoid(b_ref[...])


def _gates(a_raw, b_raw, a_log, dt_bias, c, tb):
    n = a_raw.shape[0]
    tok = pl.BlockSpec((tb, HV_C), lambda i: (i, 0))
    hed = pl.BlockSpec((HV_C, tb), lambda i: (0, i))
    row = pl.BlockSpec((1, HV_C), lambda i: (0, 0))
    colv = pl.BlockSpec((HV_C, 1), lambda i: (0, 0))
    return pl.pallas_call(
        functools.partial(_gates_kernel, c=c, tb=tb),
        grid=(n // tb,),
        in_specs=[tok, hed, tok, row, colv, row, colv],
        out_specs=[tok, hed, tok],
        out_shape=[jax.ShapeDtypeStruct((n, HV_C), F32), jax.ShapeDtypeStruct((HV_C, n), F32),
                   jax.ShapeDtypeStruct((n, HV_C), F32)],
        compiler_params=_params(("parallel",), 16),
        name="gates",
    )(a_raw, a_raw.T, b_raw, a_log.reshape(1, HV_C), a_log.reshape(HV_C, 1),
      dt_bias.reshape(1, HV_C), dt_bias.reshape(HV_C, 1))


def _delta_kernel(q_ref, k_ref, v_ref, z_ref, gc_ref, gr_ref, beta_ref, s0_ref, nw_ref, *refs, c, hg):
    o_ref, s_ref = refs[-2:]

    @pl.when(pl.program_id(2) == 0)
    def _():
        s_ref[...] = s0_ref[...]

    assert HV_C // HK_C == 2
    c2 = 2 * c
    i = lax.broadcasted_iota(jnp.int32, (c2, c2), 0)
    j = lax.broadcasted_iota(jnp.int32, (c2, c2), 1)
    eye = jnp.where(i == j, 1.0, 0.0)
    same = (i // c) == (j // c)
    top = lax.broadcasted_iota(jnp.int32, (c2, 1), 0) < c
    pairs = range(hg // 2)
    stack = lambda a, b: jnp.concatenate([a, b], axis=0)

    def per_head(x):
        return jnp.concatenate([jnp.where(top, x, 0.0), jnp.where(top, 0.0, x)], axis=1)

    gc_all, gr_all, beta_all = gc_ref[0, 0], gr_ref[0, 0], beta_ref[0, 0]
    q1 = [q_ref[:, n * DK_C:(n + 1) * DK_C] for n in pairs]
    k1 = [k_ref[:, n * DK_C:(n + 1) * DK_C] for n in pairs]
    q = [stack(q1[n], q1[n]) for n in pairs]
    k = [stack(k1[n], k1[n]) for n in pairs]
    v = [stack(v_ref[:, 2 * n * DV_C:(2 * n + 1) * DV_C], v_ref[:, (2 * n + 1) * DV_C:(2 * n + 2) * DV_C]) for n in pairs]
    z = [stack(z_ref[:, 2 * n * DV_C:(2 * n + 1) * DV_C], z_ref[:, (2 * n + 1) * DV_C:(2 * n + 2) * DV_C]) for n in pairs]
    s_old = [(s_ref[2 * n], s_ref[2 * n + 1]) for n in pairs]
    gcol = [stack(gc_all[:, 2 * n:2 * n + 1], gc_all[:, 2 * n + 1:2 * n + 2]) for n in pairs]
    grow = [jnp.concatenate([gr_all[2 * n:2 * n + 1, :], gr_all[2 * n + 1:2 * n + 2, :]], axis=1) for n in pairs]
    beta = [stack(beta_all[:, 2 * n:2 * n + 1], beta_all[:, 2 * n + 1:2 * n + 2]) for n in pairs]
    d_incl = [jnp.exp(jnp.where(same & (i >= j), gcol[n] - grow[n], NEG_INF)) for n in pairs]
    kh = [k[n].astype(BF16) for n in pairs]
    m = [beta[n] * _dot_nt(kh[n], kh[n]) * jnp.where(i > j, d_incl[n], 0.0) for n in pairs]
    t = [eye - jnp.where(i // 2 == j // 2, m[n], 0.0) for n in pairs]
    s = 2
    while s < c:
        join = (i // (2 * s) == j // (2 * s)) & (i // s != j // s)
        tb = [t[n].astype(BF16) for n in pairs]
        x = [_dot(tb[n], jnp.where(join, m[n], 0.0).astype(BF16)) for n in pairs]
        t = [t[n] - _dot(x[n].astype(BF16), tb[n]) for n in pairs]
        s *= 2
    e_g = [jnp.exp(gcol[n]) for n in pairs]
    sol = [_dot(t[n].astype(BF16),
                jnp.concatenate([v[n] * beta[n], k[n] * (beta[n] * e_g[n])], axis=-1).astype(BF16)) for n in pairs]
    s_bf = [stack(s_old[n][0], s_old[n][1]).astype(BF16) for n in pairs]
    u = [sol[n][:, :DV_C] - _dot(per_head(sol[n][:, DV_C:]).astype(BF16), s_bf[n]) for n in pairs]
    ub = [u[n].astype(BF16) for n in pairs]
    qk = [_dot_nt(q[n].astype(BF16), kh[n]) * d_incl[n] for n in pairs]
    o = [_dot(jnp.concatenate([per_head(q[n] * e_g[n]), qk[n]], axis=1).astype(BF16), stack(s_bf[n], ub[n]))
         for n in pairs]
    g_last = [stack(jnp.broadcast_to(gcol[n][c - 1:c, :], (c, 1)), jnp.broadcast_to(gcol[n][c2 - 1:c2, :], (c, 1)))
              for n in pairs]
    ds = [_dot_tn((k[n] * jnp.exp(g_last[n] - gcol[n])).astype(BF16), per_head(u[n]).astype(BF16)) for n in pairs]
    s_new = [(s_old[n][0] * jnp.exp(gcol[n][c - 1:c, :]) + ds[n][:, :DV_C],
              s_old[n][1] * jnp.exp(gcol[n][c2 - 1:c2, :]) + ds[n][:, DV_C:]) for n in pairs]
    o = [o[n] * lax.rsqrt(jnp.mean(o[n] * o[n], axis=-1, keepdims=True) + RMS_EPS) * nw_ref[...] for n in pairs]
    o = [(o[n] * (z[n] * _sigmoid(z[n]))).astype(BF16) for n in pairs]
    for n in pairs:
        s_ref[2 * n] = s_new[n][0]
        s_ref[2 * n + 1] = s_new[n][1]
        o_ref[:, 2 * n * DV_C:(2 * n + 1) * DV_C] = o[n][:c]
        o_ref[:, (2 * n + 1) * DV_C:(2 * n + 2) * DV_C] = o[n][c:]


def _delta(qkv_act, z_all, gc, gr, beta, s0, norm_w, o_all, *, row0, z_col0, n_batch, t, c, hg):
    nc = t // c
    nhg = HV_C // hg
    rep = HV_C // HK_C
    qw = hg // rep * DK_C
    rb = row0 // c
    vmem = 24
    operands = [qkv_act, qkv_act, qkv_act, z_all, gc, gr, beta, s0, norm_w.reshape(1, DV_C)]
    in_specs = [pl.BlockSpec((c, qw), lambda b, g, ci: (b * nc + ci, g)),
                pl.BlockSpec((c, qw), lambda b, g, ci: (b * nc + ci, C_QK_DIM // qw + g)),
                pl.BlockSpec((c, hg * DV_C), lambda b, g, ci: (b * nc + ci, 2 * C_QK_DIM // (hg * DV_C) + g)),
                pl.BlockSpec((c, hg * DV_C), lambda b, g, ci: (rb + b * nc + ci, z_col0 // (hg * DV_C) + g)),
                pl.BlockSpec((1, 1, c, hg), lambda b, g, ci: (b * nc + ci, g, 0, 0)),
                pl.BlockSpec((1, 1, hg, c), lambda b, g, ci: (b * nc + ci, g, 0, 0)),
                pl.BlockSpec((1, 1, c, hg), lambda b, g, ci: (b * nc + ci, g, 0, 0)),
                pl.BlockSpec((hg, DK_C, DV_C), lambda b, g, ci: (b * nhg + g, 0, 0)),
                pl.BlockSpec((1, DV_C), lambda b, g, ci: (0, 0))]
    aliases = {}
    if o_all is not None:
        operands.append(o_all)
        in_specs.append(pl.BlockSpec(memory_space=pl.ANY))
        aliases = {len(operands) - 1: 0}
    return pl.pallas_call(
        functools.partial(_delta_kernel, c=c, hg=hg),
        grid=(n_batch, nhg, nc),
        in_specs=in_specs,
        out_specs=[pl.BlockSpec((c, hg * DV_C), lambda b, g, ci: (rb + b * nc + ci, g)),
                   pl.BlockSpec((hg, DK_C, DV_C), lambda b, g, ci: (b * nhg + g, 0, 0))],
        out_shape=[jax.ShapeDtypeStruct((z_all.shape[0], C_V_DIM), BF16),
                   jax.ShapeDtypeStruct((n_batch * HV_C, DK_C, DV_C), F32)],
        input_output_aliases=aliases,
        compiler_params=_params(("parallel", "parallel", "arbitrary"), vmem),
        name="delta",
    )(*operands)


def _chunk_layout(x, n_chunks, c, nhg, hg, head_major):
    if head_major:
        return x.reshape(nhg, hg, n_chunks, c).transpose(2, 0, 1, 3)
    return x.reshape(n_chunks, c, nhg, hg).transpose(0, 2, 1, 3)


TM = 512
TM_MLP = 1088
TF_MLP = 512
C_HG = 16
B_QBLK = 256
B_KBLK = 512
B_KBLK_S = 384


def _pad_cols(w, width):
    return jnp.pad(w, ((0, 0), (0, width - w.shape[1])))


def kernel(x_prompt, x_sample, cache_a_k, cache_a_v, cache_b_k, cache_b_v, cache_b_kidx, state_c_conv,
           state_c_ssm, a_w_in, a_rel_bias, a_w_out, b_w_in, b_w_out, c_w_in, c_conv_w, c_a_log, c_dt_bias,
           c_norm_w, c_w_out, ln1_g, ln1_b, mlp_w1, mlp_w2, ln2_g, ln2_b):
    nb_p, t_p, _ = x_prompt.shape
    nb_s, t_s, _ = x_sample.shape
    mp = nb_p * t_p
    ms = nb_s * t_s
    y = _join_rows(x_prompt.reshape(mp, D_MODEL), x_sample.reshape(ms, D_MODEL), TM)
    xin = y
    outs = {n: [] for n in ("akp", "avp", "aks", "avs", "bkp", "bvp", "bip", "bks", "bvs", "bis",
                            "ccp", "csp", "ccs", "css")}
    for i in range(DEPTH):
        kind, j = i % N_MIXERS, i // N_MIXERS
        if kind == 0:
            assert t_p % A_QBLK == 0 and t_p >= LEFT_CHUNKS * CHUNK == A_QBLK
            qkv = _mm(xin, a_w_in, j, TM, 1536, 3 * D_MODEL // 1536)
            past = cache_a_k.shape[2]
            win = -(-(past + t_s) // LANES) * LANES
            mix_in, k_tail, v_tail = _attn_a_prompt(qkv, _a_bias_rows(a_rel_bias[j], LEFT_CHUNKS * CHUNK),
                                                    _a_band_prompt(), _zeros(mp + ms, D_MODEL, TM), nb_p, t_p)
            mix_in, k_new, v_new = _attn_a_sample(qkv, cache_a_k.reshape(-1, D_MODEL), cache_a_v.reshape(-1, D_MODEL),
                                                  _a_bias_rows(a_rel_bias[j], past), _a_band_sample(past, t_s, win),
                                                  mix_in, j, past, mp, nb_s, t_s)
            outs["akp"].append(k_tail.reshape(nb_p, A_QBLK, H_A, DH_A))
            outs["avp"].append(v_tail.reshape(nb_p, A_QBLK, H_A, DH_A))
            outs["aks"].append(k_new.reshape(nb_s, t_s, H_A, DH_A))
            outs["avs"].append(v_new.reshape(nb_s, t_s, H_A, DH_A))
            w_out = a_w_out
        elif kind == 1:
            nq_, nk_, ni_ = H_B * DH_B, KV_B * DH_B, H_IDX * D_IDX
            w = b_w_in[j]
            w_ki = w[:, nq_ + 2 * nk_ + ni_:nq_ + 2 * nk_ + ni_ + D_IDX]
            w_wi = w[:, nq_ + 2 * nk_ + ni_ + D_IDX:]
            zc = lambda n: jnp.zeros((D_MODEL, n), F32)
            w_x = jnp.concatenate([w_ki, zc(LANES - D_IDX), zc(LANES - D_IDX), w_ki,
                                   w_wi, zc(2 * LANES - H_IDX)], axis=1)
            assert nq_ == 2 * ni_ and 2 * nk_ == ni_
            qq = _mm(xin, b_w_in, j, TM, ni_, 3, (BF16,), col_of=lambda t: jnp.where(t < 2, t, t + 1),
                     n_scaled=2, scale=DH_B ** -0.5 * LOG2E)
            k32, k16 = _mm(xin, b_w_in, j, TM, nk_, 1, (F32, BF16), col_of=lambda t: nq_ // nk_)
            v32, v16 = _mm(xin, b_w_in, j, TM, nk_, 1, (F32, BF16), col_of=lambda t: nq_ // nk_ + 1)
            x32, x16 = _mm(xin, w_x[None], 0, TM, B_XW, 1, (F32, BF16))
            k_p, k_s = _split_rows(k32, mp, TM)
            v_p, v_s = _split_rows(v32, mp, TM)
            outs["bkp"].append(k_p.reshape(nb_p, t_p, KV_B, DH_B))
            outs["bvp"].append(v_p.reshape(nb_p, t_p, KV_B, DH_B))
            outs["bip"].append(x32[:mp, :D_IDX].reshape(nb_p, t_p, D_IDX))
            outs["bks"].append(k_s.reshape(nb_s, t_s, KV_B, DH_B))
            outs["bvs"].append(v_s.reshape(nb_s, t_s, KV_B, DH_B))
            outs["bis"].append(x32[mp:, :D_IDX].reshape(nb_s, t_s, D_IDX))
            mix_in = _dsa(qq, x32, k16, v16, x16, x16, _zeros(mp + ms, D_MODEL, TM), ka_col=0, kz_col=1, row0=0,
                          n_batch=nb_p, lp=t_p, nq=B_QBLK, n_qblk=t_p // B_QBLK, kblk=B_KBLK, n_keys=t_p,
                          q_pos0=0, q_step=B_QBLK)
            past = cache_b_k.shape[2]
            n_keys = past + t_s
            lp = -(-n_keys // B_KBLK_S) * B_KBLK_S
            k_s, v_s, ka_s, kz_s = _sample_keys(
                cache_b_k[j].reshape(nb_s * past, nk_), cache_b_v[j].reshape(nb_s * past, nk_),
                cache_b_kidx[j].reshape(nb_s * past, D_IDX), k16, v16, x16,
                row0=mp, n_batch=nb_s, past=past, s_len=t_s, lp=lp)
            mix_in = _dsa(qq, x32, k_s, v_s, ka_s, kz_s, mix_in, ka_col=0, kz_col=0, row0=mp, n_batch=nb_s, lp=lp,
                          nq=t_s, n_qblk=1, kblk=B_KBLK_S, n_keys=n_keys, q_pos0=PAST_LEN, q_step=0)
            w_out = b_w_out
        else:
            n_main = C_CONV_DIM + C_V_DIM
            proj = _mm(xin, c_w_in, j, TM, 1536, n_main // 1536)
            gate_raw = _mm_x3(y, _pad_cols(c_w_in[j][:, n_main:], LANES), TM)
            b_raw, a_raw = gate_raw[:, :HV_C], gate_raw[:, HV_C:2 * HV_C]
            nhg = HV_C // C_HG
            mix_in = _zeros(mp + ms, C_V_DIM, TM)
            for (row0, nbt, tt, state, s0, names) in (
                    (0, nb_p, t_p, jnp.zeros((nb_p, CONV_W - 1, C_CONV_DIM), F32),
                     jnp.zeros((nb_p * HV_C, DK_C, DV_C), F32), ("ccp", "csp")),
                    (mp, nb_s, t_s, state_c_conv[j], state_c_ssm[j].reshape(nb_s * HV_C, DK_C, DV_C), ("ccs", "css"))):
                c = min(CHUNK, tt)
                n_rows = nbt * tt
                n_new = min(tt, CONV_W - 1)
                u_tail = jnp.stack([proj[row0 + (b + 1) * tt - n_new:row0 + (b + 1) * tt, :C_CONV_DIM]
                                    for b in range(nbt)])
                outs[names[0]].append(jnp.concatenate([state[:, n_new:], u_tail], axis=1))
                state8 = jnp.pad(state, ((0, 0), (SUBLANES - (CONV_W - 1), 0), (0, 0)))
                act = _conv(proj, state8, c_conv_w[j], row0=row0, n_batch=nbt, t=tt,
                            cw=CONV_PW if tt > 256 else C_CONV_DIM, rows=min(tt, 256))
                gc, gr, beta = _gates(a_raw[row0:row0 + n_rows], b_raw[row0:row0 + n_rows], c_a_log[j], c_dt_bias[j],
                                      c, 512)
                n_chunks = n_rows // c
                mix_in, s_new = _delta(act, proj, _chunk_layout(gc, n_chunks, c, nhg, C_HG, False),
                                       _chunk_layout(gr, n_chunks, c, nhg, C_HG, True),
                                       _chunk_layout(beta, n_chunks, c, nhg, C_HG, False), s0, c_norm_w[j], mix_in,
                                       row0=row0, z_col0=C_CONV_DIM, n_batch=nbt, t=tt, c=c, hg=C_HG)
                outs[names[1]].append(s_new.reshape(nbt, HV_C, DK_C, DV_C))
            w_out = c_w_out
        y = _mm_ln(mix_in, w_out.astype(BF16), j, y, ln1_g[i], ln1_b[i], TM, 1024)
        y, xin = _mlp_ln(y, mlp_w1, mlp_w2, i, ln2_g[i], ln2_b[i], TM_MLP, TF_MLP)
    st = lambda name: jnp.stack(outs[name])
    y_p, y_s = _split_rows(y, mp, TM)
    return (y_p.reshape(nb_p, t_p, D_MODEL), y_s.reshape(nb_s, t_s, D_MODEL),
            st("akp"), st("avp"), st("aks"), st("avs"),
            st("bkp"), st("bvp"), st("bip"), st("bks"), st("bvs"), st("bis"),
            st("ccp"), st("csp"), st("ccs"), st("css"))
```

```python
import functools

import numpy as np
import jax
import jax.numpy as jnp
from jax import lax
from jax.experimental import pallas as pl
from jax.experimental.pallas import tpu as pltpu

D_MODEL = 2048
DEPTH = 4
PAST_LEN = 1024
CHUNK = 64
N_MIXERS = 3
D_FF = 4 * D_MODEL
H_A = 16
DH_A = D_MODEL // H_A
LEFT_CHUNKS = 8
REL_CLIP = 128
H_B = 16
KV_B = 4
DH_B = D_MODEL // H_B
H_IDX = 16
D_IDX = 64
TOPK_MAX = 256
HK_C = 16
HV_C = 32
DK_C = 128
DV_C = 128
CONV_W = 4
C_QK_DIM = HK_C * DK_C
C_V_DIM = HV_C * DV_C
C_CONV_DIM = 2 * C_QK_DIM + C_V_DIM
ALPHA = (2.0 * DEPTH) ** 0.25
LN_EPS = 1e-5
RMS_EPS = 1e-6

LANES = 128
SUBLANES = 8
VMEM_CAP_MB = 56

F32 = jnp.float32
BF16 = jnp.bfloat16
NEG_INF = float("-inf")
LOG2E = 1.4426950408889634
INT_MIN = -(2 ** 31)


def _params(sem, vmem_mb):
    return pltpu.CompilerParams(dimension_semantics=sem,
                                vmem_limit_bytes=min(vmem_mb, VMEM_CAP_MB) * 1024 * 1024)


def _dot(a, b):
    return jnp.dot(a, b, preferred_element_type=F32)


def _dot_nt(a, b):
    return lax.dot_general(a, b, (((1,), (1,)), ((), ())), preferred_element_type=F32)


def _dot_tn(a, b):
    return lax.dot_general(a, b, (((0,), (0,)), ((), ())), preferred_element_type=F32)


def _split2(a):
    hi = a.astype(BF16)
    lo = (a - hi.astype(F32)).astype(BF16)
    return hi, lo


def _dot_x3(a, b):
    ah, al = _split2(a)
    bh, bl = _split2(b)
    return _dot(ah, bh) + (_dot(ah, bl) + _dot(al, bh))


def _layer_norm(z, g, b):
    mu = jnp.mean(z, axis=-1, keepdims=True)
    zc = z - mu
    var = jnp.mean(zc * zc, axis=-1, keepdims=True)
    return zc * lax.rsqrt(var + LN_EPS) * g + b


def _sigmoid(x):
    return 0.5 * jnp.tanh(0.5 * x) + 0.5


def _mm_kernel(x_ref, w_ref, *refs, n_scaled, scale):
    *o_refs, wb_ref = refs

    @pl.when(pl.program_id(1) == 0)
    def _():
        wb_ref[...] = w_ref[0].astype(BF16)

    res = _dot(x_ref[...].astype(BF16), wb_ref[...])
    if n_scaled:
        res = res * jnp.where(pl.program_id(0) < n_scaled, scale, 1.0)
    for o_ref in o_refs:
        o_ref[...] = res.astype(o_ref.dtype)


def _mm(x, w, layer, tm, tn, n_tiles, dtypes=(F32,), col_of=lambda j: j, n_scaled=0, scale=1.0):
    m, k = x.shape
    out_b = sum(jnp.dtype(d).itemsize for d in dtypes)
    vmem = (2 * tm * k * 4 + tm * k * 2 + 2 * k * tn * 4 + k * tn * 2 + 2 * tm * tn * out_b
            + tm * tn * 4) // 2 ** 20 + 4
    outs = pl.pallas_call(
        functools.partial(_mm_kernel, n_scaled=n_scaled, scale=scale),
        grid=(n_tiles, m // tm),
        in_specs=[pl.BlockSpec((tm, k), lambda j, i: (i, 0)),
                  pl.BlockSpec((1, k, tn), lambda j, i: (layer, 0, col_of(j)))],
        out_specs=[pl.BlockSpec((tm, tn), lambda j, i: (i, j)) for _ in dtypes],
        out_shape=[jax.ShapeDtypeStruct((m, n_tiles * tn), d) for d in dtypes],
        scratch_shapes=[pltpu.VMEM((k, tn), BF16)],
        compiler_params=_params(("parallel", "arbitrary"), vmem),
        name="mm",
    )(x, w)
    return outs[0] if len(dtypes) == 1 else outs


def _mm_x3_kernel(x_ref, w_ref, o_ref):
    o_ref[...] = _dot_x3(x_ref[...], w_ref[...])


def _mm_x3(x, w, tm):
    m, k = x.shape
    n = w.shape[1]
    return pl.pallas_call(
        _mm_x3_kernel,
        grid=(m // tm,),
        in_specs=[pl.BlockSpec((tm, k), lambda i: (i, 0)),
                  pl.BlockSpec((k, n), lambda i: (0, 0))],
        out_specs=pl.BlockSpec((tm, n), lambda i: (i, 0)),
        out_shape=jax.ShapeDtypeStruct((m, n), F32),
        compiler_params=_params(("parallel",), 24),
        name="mm_x3",
    )(x, w)


def _mm_ln_kernel(x_ref, w_ref, r_ref, g_ref, b_ref, o_ref, acc_ref, *, rows):
    k = pl.program_id(1)

    @pl.when(k == 0)
    def _():
        acc_ref[...] = jnp.zeros_like(acc_ref)

    acc_ref[...] += _dot(x_ref[...], w_ref[0])

    @pl.when(k == pl.num_programs(1) - 1)
    def _():
        def norm_rows(r, carry):
            sl = pl.ds(pl.multiple_of(r * rows, rows), rows)
            o_ref[sl, :] = _layer_norm(ALPHA * r_ref[sl, :] + acc_ref[sl, :], g_ref[...], b_ref[...])
            return carry

        lax.fori_loop(0, o_ref.shape[0] // rows, norm_rows, 0)


def _mm_ln(x, w, layer, resid, g, b, tm, tk):
    m, k = x.shape
    d = w.shape[2]
    vmem = (2 * tm * tk * 2 + 2 * tk * d * 2 + 6 * tm * d * 4) // 2 ** 20 + 4
    return pl.pallas_call(
        functools.partial(_mm_ln_kernel, rows=tm // 4),
        grid=(m // tm, k // tk),
        in_specs=[pl.BlockSpec((tm, tk), lambda i, kk: (i, kk)),
                  pl.BlockSpec((1, tk, d), lambda i, kk: (layer, kk, 0)),
                  pl.BlockSpec((tm, d), lambda i, kk: (i, 0)),
                  pl.BlockSpec((1, d), lambda i, kk: (0, 0)),
                  pl.BlockSpec((1, d), lambda i, kk: (0, 0))],
        out_specs=pl.BlockSpec((tm, d), lambda i, kk: (i, 0)),
        out_shape=jax.ShapeDtypeStruct((m, d), F32),
        scratch_shapes=[pltpu.VMEM((tm, d), F32)],
        compiler_params=_params(("parallel", "arbitrary"), vmem),
        name="mm_ln",
    )(x, w, resid, g.reshape(1, d), b.reshape(1, d))


def _join_kernel(a_ref, b_ref, o_ref, *, n_a):
    i = pl.program_id(0)

    @pl.when(i < n_a)
    def _():
        o_ref[...] = a_ref[...]

    @pl.when(i >= n_a)
    def _():
        o_ref[...] = b_ref[...]


def _join_rows(a, b, tm):
    (ma, d), mb = a.shape, b.shape[0]
    n_a = ma // tm
    return pl.pallas_call(
        functools.partial(_join_kernel, n_a=n_a),
        grid=(n_a + mb // tm,),
        in_specs=[pl.BlockSpec((tm, d), lambda i: (jnp.minimum(i, n_a - 1), 0)),
                  pl.BlockSpec((tm, d), lambda i: (jnp.maximum(i - n_a, 0), 0))],
        out_specs=pl.BlockSpec((tm, d), lambda i: (i, 0)),
        out_shape=jax.ShapeDtypeStruct((ma + mb, d), a.dtype),
        compiler_params=_params(("arbitrary",), 6 * tm * d * a.dtype.itemsize // 2 ** 20 + 4),
        name="join_rows",
    )(a, b)


def _split_kernel(x_ref, a_ref, b_ref, *, n_a):
    i = pl.program_id(0)

    @pl.when(i < n_a)
    def _():
        a_ref[...] = x_ref[...]

    @pl.when(i >= n_a)
    def _():
        b_ref[...] = x_ref[...]


def _split_rows(x, ma, tm):
    m, d = x.shape
    n_a = ma // tm
    return pl.pallas_call(
        functools.partial(_split_kernel, n_a=n_a),
        grid=(m // tm,),
        in_specs=[pl.BlockSpec((tm, d), lambda i: (i, 0))],
        out_specs=[pl.BlockSpec((tm, d), lambda i: (jnp.minimum(i, n_a - 1), 0)),
                   pl.BlockSpec((tm, d), lambda i: (jnp.maximum(i - n_a, 0), 0))],
        out_shape=[jax.ShapeDtypeStruct((ma, d), x.dtype), jax.ShapeDtypeStruct((m - ma, d), x.dtype)],
        compiler_params=_params(("arbitrary",), 6 * tm * d * x.dtype.itemsize // 2 ** 20 + 4),
        name="split_rows",
    )(x)


def _fill_kernel(o_ref):
    o_ref[...] = jnp.zeros(o_ref.shape, o_ref.dtype)


def _zeros(m, d, tm):
    return pl.pallas_call(
        _fill_kernel,
        grid=(m // tm,),
        out_specs=pl.BlockSpec((tm, d), lambda i: (i, 0)),
        out_shape=jax.ShapeDtypeStruct((m, d), BF16),
        compiler_params=_params(("parallel",), 2 * tm * d * 2 // 2 ** 20 + 4),
        name="fill",
    )()


def _mlp_ln_kernel(y_ref, w1_ref, w2_ref, g_ref, b_ref, o_ref, o16_ref, xb_ref, *, rows):
    f = pl.program_id(1)

    @pl.when(f == 0)
    def _():
        xb_ref[...] = y_ref[...].astype(BF16)
        o_ref[...] = jnp.zeros_like(o_ref)

    h = jnp.maximum(_dot(xb_ref[...], w1_ref[0].astype(BF16)), 0.0)
    o_ref[...] += _dot((h * h).astype(BF16), w2_ref[0].astype(BF16))

    @pl.when(f == pl.num_programs(1) - 1)
    def _():
        def norm_rows(r, carry):
            sl = pl.ds(pl.multiple_of(r * rows, rows), rows)
            out = _layer_norm(ALPHA * y_ref[sl, :] + o_ref[sl, :], g_ref[...], b_ref[...])
            o_ref[sl, :] = out
            o16_ref[sl, :] = out.astype(BF16)
            return carry

        lax.fori_loop(0, o_ref.shape[0] // rows, norm_rows, 0)


def _mlp_ln(y, w1, w2, layer, g, b, tm, tf):
    m, d = y.shape
    f = w1.shape[2]
    vmem = (3 * tm * d * 4 + 2 * tm * d * 2 + 4 * d * tf * 4 + 2 * d * tf * 2 + 3 * tm * tf * 4) // 2 ** 20 + 12
    once = pl.Buffered(1)
    return pl.pallas_call(
        functools.partial(_mlp_ln_kernel, rows=tm // 4),
        grid=(m // tm, f // tf),
        in_specs=[pl.BlockSpec((tm, d), lambda i, ff: (i, 0)),
                  pl.BlockSpec((1, d, tf), lambda i, ff: (layer, 0, ff)),
                  pl.BlockSpec((1, tf, d), lambda i, ff: (layer, ff, 0)),
                  pl.BlockSpec((1, d), lambda i, ff: (0, 0)),
                  pl.BlockSpec((1, d), lambda i, ff: (0, 0))],
        out_specs=[pl.BlockSpec((tm, d), lambda i, ff: (i, 0), pipeline_mode=once),
                   pl.BlockSpec((tm, d), lambda i, ff: (i, 0), pipeline_mode=once)],
        out_shape=[jax.ShapeDtypeStruct((m, d), F32), jax.ShapeDtypeStruct((m, d), BF16)],
        scratch_shapes=[pltpu.VMEM((tm, d), BF16)],
        compiler_params=_params(("parallel", "arbitrary"), vmem),
        name="mlp_ln",
    )(y, w1, w2, g.reshape(1, d), b.reshape(1, d))


A_QBLK = 512
A_PAIR = 2 * CHUNK
A_WIN = (LEFT_CHUNKS + 2) * CHUNK
A_UW = A_WIN + A_PAIR


def _softmax_pv(s2, v):
    m = jnp.max(s2, axis=-1, keepdims=True)
    p = jnp.exp2(s2 - m)
    l = jnp.sum(p, axis=-1, keepdims=True)
    return _dot(p.astype(BF16), v) / l


def _bias_table(u_row, band, rows):
    u = jnp.broadcast_to(u_row * LOG2E, (rows, A_UW))
    return pltpu.roll(u, A_UW - A_PAIR + 1, 1, stride=1, stride_axis=0)[:, :band.shape[1]] + band


def _attn_a_prompt_kernel(q_ref, kp_ref, kc_ref, vp_ref, vc_ref, u_ref, band_ref, o_hbm_ref, o_ref, ko_ref, vo_ref,
                          kw_ref, vw_ref):
    del o_hbm_ref
    i = pl.program_id(2)

    @pl.when(i == pl.num_programs(2) - 1)
    def _():
        ko_ref[...] = kc_ref[...]
        vo_ref[...] = vc_ref[...]

    kw_ref[0:A_QBLK, :] = kp_ref[...].astype(BF16)
    kw_ref[A_QBLK:2 * A_QBLK, :] = kc_ref[...].astype(BF16)
    vw_ref[0:A_QBLK, :] = vp_ref[...].astype(BF16)
    vw_ref[A_QBLK:2 * A_QBLK, :] = vc_ref[...].astype(BF16)
    bias = _bias_table(u_ref[0], band_ref[...], A_PAIR)
    col = lax.broadcasted_iota(jnp.int32, (A_PAIR, A_WIN), 1)
    pairs = range(A_QBLK // A_PAIR)
    q = [(q_ref[c * A_PAIR:(c + 1) * A_PAIR, :] * (DH_A ** -0.5 * LOG2E)).astype(BF16) for c in pairs]
    s = [_dot_nt(q[c], kw_ref[c * A_PAIR:c * A_PAIR + A_WIN, :]) + bias for c in pairs]
    s = [jnp.where(col + (i * A_QBLK - A_QBLK + c * A_PAIR) >= 0, s[c], NEG_INF) for c in pairs]
    o = [_softmax_pv(s[c], vw_ref[c * A_PAIR:c * A_PAIR + A_WIN, :]).astype(BF16) for c in pairs]
    for c in pairs:
        o_ref[c * A_PAIR:(c + 1) * A_PAIR, :] = o[c]


def _attn_a_prompt(qkv, u_rows, band, o_all, n_batch, t):
    nblk = t // A_QBLK

    def spec(col0, prev):
        if prev:
            return pl.BlockSpec((A_QBLK, DH_A), lambda b, h, i: (b * nblk + jnp.maximum(i - 1, 0), col0 + h))
        return pl.BlockSpec((A_QBLK, DH_A), lambda b, h, i: (b * nblk + i, col0 + h))

    tail = pl.BlockSpec((A_QBLK, DH_A), lambda b, h, i: (b, h))
    return pl.pallas_call(
        _attn_a_prompt_kernel,
        grid=(n_batch, H_A, nblk),
        in_specs=[spec(0, False), spec(H_A, True), spec(H_A, False), spec(2 * H_A, True), spec(2 * H_A, False),
                  pl.BlockSpec((1, 1, A_UW), lambda b, h, i: (h, 0, 0)),
                  pl.BlockSpec((A_PAIR, A_WIN), lambda b, h, i: (0, 0)),
                  pl.BlockSpec(memory_space=pl.ANY)],
        out_specs=[pl.BlockSpec((A_QBLK, DH_A), lambda b, h, i: (b * nblk + i, h)), tail, tail],
        out_shape=[jax.ShapeDtypeStruct(o_all.shape, BF16),
                   jax.ShapeDtypeStruct((n_batch * A_QBLK, D_MODEL), F32),
                   jax.ShapeDtypeStruct((n_batch * A_QBLK, D_MODEL), F32)],
        input_output_aliases={7: 0},
        scratch_shapes=[pltpu.VMEM((2 * A_QBLK, DH_A), BF16), pltpu.VMEM((2 * A_QBLK, DH_A), BF16)],
        compiler_params=_params(("parallel", "parallel", "arbitrary"), 16),
        name="attn_a_prompt",
    )(qkv, qkv, qkv, qkv, qkv, u_rows, band, o_all)


def _attn_a_sample_kernel(q_ref, kc_ref, kn_ref, vc_ref, vn_ref, u_ref, band_ref, o_hbm_ref, o_ref, ko_ref, vo_ref,
                          *, past, s_len):
    del o_hbm_ref
    ko_ref[...] = kn_ref[...]
    vo_ref[...] = vn_ref[...]
    band = band_ref[...]
    zero = jnp.zeros((band.shape[1] - past - s_len, DH_A), BF16)
    heads = range(H_A)
    cols = [slice(h * DH_A, (h + 1) * DH_A) for h in heads]

    def window(old_ref, new_ref, h):
        return jnp.concatenate([old_ref[:, cols[h]].astype(BF16), new_ref[:, cols[h]].astype(BF16), zero], axis=0)

    q = [(q_ref[:, cols[h]] * (DH_A ** -0.5 * LOG2E)).astype(BF16) for h in heads]
    s = [_dot_nt(q[h], window(kc_ref, kn_ref, h)) + _bias_table(u_ref[h], band, s_len) for h in heads]
    o = [_softmax_pv(s[h], window(vc_ref, vn_ref, h)).astype(BF16) for h in heads]
    for h in heads:
        o_ref[:, cols[h]] = o[h]


def _attn_a_sample(qkv, k_cache, v_cache, u_rows, band, o_all, layer, past, row0, n_batch, s_len):
    win = band.shape[1]
    rb = row0 // s_len
    new = lambda colblk: pl.BlockSpec((s_len, D_MODEL), lambda b: (rb + b, colblk))
    old = pl.BlockSpec((past, D_MODEL), lambda b: (layer * n_batch + b, 0))
    rows = pl.BlockSpec((s_len, D_MODEL), lambda b: (b, 0))
    return pl.pallas_call(
        functools.partial(_attn_a_sample_kernel, past=past, s_len=s_len),
        grid=(n_batch,),
        in_specs=[new(0), old, new(1), old, new(2),
                  pl.BlockSpec((H_A, 1, A_UW), lambda b: (0, 0, 0)),
                  pl.BlockSpec((s_len, win), lambda b: (0, 0)),
                  pl.BlockSpec(memory_space=pl.ANY)],
        out_specs=[pl.BlockSpec((s_len, D_MODEL), lambda b: (rb + b, 0)), rows, rows],
        out_shape=[jax.ShapeDtypeStruct(o_all.shape, BF16),
                   jax.ShapeDtypeStruct((n_batch * s_len, D_MODEL), F32),
                   jax.ShapeDtypeStruct((n_batch * s_len, D_MODEL), F32)],
        input_output_aliases={7: 0},
        compiler_params=_params(("parallel",), 32),
        name="attn_a_sample",
    )(qkv, k_cache, qkv, v_cache, qkv, u_rows, band, o_all)


def _a_bias_rows(rel_bias, q0):
    d = q0 + A_PAIR - 1 - np.arange(A_UW)
    return rel_bias[:, np.clip(d, -REL_CLIP, REL_CLIP) + REL_CLIP][:, None, :]


def _a_band_prompt():
    qc = LEFT_CHUNKS * CHUNK + np.arange(A_PAIR)[:, None]
    kc = np.arange(A_WIN)[None, :]
    vis = (kc // CHUNK <= qc // CHUNK) & (kc // CHUNK >= qc // CHUNK - LEFT_CHUNKS)
    return jnp.asarray(np.where(vis, 0.0, -np.inf).astype(np.float32))


def _a_band_sample(past, s_len, win):
    q_pos = PAST_LEN + np.arange(s_len)[:, None]
    k_pos = PAST_LEN - past + np.arange(win)[None, :]
    vis = ((k_pos >= 0) & (k_pos // CHUNK <= q_pos // CHUNK) & (k_pos // CHUNK >= q_pos // CHUNK - LEFT_CHUNKS)
           & (np.arange(win)[None, :] < past + s_len))
    return jnp.asarray(np.where(vis, 0.0, -np.inf).astype(np.float32))


B_XW = 4 * LANES


def _key_to_float(key):
    return lax.bitcast_convert_type(jnp.where(key >= 0, key, key ^ jnp.int32(0x7FFFFFFF)), F32)


KEY_NEG_INF = int(np.array(-np.inf, np.float32).view(np.int32)) ^ 0x7FFFFFFF


def _lane_fold(x):
    acc = x[:, 0:LANES]
    for t in range(1, x.shape[1] // LANES):
        acc = acc + x[:, t * LANES:(t + 1) * LANES]
    return acc


def _dsa_kernel(q_ref, qi_ref, wi_ref, k_ref, v_ref, ka_ref, kb_ref, *refs,
                nq, kblk, n_keys, q_pos0, q_step, n_sel, idx_bits):
    o_ref, sc_ref, msk_ref, qs_ref, m_ref, l_ref, acc_ref = refs[-7:]
    qbase = q_pos0 + pl.program_id(1) * q_step
    n_adm = jnp.minimum(((qbase + nq - 1) // CHUNK + 1) * CHUNK, n_keys)
    nb = (n_adm + kblk - 1) // kblk
    qpos = qbase + lax.broadcasted_iota(jnp.int32, (nq, kblk), 0)
    col0 = lax.broadcasted_iota(jnp.int32, (nq, kblk), 1)

    wsc = wi_ref[...] * (H_IDX ** -0.5 * D_IDX ** -0.5)

    def score_block(kb, carry):
        r = pl.multiple_of(kb * kblk, kblk)
        ka = ka_ref[pl.ds(r, kblk), :]
        kz = kb_ref[pl.ds(r, kblk), :]
        acc = jnp.zeros((nq, kblk), F32)
        for p in range(H_IDX // 2):
            q2 = qi_ref[:, p * LANES:(p + 1) * LANES]
            acc = acc + wsc[:, 2 * p:2 * p + 1] * jnp.maximum(_dot_nt(q2, ka), 0.0)
            acc = acc + wsc[:, 2 * p + 1:2 * p + 2] * jnp.maximum(_dot_nt(q2, kz), 0.0)
        col = col0 + kb * kblk
        adm = (col // CHUNK <= qpos // CHUNK) & (col < n_keys)
        sc_ref[kb] = jnp.where(adm, acc, NEG_INF)
        return carry

    lax.fori_loop(0, nb, score_block, 0)

    def count(pred_fn):
        def body(kb, c):
            return c + _lane_fold(jnp.where(pred_fn(kb, sc_ref[kb]), 1.0, 0.0))
        c = lax.fori_loop(0, nb, body, jnp.zeros((nq, LANES), F32))
        return jnp.sum(c, axis=-1, keepdims=True)

    def bit_step(bi, pre):
        cand = pre | jnp.left_shift(jnp.int32(1), 31 - bi)
        cand_s = cand ^ jnp.int32(INT_MIN)
        thr_c = _key_to_float(cand_s)
        cnt = count(lambda kb, sc: sc >= thr_c)
        return jnp.where((cnt >= n_sel) | (cand_s <= KEY_NEG_INF), cand, pre)

    pre = lax.fori_loop(0, 32, bit_step, jnp.zeros((nq, 1), jnp.int32))
    thr = _key_to_float(pre ^ jnp.int32(INT_MIN))
    need = n_sel - count(lambda kb, sc: sc > thr)
    n_ge = count(lambda kb, sc: sc >= thr)

    def idx_step(bi, lim):
        cand = lim | jnp.left_shift(jnp.int32(1), idx_bits - 1 - bi)
        cnt = count(lambda kb, sc: (sc == thr) & (col0 + kb * kblk < cand))
        return jnp.where(cnt < need, cand, lim)

    tied = jnp.max(jnp.where(thr > NEG_INF, n_ge, 0.0)) > n_sel
    lim = lax.cond(tied,
                   lambda: lax.fori_loop(0, idx_bits, idx_step, jnp.zeros((nq, 1), jnp.int32)),
                   lambda: jnp.full((nq, 1), 2 ** idx_bits, jnp.int32))

    def mask_block(kb, carry):
        sc = sc_ref[kb]
        sel = (sc > thr) | ((sc == thr) & (col0 + kb * kblk <= lim))
        sel = sel & (sc > NEG_INF) & (sc < float("inf"))
        msk_ref[kb] = jnp.where(sel, 0.0, NEG_INF)
        return carry

    lax.fori_loop(0, nb, mask_block, 0)

    group = H_B // KV_B
    gq = group * nq
    for h in range(H_B):
        qs_ref[h * nq:(h + 1) * nq, :] = q_ref[:, h * DH_B:(h + 1) * DH_B]
    m_ref[...] = jnp.full(m_ref.shape, NEG_INF, F32)
    l_ref[...] = jnp.zeros(l_ref.shape, F32)
    acc_ref[...] = jnp.zeros(acc_ref.shape, F32)

    def attend(kb, carry):
        r = pl.multiple_of(kb * kblk, kblk)
        msk = msk_ref[kb]
        s = [_dot_nt(qs_ref[g * gq:(g + 1) * gq, :], k_ref[pl.ds(r, kblk), g * DH_B:(g + 1) * DH_B])
             for g in range(KV_B)]
        for g in range(KV_B):
            ps = []
            for hq in range(group):
                rows = slice((g * group + hq) * nq, (g * group + hq + 1) * nq)
                sh = s[g][hq * nq:(hq + 1) * nq] + msk
                m_old = m_ref[rows]
                m_new = jnp.maximum(m_old, jnp.max(sh, axis=-1, keepdims=True))
                m_use = jnp.where(m_new == NEG_INF, 0.0, m_new)
                p = jnp.exp2(sh - m_use)
                a = jnp.exp2(m_old - m_use)
                m_ref[rows] = m_new
                l_ref[rows] = a * l_ref[rows] + jnp.sum(p, axis=-1, keepdims=True)
                acc_ref[rows] = a * acc_ref[rows]
                ps.append(p.astype(BF16))
            acc_ref[g * gq:(g + 1) * gq, :] += _dot(jnp.concatenate(ps, axis=0),
                                                    v_ref[pl.ds(r, kblk), g * DH_B:(g + 1) * DH_B])
        return carry

    lax.fori_loop(0, nb, attend, 0)
    for h in range(H_B):
        rows = slice(h * nq, (h + 1) * nq)
        o_ref[:, h * DH_B:(h + 1) * DH_B] = (acc_ref[rows] / l_ref[rows]).astype(BF16)


def _dsa(qq, x32, k, v, ka, kz, o_all, *, ka_col, kz_col, row0, n_batch, lp, nq, n_qblk, kblk, n_keys,
         q_pos0, q_step):
    nkb = lp // kblk
    n_sel = min(TOPK_MAX, n_keys // 4)
    rb = row0 // nq
    qspec = lambda width, colblk: pl.BlockSpec((nq, width), lambda b, i: (rb + b * n_qblk + i, colblk))
    kspec = lambda width, colblk: pl.BlockSpec((lp, width), lambda b, i: (b, colblk))
    kern = functools.partial(_dsa_kernel, nq=nq, kblk=kblk, n_keys=n_keys, q_pos0=q_pos0, q_step=q_step,
                             n_sel=n_sel, idx_bits=max(1, int(np.ceil(np.log2(lp)))))
    vmem = (2 * 2 * lp * (2 * KV_B * DH_B + 2 * LANES) + 2 * nkb * nq * kblk * 4
            + H_B * nq * (DH_B * 6 + 2 * LANES * 4) + 4 * nq * (qq.shape[1] + D_MODEL) * 2) // 2 ** 20 + 8
    operands = [qq, qq, x32, k, v, ka, kz]
    in_specs = [qspec(H_B * DH_B, 0), qspec(H_IDX * D_IDX, H_B * DH_B // (H_IDX * D_IDX)), qspec(LANES, 2),
                kspec(KV_B * DH_B, 0), kspec(KV_B * DH_B, 0), kspec(LANES, ka_col), kspec(LANES, kz_col)]
    aliases = {}
    if o_all is not None:
        operands.append(o_all)
        in_specs.append(pl.BlockSpec(memory_space=pl.ANY))
        aliases = {len(operands) - 1: 0}
    return pl.pallas_call(
        kern,
        grid=(n_batch, n_qblk),
        in_specs=in_specs,
        out_specs=pl.BlockSpec((nq, D_MODEL), lambda b, i: (rb + b * n_qblk + i, 0)),
        out_shape=jax.ShapeDtypeStruct((qq.shape[0], D_MODEL), BF16),
        input_output_aliases=aliases,
        scratch_shapes=[pltpu.VMEM((nkb, nq, kblk), F32),
                        pltpu.VMEM((nkb, nq, kblk), F32), pltpu.VMEM((H_B * nq, DH_B), BF16),
                        pltpu.VMEM((H_B * nq, 1), F32), pltpu.VMEM((H_B * nq, 1), F32),
                        pltpu.VMEM((H_B * nq, DH_B), F32)],
        compiler_params=_params(("parallel", "arbitrary"), vmem),
        name="dsa",
    )(*operands)


def _sample_keys_kernel(ck_ref, cv_ref, ci_ref, kn_ref, vn_ref, xn_ref, k_ref, v_ref, ka_ref, kz_ref,
                        *, past, s_len):
    end = past + s_len
    pad = k_ref.shape[0] - end
    k_ref[0:past, :] = ck_ref[...].astype(BF16)
    k_ref[past:end, :] = kn_ref[...]
    k_ref[end:, :] = jnp.zeros((pad, k_ref.shape[1]), BF16)
    v_ref[0:past, :] = cv_ref[...].astype(BF16)
    v_ref[past:end, :] = vn_ref[...]
    v_ref[end:, :] = jnp.zeros((pad, v_ref.shape[1]), BF16)
    ci = ci_ref[...]
    zero = jnp.zeros((past, LANES - D_IDX), F32)
    ka_ref[0:past, :] = jnp.concatenate([ci, zero], axis=1).astype(BF16)
    kz_ref[0:past, :] = jnp.concatenate([zero, ci], axis=1).astype(BF16)
    ka_ref[past:end, :] = xn_ref[:, 0:LANES]
    kz_ref[past:end, :] = xn_ref[:, LANES:2 * LANES]
    ka_ref[end:, :] = jnp.zeros((pad, LANES), BF16)
    kz_ref[end:, :] = jnp.zeros((pad, LANES), BF16)


def _sample_keys(cache_k, cache_v, cache_ki, k16, v16, x16, *, row0, n_batch, past, s_len, lp):
    rb = row0 // s_len
    nk = KV_B * DH_B
    old = lambda width: pl.BlockSpec((past, width), lambda b: (b, 0))
    new = lambda width: pl.BlockSpec((s_len, width), lambda b: (rb + b, 0))
    out = lambda width: pl.BlockSpec((lp, width), lambda b: (b, 0))
    return pl.pallas_call(
        functools.partial(_sample_keys_kernel, past=past, s_len=s_len),
        grid=(n_batch,),
        in_specs=[old(nk), old(nk), old(D_IDX), new(nk), new(nk), new(B_XW)],
        out_specs=[out(nk), out(nk), out(LANES), out(LANES)],
        out_shape=[jax.ShapeDtypeStruct((n_batch * lp, nk), BF16), jax.ShapeDtypeStruct((n_batch * lp, nk), BF16),
                   jax.ShapeDtypeStruct((n_batch * lp, LANES), BF16),
                   jax.ShapeDtypeStruct((n_batch * lp, LANES), BF16)],
        compiler_params=_params(("parallel",), 24),
        name="sample_keys",
    )(cache_k, cache_v, cache_ki, k16, v16, x16)


CONV_PW = 2 * DK_C


def _conv_kernel(u_ref, st_ref, w_ref, o_ref, ext_ref, *, t, cw, rows):
    j = pl.program_id(1)
    ext_ref[0:SUBLANES, :] = st_ref[0]
    ext_ref[SUBLANES:SUBLANES + t, :] = u_ref[...]
    off = SUBLANES - (CONV_W - 1)
    for p in range(cw // CONV_PW):
        cols = slice(p * CONV_PW, (p + 1) * CONV_PW)
        col0 = j * cw + p * CONV_PW
        scale = jnp.where(col0 < C_QK_DIM, DK_C ** -0.5, 1.0)
        w = w_ref[:, cols]

        def body(r, carry, cols=cols, col0=col0, scale=scale, w=w):
            base = pl.multiple_of(r * rows, rows)
            blk = ext_ref[pl.ds(base, rows + SUBLANES), cols]
            acc = blk[SUBLANES:, :] * w[CONV_W - 1:CONV_W, :]
            for jj in range(CONV_W - 1):
                tap = pltpu.roll(blk, SUBLANES - off - jj, 0)[SUBLANES:, :]
                acc = acc + tap * w[jj:jj + 1, :]
            a = acc * _sigmoid(acc)
            normed = []
            for hh in range(CONV_PW // DK_C):
                ah = a[:, hh * DK_C:(hh + 1) * DK_C]
                normed.append(ah * (lax.rsqrt(jnp.sum(ah * ah, axis=-1, keepdims=True) + RMS_EPS) * scale))
            o_ref[pl.ds(base, rows), cols] = jnp.where(col0 < 2 * C_QK_DIM, jnp.concatenate(normed, axis=-1), a)
            return carry

        lax.fori_loop(0, t // rows, body, 0)


def _conv(u_all, state8, conv_w, *, row0, n_batch, t, cw, rows):
    rb = row0 // t
    return pl.pallas_call(
        functools.partial(_conv_kernel, t=t, cw=cw, rows=rows),
        grid=(n_batch, C_CONV_DIM // cw),
        in_specs=[pl.BlockSpec((t, cw), lambda b, j: (rb + b, j)),
                  pl.BlockSpec((1, SUBLANES, cw), lambda b, j: (b, 0, j)),
                  pl.BlockSpec((CONV_W, cw), lambda b, j: (0, j))],
        out_specs=pl.BlockSpec((t, cw), lambda b, j: (b, j)),
        out_shape=jax.ShapeDtypeStruct((n_batch * t, C_CONV_DIM), F32),
        scratch_shapes=[pltpu.VMEM((t + SUBLANES, cw), F32)],
        compiler_params=_params(("parallel", "parallel"), 6 * t * cw * 4 // 2 ** 20 + 8),
        name="conv",
    )(u_all, state8, conv_w)


def _gates_kernel(a_ref, at_ref, b_ref, alog_ref, alogt_ref, dt_ref, dtt_ref, gc_ref, gr_ref, beta_ref, *, c, tb):
    def decay(a_raw, a_log, dt):
        x = a_raw + dt
        softplus = jnp.maximum(x, 0.0) + jnp.log1p(jnp.exp(-jnp.abs(x)))
        return -jnp.exp(a_log) * softplus

    def split3(x):
        p0 = x.astype(BF16)
        r = x - p0.astype(F32)
        p1 = r.astype(BF16)
        p2 = (r - p1.astype(F32)).astype(BF16)
        return p0, p1, p2

    i = lax.broadcasted_iota(jnp.int32, (tb, tb), 0)
    j = lax.broadcasted_iota(jnp.int32, (tb, tb), 1)
    same = (i // c) == (j // c)
    lower = jnp.where(same & (j <= i), 1.0, 0.0).astype(BF16)
    upper = jnp.where(same & (i <= j), 1.0, 0.0).astype(BF16)
    g = decay(a_ref[...], alog_ref[...], dt_ref[...])
    gt = decay(at_ref[...], alogt_ref[...], dtt_ref[...])
    g0, g1, g2 = split3(g)
    gc_ref[...] = _dot(lower, g0) + (_dot(lower, g1) + _dot(lower, g2))
    t0, t1, t2 = split3(gt)
    gr_ref[...] = _dot(t0, upper) + (_dot(t1, upper) + _dot(t2, upper))
    beta_ref[...] = _sigmoid(b_ref[...])


def _gates(a_raw, b_raw, a_log, dt_bias, c, tb):
    n = a_raw.shape[0]
    tok = pl.BlockSpec((tb, HV_C), lambda i: (i, 0))
    hed = pl.BlockSpec((HV_C, tb), lambda i: (0, i))
    row = pl.BlockSpec((1, HV_C), lambda i: (0, 0))
    colv = pl.BlockSpec((HV_C, 1), lambda i: (0, 0))
    return pl.pallas_call(
        functools.partial(_gates_kernel, c=c, tb=tb),
        grid=(n // tb,),
        in_specs=[tok, hed, tok, row, colv, row, colv],
        out_specs=[tok, hed, tok],
        out_shape=[jax.ShapeDtypeStruct((n, HV_C), F32), jax.ShapeDtypeStruct((HV_C, n), F32),
                   jax.ShapeDtypeStruct((n, HV_C), F32)],
        compiler_params=_params(("parallel",), 16),
        name="gates",
    )(a_raw, a_raw.T, b_raw, a_log.reshape(1, HV_C), a_log.reshape(HV_C, 1),
      dt_bias.reshape(1, HV_C), dt_bias.reshape(HV_C, 1))


def _delta_kernel(q_ref, k_ref, v_ref, z_ref, gc_ref, gr_ref, beta_ref, s0_ref, nw_ref, *refs, c, hg):
    o_ref, s_ref = refs[-2:]

    @pl.when(pl.program_id(2) == 0)
    def _():
        s_ref[...] = s0_ref[...]

    assert HV_C // HK_C == 2
    c2 = 2 * c
    i = lax.broadcasted_iota(jnp.int32, (c2, c2), 0)
    j = lax.broadcasted_iota(jnp.int32, (c2, c2), 1)
    eye = jnp.where(i == j, 1.0, 0.0)
    same = (i // c) == (j // c)
    top = lax.broadcasted_iota(jnp.int32, (c2, 1), 0) < c
    pairs = range(hg // 2)
    stack = lambda a, b: jnp.concatenate([a, b], axis=0)

    def per_head(x):
        return jnp.concatenate([jnp.where(top, x, 0.0), jnp.where(top, 0.0, x)], axis=1)

    gc_all, gr_all, beta_all = gc_ref[0, 0], gr_ref[0, 0], beta_ref[0, 0]
    q1 = [q_ref[:, n * DK_C:(n + 1) * DK_C] for n in pairs]
    k1 = [k_ref[:, n * DK_C:(n + 1) * DK_C] for n in pairs]
    q = [stack(q1[n], q1[n]) for n in pairs]
    k = [stack(k1[n], k1[n]) for n in pairs]
    v = [stack(v_ref[:, 2 * n * DV_C:(2 * n + 1) * DV_C], v_ref[:, (2 * n + 1) * DV_C:(2 * n + 2) * DV_C]) for n in pairs]
    z = [stack(z_ref[:, 2 * n * DV_C:(2 * n + 1) * DV_C], z_ref[:, (2 * n + 1) * DV_C:(2 * n + 2) * DV_C]) for n in pairs]
    s_old = [(s_ref[2 * n], s_ref[2 * n + 1]) for n in pairs]
    gcol = [stack(gc_all[:, 2 * n:2 * n + 1], gc_all[:, 2 * n + 1:2 * n + 2]) for n in pairs]
    grow = [jnp.concatenate([gr_all[2 * n:2 * n + 1, :], gr_all[2 * n + 1:2 * n + 2, :]], axis=1) for n in pairs]
    beta = [stack(beta_all[:, 2 * n:2 * n + 1], beta_all[:, 2 * n + 1:2 * n + 2]) for n in pairs]
    d_incl = [jnp.exp(jnp.where(same & (i >= j), gcol[n] - grow[n], NEG_INF)) for n in pairs]
    kh = [k[n].astype(BF16) for n in pairs]
    m = [beta[n] * _dot_nt(kh[n], kh[n]) * jnp.where(i > j, d_incl[n], 0.0) for n in pairs]
    t = [eye - jnp.where(i // 2 == j // 2, m[n], 0.0) for n in pairs]
    s = 2
    while s < c:
        join = (i // (2 * s) == j // (2 * s)) & (i // s != j // s)
        tb = [t[n].astype(BF16) for n in pairs]
        x = [_dot(tb[n], jnp.where(join, m[n], 0.0).astype(BF16)) for n in pairs]
        t = [t[n] - _dot(x[n].astype(BF16), tb[n]) for n in pairs]
        s *= 2
    e_g = [jnp.exp(gcol[n]) for n in pairs]
    sol = [_dot(t[n].astype(BF16),
                jnp.concatenate([v[n] * beta[n], k[n] * (beta[n] * e_g[n])], axis=-1).astype(BF16)) for n in pairs]
    s_bf = [stack(s_old[n][0], s_old[n][1]).astype(BF16) for n in pairs]
    u = [sol[n][:, :DV_C] - _dot(per_head(sol[n][:, DV_C:]).astype(BF16), s_bf[n]) for n in pairs]
    ub = [u[n].astype(BF16) for n in pairs]
    qk = [_dot_nt(q[n].astype(BF16), kh[n]) * d_incl[n] for n in pairs]
    o = [_dot(jnp.concatenate([per_head(q[n] * e_g[n]), qk[n]], axis=1).astype(BF16), stack(s_bf[n], ub[n]))
         for n in pairs]
    g_last = [stack(jnp.broadcast_to(gcol[n][c - 1:c, :], (c, 1)), jnp.broadcast_to(gcol[n][c2 - 1:c2, :], (c, 1)))
              for n in pairs]
    ds = [_dot_tn((k[n] * jnp.exp(g_last[n] - gcol[n])).astype(BF16), per_head(u[n]).astype(BF16)) for n in pairs]
    s_new = [(s_old[n][0] * jnp.exp(gcol[n][c - 1:c, :]) + ds[n][:, :DV_C],
              s_old[n][1] * jnp.exp(gcol[n][c2 - 1:c2, :]) + ds[n][:, DV_C:]) for n in pairs]
    o = [o[n] * lax.rsqrt(jnp.mean(o[n] * o[n], axis=-1, keepdims=True) + RMS_EPS) * nw_ref[...] for n in pairs]
    o = [(o[n] * (z[n] * _sigmoid(z[n]))).astype(BF16) for n in pairs]
    for n in pairs:
        s_ref[2 * n] = s_new[n][0]
        s_ref[2 * n + 1] = s_new[n][1]
        o_ref[:, 2 * n * DV_C:(2 * n + 1) * DV_C] = o[n][:c]
        o_ref[:, (2 * n + 1) * DV_C:(2 * n + 2) * DV_C] = o[n][c:]


def _delta(qkv_act, z_all, gc, gr, beta, s0, norm_w, o_all, *, row0, z_col0, n_batch, t, c, hg):
    nc = t // c
    nhg = HV_C // hg
    rep = HV_C // HK_C
    qw = hg // rep * DK_C
    rb = row0 // c
    vmem = 24
    operands = [qkv_act, qkv_act, qkv_act, z_all, gc, gr, beta, s0, norm_w.reshape(1, DV_C)]
    in_specs = [pl.BlockSpec((c, qw), lambda b, g, ci: (b * nc + ci, g)),
                pl.BlockSpec((c, qw), lambda b, g, ci: (b * nc + ci, C_QK_DIM // qw + g)),
                pl.BlockSpec((c, hg * DV_C), lambda b, g, ci: (b * nc + ci, 2 * C_QK_DIM // (hg * DV_C) + g)),
                pl.BlockSpec((c, hg * DV_C), lambda b, g, ci: (rb + b * nc + ci, z_col0 // (hg * DV_C) + g)),
                pl.BlockSpec((1, 1, c, hg), lambda b, g, ci: (b * nc + ci, g, 0, 0)),
                pl.BlockSpec((1, 1, hg, c), lambda b, g, ci: (b * nc + ci, g, 0, 0)),
                pl.BlockSpec((1, 1, c, hg), lambda b, g, ci: (b * nc + ci, g, 0, 0)),
                pl.BlockSpec((hg, DK_C, DV_C), lambda b, g, ci: (b * nhg + g, 0, 0)),
                pl.BlockSpec((1, DV_C), lambda b, g, ci: (0, 0))]
    aliases = {}
    if o_all is not None:
        operands.append(o_all)
        in_specs.append(pl.BlockSpec(memory_space=pl.ANY))
        aliases = {len(operands) - 1: 0}
    return pl.pallas_call(
        functools.partial(_delta_kernel, c=c, hg=hg),
        grid=(n_batch, nhg, nc),
        in_specs=in_specs,
        out_specs=[pl.BlockSpec((c, hg * DV_C), lambda b, g, ci: (rb + b * nc + ci, g)),
                   pl.BlockSpec((hg, DK_C, DV_C), lambda b, g, ci: (b * nhg + g, 0, 0))],
        out_shape=[jax.ShapeDtypeStruct((z_all.shape[0], C_V_DIM), BF16),
                   jax.ShapeDtypeStruct((n_batch * HV_C, DK_C, DV_C), F32)],
        input_output_aliases=aliases,
        compiler_params=_params(("parallel", "parallel", "arbitrary"), vmem),
        name="delta",
    )(*operands)


def _chunk_layout(x, n_chunks, c, nhg, hg, head_major):
    if head_major:
        return x.reshape(nhg, hg, n_chunks, c).transpose(2, 0, 1, 3)
    return x.reshape(n_chunks, c, nhg, hg).transpose(0, 2, 1, 3)


TM = 512
TM_MLP = 1088
TF_MLP = 512
C_HG = 16
B_QBLK = 256
B_KBLK = 512
B_KBLK_S = 384


def _pad_cols(w, width):
    return jnp.pad(w, ((0, 0), (0, width - w.shape[1])))


def kernel(x_prompt, x_sample, cache_a_k, cache_a_v, cache_b_k, cache_b_v, cache_b_kidx, state_c_conv,
           state_c_ssm, a_w_in, a_rel_bias, a_w_out, b_w_in, b_w_out, c_w_in, c_conv_w, c_a_log, c_dt_bias,
           c_norm_w, c_w_out, ln1_g, ln1_b, mlp_w1, mlp_w2, ln2_g, ln2_b):
    nb_p, t_p, _ = x_prompt.shape
    nb_s, t_s, _ = x_sample.shape
    mp = nb_p * t_p
    ms = nb_s * t_s
    y = _join_rows(x_prompt.reshape(mp, D_MODEL), x_sample.reshape(ms, D_MODEL), TM)
    xin = y
    outs = {n: [] for n in ("akp", "avp", "aks", "avs", "bkp", "bvp", "bip", "bks", "bvs", "bis",
                            "ccp", "csp", "ccs", "css")}
    for i in range(DEPTH):
        kind, j = i % N_MIXERS, i // N_MIXERS
        if kind == 0:
            assert t_p % A_QBLK == 0 and t_p >= LEFT_CHUNKS * CHUNK == A_QBLK
            qkv = _mm(xin, a_w_in, j, TM, 1536, 3 * D_MODEL // 1536)
            past = cache_a_k.shape[2]
            win = -(-(past + t_s) // LANES) * LANES
            mix_in, k_tail, v_tail = _attn_a_prompt(qkv, _a_bias_rows(a_rel_bias[j], LEFT_CHUNKS * CHUNK),
                                                    _a_band_prompt(), _zeros(mp + ms, D_MODEL, TM), nb_p, t_p)
            mix_in, k_new, v_new = _attn_a_sample(qkv, cache_a_k.reshape(-1, D_MODEL), cache_a_v.reshape(-1, D_MODEL),
                                                  _a_bias_rows(a_rel_bias[j], past), _a_band_sample(past, t_s, win),
                                                  mix_in, j, past, mp, nb_s, t_s)
            outs["akp"].append(k_tail.reshape(nb_p, A_QBLK, H_A, DH_A))
            outs["avp"].append(v_tail.reshape(nb_p, A_QBLK, H_A, DH_A))
            outs["aks"].append(k_new.reshape(nb_s, t_s, H_A, DH_A))
            outs["avs"].append(v_new.reshape(nb_s, t_s, H_A, DH_A))
            w_out = a_w_out
        elif kind == 1:
            nq_, nk_, ni_ = H_B * DH_B, KV_B * DH_B, H_IDX * D_IDX
            w = b_w_in[j]
            w_ki = w[:, nq_ + 2 * nk_ + ni_:nq_ + 2 * nk_ + ni_ + D_IDX]
            w_wi = w[:, nq_ + 2 * nk_ + ni_ + D_IDX:]
            zc = lambda n: jnp.zeros((D_MODEL, n), F32)
            w_x = jnp.concatenate([w_ki, zc(LANES - D_IDX), zc(LANES - D_IDX), w_ki,
                                   w_wi, zc(2 * LANES - H_IDX)], axis=1)
            assert nq_ == 2 * ni_ and 2 * nk_ == ni_
            qq = _mm(xin, b_w_in, j, TM, ni_, 3, (BF16,), col_of=lambda t: jnp.where(t < 2, t, t + 1),
                     n_scaled=2, scale=DH_B ** -0.5 * LOG2E)
            k32, k16 = _mm(xin, b_w_in, j, TM, nk_, 1, (F32, BF16), col_of=lambda t: nq_ // nk_)
            v32, v16 = _mm(xin, b_w_in, j, TM, nk_, 1, (F32, BF16), col_of=lambda t: nq_ // nk_ + 1)
            x32, x16 = _mm(xin, w_x[None], 0, TM, B_XW, 1, (F32, BF16))
            k_p, k_s = _split_rows(k32, mp, TM)
            v_p, v_s = _split_rows(v32, mp, TM)
            outs["bkp"].append(k_p.reshape(nb_p, t_p, KV_B, DH_B))
            outs["bvp"].append(v_p.reshape(nb_p, t_p, KV_B, DH_B))
            outs["bip"].append(x32[:mp, :D_IDX].reshape(nb_p, t_p, D_IDX))
            outs["bks"].append(k_s.reshape(nb_s, t_s, KV_B, DH_B))
            outs["bvs"].append(v_s.reshape(nb_s, t_s, KV_B, DH_B))
            outs["bis"].append(x32[mp:, :D_IDX].reshape(nb_s, t_s, D_IDX))
            mix_in = _dsa(qq, x32, k16, v16, x16, x16, _zeros(mp + ms, D_MODEL, TM), ka_col=0, kz_col=1, row0=0,
                          n_batch=nb_p, lp=t_p, nq=B_QBLK, n_qblk=t_p // B_QBLK, kblk=B_KBLK, n_keys=t_p,
                          q_pos0=0, q_step=B_QBLK)
            past = cache_b_k.shape[2]
            n_keys = past + t_s
            lp = -(-n_keys // B_KBLK_S) * B_KBLK_S
            k_s, v_s, ka_s, kz_s = _sample_keys(
                cache_b_k[j].reshape(nb_s * past, nk_), cache_b_v[j].reshape(nb_s * past, nk_),
                cache_b_kidx[j].reshape(nb_s * past, D_IDX), k16, v16, x16,
                row0=mp, n_batch=nb_s, past=past, s_len=t_s, lp=lp)
            mix_in = _dsa(qq, x32, k_s, v_s, ka_s, kz_s, mix_in, ka_col=0, kz_col=0, row0=mp, n_batch=nb_s, lp=lp,
                          nq=t_s, n_qblk=1, kblk=B_KBLK_S, n_keys=n_keys, q_pos0=PAST_LEN, q_step=0)
            w_out = b_w_out
        else:
            n_main = C_CONV_DIM + C_V_DIM
            proj = _mm(xin, c_w_in, j, TM, 1536, n_main // 1536)
            gate_raw = _mm_x3(y, _pad_cols(c_w_in[j][:, n_main:], LANES), TM)
            b_raw, a_raw = gate_raw[:, :HV_C], gate_raw[:, HV_C:2 * HV_C]
            nhg = HV_C // C_HG
            mix_in = _zeros(mp + ms, C_V_DIM, TM)
            for (row0, nbt, tt, state, s0, names) in (
                    (0, nb_p, t_p, jnp.zeros((nb_p, CONV_W - 1, C_CONV_DIM), F32),
                     jnp.zeros((nb_p * HV_C, DK_C, DV_C), F32), ("ccp", "csp")),
                    (mp, nb_s, t_s, state_c_conv[j], state_c_ssm[j].reshape(nb_s * HV_C, DK_C, DV_C), ("ccs", "css"))):
                c = min(CHUNK, tt)
                n_rows = nbt * tt
                n_new = min(tt, CONV_W - 1)
                u_tail = jnp.stack([proj[row0 + (b + 1) * tt - n_new:row0 + (b + 1) * tt, :C_CONV_DIM]
                                    for b in range(nbt)])
                outs[names[0]].append(jnp.concatenate([state[:, n_new:], u_tail], axis=1))
                state8 = jnp.pad(state, ((0, 0), (SUBLANES - (CONV_W - 1), 0), (0, 0)))
                act = _conv(proj, state8, c_conv_w[j], row0=row0, n_batch=nbt, t=tt,
                            cw=CONV_PW if tt > 256 else C_CONV_DIM, rows=min(tt, 256))
                gc, gr, beta = _gates(a_raw[row0:row0 + n_rows], b_raw[row0:row0 + n_rows], c_a_log[j], c_dt_bias[j],
                                      c, 512)
                n_chunks = n_rows // c
                mix_in, s_new = _delta(act, proj, _chunk_layout(gc, n_chunks, c, nhg, C_HG, False),
                                       _chunk_layout(gr, n_chunks, c, nhg, C_HG, True),
                                       _chunk_layout(beta, n_chunks, c, nhg, C_HG, False), s0, c_norm_w[j], mix_in,
                                       row0=row0, z_col0=C_CONV_DIM, n_batch=nbt, t=tt, c=c, hg=C_HG)
                outs[names[1]].append(s_new.reshape(nbt, HV_C, DK_C, DV_C))
            w_out = c_w_out
        y = _mm_ln(mix_in, w_out.astype(BF16), j, y, ln1_g[i], ln1_b[i], TM, 1024)
        y, xin = _mlp_ln(y, mlp_w1, mlp_w2, i, ln2_g[i], ln2_b[i], TM_MLP, TF_MLP)
    st = lambda name: jnp.stack(outs[name])
    y_p, y_s = _split_rows(y, mp, TM)
    return (y_p.reshape(nb_p, t_p, D_MODEL), y_s.reshape(nb_s, t_s, D_MODEL),
            st("akp"), st("avp"), st("aks"), st("avs"),
            st("bkp"), st("bvp"), st("bip"), st("bks"), st("bvs"), st("bis"),
            st("ccp"), st("csp"), st("ccs"), st("css"))
```

```python
import functools

import numpy as np
import jax
import jax.numpy as jnp
from jax import lax
from jax.experimental import pallas as pl
from jax.experimental.pallas import tpu as pltpu

D_MODEL = 2048
DEPTH = 4
PAST_LEN = 1024
CHUNK = 64
N_MIXERS = 3
D_FF = 4 * D_MODEL
H_A = 16
DH_A = D_MODEL // H_A
LEFT_CHUNKS = 8
REL_CLIP = 128
H_B = 16
KV_B = 4
DH_B = D_MODEL // H_B
H_IDX = 16
D_IDX = 64
TOPK_MAX = 256
HK_C = 16
HV_C = 32
DK_C = 128
DV_C = 128
CONV_W = 4
C_QK_DIM = HK_C * DK_C
C_V_DIM = HV_C * DV_C
C_CONV_DIM = 2 * C_QK_DIM + C_V_DIM
ALPHA = (2.0 * DEPTH) ** 0.25
LN_EPS = 1e-5
RMS_EPS = 1e-6

LANES = 128
SUBLANES = 8
VMEM_CAP_MB = 56

F32 = jnp.float32
BF16 = jnp.bfloat16
NEG_INF = float("-inf")
LOG2E = 1.4426950408889634
INT_MIN = -(2 ** 31)


def _params(sem, vmem_mb):
    return pltpu.CompilerParams(dimension_semantics=sem,
                                vmem_limit_bytes=min(vmem_mb, VMEM_CAP_MB) * 1024 * 1024)


def _dot(a, b):
    return jnp.dot(a, b, preferred_element_type=F32)


def _dot_nt(a, b):
    return lax.dot_general(a, b, (((1,), (1,)), ((), ())), preferred_element_type=F32)


def _dot_tn(a, b):
    return lax.dot_general(a, b, (((0,), (0,)), ((), ())), preferred_element_type=F32)


def _split2(a):
    hi = a.astype(BF16)
    lo = (a - hi.astype(F32)).astype(BF16)
    return hi, lo


def _dot_x3(a, b):
    ah, al = _split2(a)
    bh, bl = _split2(b)
    return _dot(ah, bh) + (_dot(ah, bl) + _dot(al, bh))


def _layer_norm(z, g, b):
    mu = jnp.mean(z, axis=-1, keepdims=True)
    zc = z - mu
    var = jnp.mean(zc * zc, axis=-1, keepdims=True)
    return zc * lax.rsqrt(var + LN_EPS) * g + b


def _sigmoid(x):
    return 0.5 * jnp.tanh(0.5 * x) + 0.5


def _mm_kernel(x_ref, w_ref, *refs, n_scaled, scale):
    *o_refs, wb_ref = refs

    @pl.when(pl.program_id(1) == 0)
    def _():
        wb_ref[...] = w_ref[0].astype(BF16)

    res = _dot(x_ref[...].astype(BF16), wb_ref[...])
    if n_scaled:
        res = res * jnp.where(pl.program_id(0) < n_scaled, scale, 1.0)
    for o_ref in o_refs:
        o_ref[...] = res.astype(o_ref.dtype)


def _mm(x, w, layer, tm, tn, n_tiles, dtypes=(F32,), col_of=lambda j: j, n_scaled=0, scale=1.0):
    m, k = x.shape
    out_b = sum(jnp.dtype(d).itemsize for d in dtypes)
    vmem = (2 * tm * k * 4 + tm * k * 2 + 2 * k * tn * 4 + k * tn * 2 + 2 * tm * tn * out_b
            + tm * tn * 4) // 2 ** 20 + 4
    outs = pl.pallas_call(
        functools.partial(_mm_kernel, n_scaled=n_scaled, scale=scale),
        grid=(n_tiles, m // tm),
        in_specs=[pl.BlockSpec((tm, k), lambda j, i: (i, 0)),
                  pl.BlockSpec((1, k, tn), lambda j, i: (layer, 0, col_of(j)))],
        out_specs=[pl.BlockSpec((tm, tn), lambda j, i: (i, j)) for _ in dtypes],
        out_shape=[jax.ShapeDtypeStruct((m, n_tiles * tn), d) for d in dtypes],
        scratch_shapes=[pltpu.VMEM((k, tn), BF16)],
        compiler_params=_params(("parallel", "arbitrary"), vmem),
        name="mm",
    )(x, w)
    return outs[0] if len(dtypes) == 1 else outs


def _mm_x3_kernel(x_ref, w_ref, o_ref):
    o_ref[...] = _dot_x3(x_ref[...], w_ref[...])


def _mm_x3(x, w, tm):
    m, k = x.shape
    n = w.shape[1]
    return pl.pallas_call(
        _mm_x3_kernel,
        grid=(m // tm,),
        in_specs=[pl.BlockSpec((tm, k), lambda i: (i, 0)),
                  pl.BlockSpec((k, n), lambda i: (0, 0))],
        out_specs=pl.BlockSpec((tm, n), lambda i: (i, 0)),
        out_shape=jax.ShapeDtypeStruct((m, n), F32),
        compiler_params=_params(("parallel",), 24),
        name="mm_x3",
    )(x, w)


def _mm_ln_kernel(x_ref, w_ref, r_ref, g_ref, b_ref, o_ref, acc_ref, *, rows):
    k = pl.program_id(1)

    @pl.when(k == 0)
    def _():
        acc_ref[...] = jnp.zeros_like(acc_ref)

    acc_ref[...] += _dot(x_ref[...], w_ref[0])

    @pl.when(k == pl.num_programs(1) - 1)
    def _():
        def norm_rows(r, carry):
            sl = pl.ds(pl.multiple_of(r * rows, rows), rows)
            o_ref[sl, :] = _layer_norm(ALPHA * r_ref[sl, :] + acc_ref[sl, :], g_ref[...], b_ref[...])
            return carry

        lax.fori_loop(0, o_ref.shape[0] // rows, norm_rows, 0)


def _mm_ln(x, w, layer, resid, g, b, tm, tk):
    m, k = x.shape
    d = w.shape[2]
    vmem = (2 * tm * tk * 2 + 2 * tk * d * 2 + 6 * tm * d * 4) // 2 ** 20 + 4
    return pl.pallas_call(
        functools.partial(_mm_ln_kernel, rows=tm // 4),
        grid=(m // tm, k // tk),
        in_specs=[pl.BlockSpec((tm, tk), lambda i, kk: (i, kk)),
                  pl.BlockSpec((1, tk, d), lambda i, kk: (layer, kk, 0)),
                  pl.BlockSpec((tm, d), lambda i, kk: (i, 0)),
                  pl.BlockSpec((1, d), lambda i, kk: (0, 0)),
                  pl.BlockSpec((1, d), lambda i, kk: (0, 0))],
        out_specs=pl.BlockSpec((tm, d), lambda i, kk: (i, 0)),
        out_shape=jax.ShapeDtypeStruct((m, d), F32),
        scratch_shapes=[pltpu.VMEM((tm, d), F32)],
        compiler_params=_params(("parallel", "arbitrary"), vmem),
        name="mm_ln",
    )(x, w, resid, g.reshape(1, d), b.reshape(1, d))


def _join_kernel(a_ref, b_ref, o_ref, *, n_a):
    i = pl.program_id(0)

    @pl.when(i < n_a)
    def _():
        o_ref[...] = a_ref[...]

    @pl.when(i >= n_a)
    def _():
        o_ref[...] = b_ref[...]


def _join_rows(a, b, tm):
    (ma, d), mb = a.shape, b.shape[0]
    n_a = ma // tm
    return pl.pallas_call(
        functools.partial(_join_kernel, n_a=n_a),
        grid=(n_a + mb // tm,),
        in_specs=[pl.BlockSpec((tm, d), lambda i: (jnp.minimum(i, n_a - 1), 0)),
                  pl.BlockSpec((tm, d), lambda i: (jnp.maximum(i - n_a, 0), 0))],
        out_specs=pl.BlockSpec((tm, d), lambda i: (i, 0)),
        out_shape=jax.ShapeDtypeStruct((ma + mb, d), a.dtype),
        compiler_params=_params(("arbitrary",), 6 * tm * d * a.dtype.itemsize // 2 ** 20 + 4),
        name="join_rows",
    )(a, b)


def _split_kernel(x_ref, a_ref, b_ref, *, n_a):
    i = pl.program_id(0)

    @pl.when(i < n_a)
    def _():
        a_ref[...] = x_ref[...]

    @pl.when(i >= n_a)
    def _():
        b_ref[...] = x_ref[...]


def _split_rows(x, ma, tm):
    m, d = x.shape
    n_a = ma // tm
    return pl.pallas_call(
        functools.partial(_split_kernel, n_a=n_a),
        grid=(m // tm,),
        in_specs=[pl.BlockSpec((tm, d), lambda i: (i, 0))],
        out_specs=[pl.BlockSpec((tm, d), lambda i: (jnp.minimum(i, n_a - 1), 0)),
                   pl.BlockSpec((tm, d), lambda i: (jnp.maximum(i - n_a, 0), 0))],
        out_shape=[jax.ShapeDtypeStruct((ma, d), x.dtype), jax.ShapeDtypeStruct((m - ma, d), x.dtype)],
        compiler_params=_params(("arbitrary",), 6 * tm * d * x.dtype.itemsize // 2 ** 20 + 4),
        name="split_rows",
    )(x)


def _fill_kernel(o_ref):
    o_ref[...] = jnp.zeros(o_ref.shape, o_ref.dtype)


def _zeros(m, d, tm):
    return pl.pallas_call(
        _fill_kernel,
        grid=(m // tm,),
        out_specs=pl.BlockSpec((tm, d), lambda i: (i, 0)),
        out_shape=jax.ShapeDtypeStruct((m, d), BF16),
        compiler_params=_params(("parallel",), 2 * tm * d * 2 // 2 ** 20 + 4),
        name="fill",
    )()


def _mlp_ln_kernel(y_ref, w1_ref, w2_ref, g_ref, b_ref, o_ref, o16_ref, xb_ref, *, rows):
    f = pl.program_id(1)

    @pl.when(f == 0)
    def _():
        xb_ref[...] = y_ref[...].astype(BF16)
        o_ref[...] = jnp.zeros_like(o_ref)

    h = jnp.maximum(_dot(xb_ref[...], w1_ref[0].astype(BF16)), 0.0)
    o_ref[...] += _dot((h * h).astype(BF16), w2_ref[0].astype(BF16))

    @pl.when(f == pl.num_programs(1) - 1)
    def _():
        def norm_rows(r, carry):
            sl = pl.ds(pl.multiple_of(r * rows, rows), rows)
            out = _layer_norm(ALPHA * y_ref[sl, :] + o_ref[sl, :], g_ref[...], b_ref[...])
            o_ref[sl, :] = out
            o16_ref[sl, :] = out.astype(BF16)
            return carry

        lax.fori_loop(0, o_ref.shape[0] // rows, norm_rows, 0)


def _mlp_ln(y, w1, w2, layer, g, b, tm, tf):
    m, d = y.shape
    f = w1.shape[2]
    vmem = (3 * tm * d * 4 + 2 * tm * d * 2 + 4 * d * tf * 4 + 2 * d * tf * 2 + 3 * tm * tf * 4) // 2 ** 20 + 12
    once = pl.Buffered(1)
    return pl.pallas_call(
        functools.partial(_mlp_ln_kernel, rows=tm // 4),
        grid=(m // tm, f // tf),
        in_specs=[pl.BlockSpec((tm, d), lambda i, ff: (i, 0)),
                  pl.BlockSpec((1, d, tf), lambda i, ff: (layer, 0, ff)),
                  pl.BlockSpec((1, tf, d), lambda i, ff: (layer, ff, 0)),
                  pl.BlockSpec((1, d), lambda i, ff: (0, 0)),
                  pl.BlockSpec((1, d), lambda i, ff: (0, 0))],
        out_specs=[pl.BlockSpec((tm, d), lambda i, ff: (i, 0), pipeline_mode=once),
                   pl.BlockSpec((tm, d), lambda i, ff: (i, 0), pipeline_mode=once)],
        out_shape=[jax.ShapeDtypeStruct((m, d), F32), jax.ShapeDtypeStruct((m, d), BF16)],
        scratch_shapes=[pltpu.VMEM((tm, d), BF16)],
        compiler_params=_params(("parallel", "arbitrary"), vmem),
        name="mlp_ln",
    )(y, w1, w2, g.reshape(1, d), b.reshape(1, d))


A_QBLK = 512
A_PAIR = 2 * CHUNK
A_WIN = (LEFT_CHUNKS + 2) * CHUNK
A_UW = A_WIN + A_PAIR


def _softmax_pv(s2, v):
    m = jnp.max(s2, axis=-1, keepdims=True)
    p = jnp.exp2(s2 - m)
    l = jnp.sum(p, axis=-1, keepdims=True)
    return _dot(p.astype(BF16), v) / l


def _bias_table(u_row, band, rows):
    u = jnp.broadcast_to(u_row * LOG2E, (rows, A_UW))
    return pltpu.roll(u, A_UW - A_PAIR + 1, 1, stride=1, stride_axis=0)[:, :band.shape[1]] + band


def _attn_a_prompt_kernel(q_ref, kp_ref, kc_ref, vp_ref, vc_ref, u_ref, band_ref, o_hbm_ref, o_ref, ko_ref, vo_ref,
                          kw_ref, vw_ref):
    del o_hbm_ref
    i = pl.program_id(2)

    @pl.when(i == pl.num_programs(2) - 1)
    def _():
        ko_ref[...] = kc_ref[...]
        vo_ref[...] = vc_ref[...]

    kw_ref[0:A_QBLK, :] = kp_ref[...].astype(BF16)
    kw_ref[A_QBLK:2 * A_QBLK, :] = kc_ref[...].astype(BF16)
    vw_ref[0:A_QBLK, :] = vp_ref[...].astype(BF16)
    vw_ref[A_QBLK:2 * A_QBLK, :] = vc_ref[...].astype(BF16)
    bias = _bias_table(u_ref[0], band_ref[...], A_PAIR)
    col = lax.broadcasted_iota(jnp.int32, (A_PAIR, A_WIN), 1)
    pairs = range(A_QBLK // A_PAIR)
    q = [(q_ref[c * A_PAIR:(c + 1) * A_PAIR, :] * (DH_A ** -0.5 * LOG2E)).astype(BF16) for c in pairs]
    s = [_dot_nt(q[c], kw_ref[c * A_PAIR:c * A_PAIR + A_WIN, :]) + bias for c in pairs]
    s = [jnp.where(col + (i * A_QBLK - A_QBLK + c * A_PAIR) >= 0, s[c], NEG_INF) for c in pairs]
    o = [_softmax_pv(s[c], vw_ref[c * A_PAIR:c * A_PAIR + A_WIN, :]).astype(BF16) for c in pairs]
    for c in pairs:
        o_ref[c * A_PAIR:(c + 1) * A_PAIR, :] = o[c]


def _attn_a_prompt(qkv, u_rows, band, o_all, n_batch, t):
    nblk = t // A_QBLK

    def spec(col0, prev):
        if prev:
            return pl.BlockSpec((A_QBLK, DH_A), lambda b, h, i: (b * nblk + jnp.maximum(i - 1, 0), col0 + h))
        return pl.BlockSpec((A_QBLK, DH_A), lambda b, h, i: (b * nblk + i, col0 + h))

    tail = pl.BlockSpec((A_QBLK, DH_A), lambda b, h, i: (b, h))
    return pl.pallas_call(
        _attn_a_prompt_kernel,
        grid=(n_batch, H_A, nblk),
        in_specs=[spec(0, False), spec(H_A, True), spec(H_A, False), spec(2 * H_A, True), spec(2 * H_A, False),
                  pl.BlockSpec((1, 1, A_UW), lambda b, h, i: (h, 0, 0)),
                  pl.BlockSpec((A_PAIR, A_WIN), lambda b, h, i: (0, 0)),
                  pl.BlockSpec(memory_space=pl.ANY)],
        out_specs=[pl.BlockSpec((A_QBLK, DH_A), lambda b, h, i: (b * nblk + i, h)), tail, tail],
        out_shape=[jax.ShapeDtypeStruct(o_all.shape, BF16),
                   jax.ShapeDtypeStruct((n_batch * A_QBLK, D_MODEL), F32),
                   jax.ShapeDtypeStruct((n_batch * A_QBLK, D_MODEL), F32)],
        input_output_aliases={7: 0},
        scratch_shapes=[pltpu.VMEM((2 * A_QBLK, DH_A), BF16), pltpu.VMEM((2 * A_QBLK, DH_A), BF16)],
        compiler_params=_params(("parallel", "parallel", "arbitrary"), 16),
        name="attn_a_prompt",
    )(qkv, qkv, qkv, qkv, qkv, u_rows, band, o_all)


def _attn_a_sample_kernel(q_ref, kc_ref, kn_ref, vc_ref, vn_ref, u_ref, band_ref, o_hbm_ref, o_ref, ko_ref, vo_ref,
                          *, past, s_len):
    del o_hbm_ref
    ko_ref[...] = kn_ref[...]
    vo_ref[...] = vn_ref[...]
    band = band_ref[...]
    zero = jnp.zeros((band.shape[1] - past - s_len, DH_A), BF16)
    heads = range(H_A)
    cols = [slice(h * DH_A, (h + 1) * DH_A) for h in heads]

    def window(old_ref, new_ref, h):
        return jnp.concatenate([old_ref[:, cols[h]].astype(BF16), new_ref[:, cols[h]].astype(BF16), zero], axis=0)

    q = [(q_ref[:, cols[h]] * (DH_A ** -0.5 * LOG2E)).astype(BF16) for h in heads]
    s = [_dot_nt(q[h], window(kc_ref, kn_ref, h)) + _bias_table(u_ref[h], band, s_len) for h in heads]
    o = [_softmax_pv(s[h], window(vc_ref, vn_ref, h)).astype(BF16) for h in heads]
    for h in heads:
        o_ref[:, cols[h]] = o[h]


def _attn_a_sample(qkv, k_cache, v_cache, u_rows, band, o_all, layer, past, row0, n_batch, s_len):
    win = band.shape[1]
    rb = row0 // s_len
    new = lambda colblk: pl.BlockSpec((s_len, D_MODEL), lambda b: (rb + b, colblk))
    old = pl.BlockSpec((past, D_MODEL), lambda b: (layer * n_batch + b, 0))
    rows = pl.BlockSpec((s_len, D_MODEL), lambda b: (b, 0))
    return pl.pallas_call(
        functools.partial(_attn_a_sample_kernel, past=past, s_len=s_len),
        grid=(n_batch,),
        in_specs=[new(0), old, new(1), old, new(2),
                  pl.BlockSpec((H_A, 1, A_UW), lambda b: (0, 0, 0)),
                  pl.BlockSpec((s_len, win), lambda b: (0, 0)),
                  pl.BlockSpec(memory_space=pl.ANY)],
        out_specs=[pl.BlockSpec((s_len, D_MODEL), lambda b: (rb + b, 0)), rows, rows],
        out_shape=[jax.ShapeDtypeStruct(o_all.shape, BF16),
                   jax.ShapeDtypeStruct((n_batch * s_len, D_MODEL), F32),
                   jax.ShapeDtypeStruct((n_batch * s_len, D_MODEL), F32)],
        input_output_aliases={7: 0},
        compiler_params=_params(("parallel",), 32),
        name="attn_a_sample",
    )(qkv, k_cache, qkv, v_cache, qkv, u_rows, band, o_all)


def _a_bias_rows(rel_bias, q0):
    d = q0 + A_PAIR - 1 - np.arange(A_UW)
    return rel_bias[:, np.clip(d, -REL_CLIP, REL_CLIP) + REL_CLIP][:, None, :]


def _a_band_prompt():
    qc = LEFT_CHUNKS * CHUNK + np.arange(A_PAIR)[:, None]
    kc = np.arange(A_WIN)[None, :]
    vis = (kc // CHUNK <= qc // CHUNK) & (kc // CHUNK >= qc // CHUNK - LEFT_CHUNKS)
    return jnp.asarray(np.where(vis, 0.0, -np.inf).astype(np.float32))


def _a_band_sample(past, s_len, win):
    q_pos = PAST_LEN + np.arange(s_len)[:, None]
    k_pos = PAST_LEN - past + np.arange(win)[None, :]
    vis = ((k_pos >= 0) & (k_pos // CHUNK <= q_pos // CHUNK) & (k_pos // CHUNK >= q_pos // CHUNK - LEFT_CHUNKS)
           & (np.arange(win)[None, :] < past + s_len))
    return jnp.asarray(np.where(vis, 0.0, -np.inf).astype(np.float32))


B_XW = 4 * LANES


def _key_to_float(key):
    return lax.bitcast_convert_type(jnp.where(key >= 0, key, key ^ jnp.int32(0x7FFFFFFF)), F32)


KEY_NEG_INF = int(np.array(-np.inf, np.float32).view(np.int32)) ^ 0x7FFFFFFF


def _lane_fold(x):
    acc = x[:, 0:LANES]
    for t in range(1, x.shape[1] // LANES):
        acc = acc + x[:, t * LANES:(t + 1) * LANES]
    return acc


def _dsa_kernel(q_ref, qi_ref, wi_ref, k_ref, v_ref, ka_ref, kb_ref, *refs,
                nq, kblk, n_keys, q_pos0, q_step, n_sel, idx_bits):
    o_ref, sc_ref, msk_ref, qs_ref, m_ref, l_ref, acc_ref = refs[-7:]
    qbase = q_pos0 + pl.program_id(1) * q_step
    n_adm = jnp.minimum(((qbase + nq - 1) // CHUNK + 1) * CHUNK, n_keys)
    nb = (n_adm + kblk - 1) // kblk
    qpos = qbase + lax.broadcasted_iota(jnp.int32, (nq, kblk), 0)
    col0 = lax.broadcasted_iota(jnp.int32, (nq, kblk), 1)

    wsc = wi_ref[...] * (H_IDX ** -0.5 * D_IDX ** -0.5)

    def score_block(kb, carry):
        r = pl.multiple_of(kb * kblk, kblk)
        ka = ka_ref[pl.ds(r, kblk), :]
        kz = kb_ref[pl.ds(r, kblk), :]
        acc = jnp.zeros((nq, kblk), F32)
        for p in range(H_IDX // 2):
            q2 = qi_ref[:, p * LANES:(p + 1) * LANES]
            acc = acc + wsc[:, 2 * p:2 * p + 1] * jnp.maximum(_dot_nt(q2, ka), 0.0)
            acc = acc + wsc[:, 2 * p + 1:2 * p + 2] * jnp.maximum(_dot_nt(q2, kz), 0.0)
        col = col0 + kb * kblk
        adm = (col // CHUNK <= qpos // CHUNK) & (col < n_keys)
        sc_ref[kb] = jnp.where(adm, acc, NEG_INF)
        return carry

    lax.fori_loop(0, nb, score_block, 0)

    def count(pred_fn):
        def body(kb, c):
            return c + _lane_fold(jnp.where(pred_fn(kb, sc_ref[kb]), 1.0, 0.0))
        c = lax.fori_loop(0, nb, body, jnp.zeros((nq, LANES), F32))
        return jnp.sum(c, axis=-1, keepdims=True)

    def bit_step(bi, pre):
        cand = pre | jnp.left_shift(jnp.int32(1), 31 - bi)
        cand_s = cand ^ jnp.int32(INT_MIN)
        thr_c = _key_to_float(cand_s)
        cnt = count(lambda kb, sc: sc >= thr_c)
        return jnp.where((cnt >= n_sel) | (cand_s <= KEY_NEG_INF), cand, pre)

    pre = lax.fori_loop(0, 32, bit_step, jnp.zeros((nq, 1), jnp.int32))
    thr = _key_to_float(pre ^ jnp.int32(INT_MIN))
    need = n_sel - count(lambda kb, sc: sc > thr)
    n_ge = count(lambda kb, sc: sc >= thr)

    def idx_step(bi, lim):
        cand = lim | jnp.left_shift(jnp.int32(1), idx_bits - 1 - bi)
        cnt = count(lambda kb, sc: (sc == thr) & (col0 + kb * kblk < cand))
        return jnp.where(cnt < need, cand, lim)

    tied = jnp.max(jnp.where(thr > NEG_INF, n_ge, 0.0)) > n_sel
    lim = lax.cond(tied,
                   lambda: lax.fori_loop(0, idx_bits, idx_step, jnp.zeros((nq, 1), jnp.int32)),
                   lambda: jnp.full((nq, 1), 2 ** idx_bits, jnp.int32))

    def mask_block(kb, carry):
        sc = sc_ref[kb]
        sel = (sc > thr) | ((sc == thr) & (col0 + kb * kblk <= lim))
        sel = sel & (sc > NEG_INF) & (sc < float("inf"))
        msk_ref[kb] = jnp.where(sel, 0.0, NEG_INF)
        return carry

    lax.fori_loop(0, nb, mask_block, 0)

    group = H_B // KV_B
    gq = group * nq
    for h in range(H_B):
        qs_ref[h * nq:(h + 1) * nq, :] = q_ref[:, h * DH_B:(h + 1) * DH_B]
    m_ref[...] = jnp.full(m_ref.shape, NEG_INF, F32)
    l_ref[...] = jnp.zeros(l_ref.shape, F32)
    acc_ref[...] = jnp.zeros(acc_ref.shape, F32)

    def attend(kb, carry):
        r = pl.multiple_of(kb * kblk, kblk)
        msk = msk_ref[kb]
        s = [_dot_nt(qs_ref[g * gq:(g + 1) * gq, :], k_ref[pl.ds(r, kblk), g * DH_B:(g + 1) * DH_B])
             for g in range(KV_B)]
        for g in range(KV_B):
            ps = []
            for hq in range(group):
                rows = slice((g * group + hq) * nq, (g * group + hq + 1) * nq)
                sh = s[g][hq * nq:(hq + 1) * nq] + msk
                m_old = m_ref[rows]
                m_new = jnp.maximum(m_old, jnp.max(sh, axis=-1, keepdims=True))
                m_use = jnp.where(m_new == NEG_INF, 0.0, m_new)
                p = jnp.exp2(sh - m_use)
                a = jnp.exp2(m_old - m_use)
                m_ref[rows] = m_new
                l_ref[rows] = a * l_ref[rows] + jnp.sum(p, axis=-1, keepdims=True)
                acc_ref[rows] = a * acc_ref[rows]
                ps.append(p.astype(BF16))
            acc_ref[g * gq:(g + 1) * gq, :] += _dot(jnp.concatenate(ps, axis=0),
                                                    v_ref[pl.ds(r, kblk), g * DH_B:(g + 1) * DH_B])
        return carry

    lax.fori_loop(0, nb, attend, 0)
    for h in range(H_B):
        rows = slice(h * nq, (h + 1) * nq)
        o_ref[:, h * DH_B:(h + 1) * DH_B] = (acc_ref[rows] / l_ref[rows]).astype(BF16)


def _dsa(qq, x32, k, v, ka, kz, o_all, *, ka_col, kz_col, row0, n_batch, lp, nq, n_qblk, kblk, n_keys,
         q_pos0, q_step):
    nkb = lp // kblk
    n_sel = min(TOPK_MAX, n_keys // 4)
    rb = row0 // nq
    qspec = lambda width, colblk: pl.BlockSpec((nq, width), lambda b, i: (rb + b * n_qblk + i, colblk))
    kspec = lambda width, colblk: pl.BlockSpec((lp, width), lambda b, i: (b, colblk))
    kern = functools.partial(_dsa_kernel, nq=nq, kblk=kblk, n_keys=n_keys, q_pos0=q_pos0, q_step=q_step,
                             n_sel=n_sel, idx_bits=max(1, int(np.ceil(np.log2(lp)))))
    vmem = (2 * 2 * lp * (2 * KV_B * DH_B + 2 * LANES) + 2 * nkb * nq * kblk * 4
            + H_B * nq * (DH_B * 6 + 2 * LANES * 4) + 4 * nq * (qq.shape[1] + D_MODEL) * 2) // 2 ** 20 + 8
    operands = [qq, qq, x32, k, v, ka, kz]
    in_specs = [qspec(H_B * DH_B, 0), qspec(H_IDX * D_IDX, H_B * DH_B // (H_IDX * D_IDX)), qspec(LANES, 2),
                kspec(KV_B * DH_B, 0), kspec(KV_B * DH_B, 0), kspec(LANES, ka_col), kspec(LANES, kz_col)]
    aliases = {}
    if o_all is not None:
        operands.append(o_all)
        in_specs.append(pl.BlockSpec(memory_space=pl.ANY))
        aliases = {len(operands) - 1: 0}
    return pl.pallas_call(
        kern,
        grid=(n_batch, n_qblk),
        in_specs=in_specs,
        out_specs=pl.BlockSpec((nq, D_MODEL), lambda b, i: (rb + b * n_qblk + i, 0)),
        out_shape=jax.ShapeDtypeStruct((qq.shape[0], D_MODEL), BF16),
        input_output_aliases=aliases,
        scratch_shapes=[pltpu.VMEM((nkb, nq, kblk), F32),
                        pltpu.VMEM((nkb, nq, kblk), F32), pltpu.VMEM((H_B * nq, DH_B), BF16),
                        pltpu.VMEM((H_B * nq, 1), F32), pltpu.VMEM((H_B * nq, 1), F32),
                        pltpu.VMEM((H_B * nq, DH_B), F32)],
        compiler_params=_params(("parallel", "arbitrary"), vmem),
        name="dsa",
    )(*operands)


def _sample_keys_kernel(ck_ref, cv_ref, ci_ref, kn_ref, vn_ref, xn_ref, k_ref, v_ref, ka_ref, kz_ref,
                        *, past, s_len):
    end = past + s_len
    pad = k_ref.shape[0] - end
    k_ref[0:past, :] = ck_ref[...].astype(BF16)
    k_ref[past:end, :] = kn_ref[...]
    k_ref[end:, :] = jnp.zeros((pad, k_ref.shape[1]), BF16)
    v_ref[0:past, :] = cv_ref[...].astype(BF16)
    v_ref[past:end, :] = vn_ref[...]
    v_ref[end:, :] = jnp.zeros((pad, v_ref.shape[1]), BF16)
    ci = ci_ref[...]
    zero = jnp.zeros((past, LANES - D_IDX), F32)
    ka_ref[0:past, :] = jnp.concatenate([ci, zero], axis=1).astype(BF16)
    kz_ref[0:past, :] = jnp.concatenate([zero, ci], axis=1).astype(BF16)
    ka_ref[past:end, :] = xn_ref[:, 0:LANES]
    kz_ref[past:end, :] = xn_ref[:, LANES:2 * LANES]
    ka_ref[end:, :] = jnp.zeros((pad, LANES), BF16)
    kz_ref[end:, :] = jnp.zeros((pad, LANES), BF16)


def _sample_keys(cache_k, cache_v, cache_ki, k16, v16, x16, *, row0, n_batch, past, s_len, lp):
    rb = row0 // s_len
    nk = KV_B * DH_B
    old = lambda width: pl.BlockSpec((past, width), lambda b: (b, 0))
    new = lambda width: pl.BlockSpec((s_len, width), lambda b: (rb + b, 0))
    out = lambda width: pl.BlockSpec((lp, width), lambda b: (b, 0))
    return pl.pallas_call(
        functools.partial(_sample_keys_kernel, past=past, s_len=s_len),
        grid=(n_batch,),
        in_specs=[old(nk), old(nk), old(D_IDX), new(nk), new(nk), new(B_XW)],
        out_specs=[out(nk), out(nk), out(LANES), out(LANES)],
        out_shape=[jax.ShapeDtypeStruct((n_batch * lp, nk), BF16), jax.ShapeDtypeStruct((n_batch * lp, nk), BF16),
                   jax.ShapeDtypeStruct((n_batch * lp, LANES), BF16),
                   jax.ShapeDtypeStruct((n_batch * lp, LANES), BF16)],
        compiler_params=_params(("parallel",), 24),
        name="sample_keys",
    )(cache_k, cache_v, cache_ki, k16, v16, x16)


CONV_PW = 2 * DK_C


def _conv_kernel(u_ref, st_ref, w_ref, o_ref, ext_ref, *, t, cw, rows):
    j = pl.program_id(1)
    ext_ref[0:SUBLANES, :] = st_ref[0]
    ext_ref[SUBLANES:SUBLANES + t, :] = u_ref[...]
    off = SUBLANES - (CONV_W - 1)
    for p in range(cw // CONV_PW):
        cols = slice(p * CONV_PW, (p + 1) * CONV_PW)
        col0 = j * cw + p * CONV_PW
        scale = jnp.where(col0 < C_QK_DIM, DK_C ** -0.5, 1.0)
        w = w_ref[:, cols]

        def body(r, carry, cols=cols, col0=col0, scale=scale, w=w):
            base = pl.multiple_of(r * rows, rows)
            blk = ext_ref[pl.ds(base, rows + SUBLANES), cols]
            acc = blk[SUBLANES:, :] * w[CONV_W - 1:CONV_W, :]
            for jj in range(CONV_W - 1):
                tap = pltpu.roll(blk, SUBLANES - off - jj, 0)[SUBLANES:, :]
                acc = acc + tap * w[jj:jj + 1, :]
            a = acc * _sigmoid(acc)
            normed = []
            for hh in range(CONV_PW // DK_C):
                ah = a[:, hh * DK_C:(hh + 1) * DK_C]
                normed.append(ah * (lax.rsqrt(jnp.sum(ah * ah, axis=-1, keepdims=True) + RMS_EPS) * scale))
            o_ref[pl.ds(base, rows), cols] = jnp.where(col0 < 2 * C_QK_DIM, jnp.concatenate(normed, axis=-1), a)
            return carry

        lax.fori_loop(0, t // rows, body, 0)


def _conv(u_all, state8, conv_w, *, row0, n_batch, t, cw, rows):
    rb = row0 // t
    return pl.pallas_call(
        functools.partial(_conv_kernel, t=t, cw=cw, rows=rows),
        grid=(n_batch, C_CONV_DIM // cw),
        in_specs=[pl.BlockSpec((t, cw), lambda b, j: (rb + b, j)),
                  pl.BlockSpec((1, SUBLANES, cw), lambda b, j: (b, 0, j)),
                  pl.BlockSpec((CONV_W, cw), lambda b, j: (0, j))],
        out_specs=pl.BlockSpec((t, cw), lambda b, j: (b, j)),
        out_shape=jax.ShapeDtypeStruct((n_batch * t, C_CONV_DIM), F32),
        scratch_shapes=[pltpu.VMEM((t + SUBLANES, cw), F32)],
        compiler_params=_params(("parallel", "parallel"), 6 * t * cw * 4 // 2 ** 20 + 8),
        name="conv",
    )(u_all, state8, conv_w)


def _gates_kernel(a_ref, at_ref, b_ref, alog_ref, alogt_ref, dt_ref, dtt_ref, gc_ref, gr_ref, beta_ref, *, c, tb):
    def decay(a_raw, a_log, dt):
        x = a_raw + dt
        softplus = jnp.maximum(x, 0.0) + jnp.log1p(jnp.exp(-jnp.abs(x)))
        return -jnp.exp(a_log) * softplus

    def split3(x):
        p0 = x.astype(BF16)
        r = x - p0.astype(F32)
        p1 = r.astype(BF16)
        p2 = (r - p1.astype(F32)).astype(BF16)
        return p0, p1, p2

    i = lax.broadcasted_iota(jnp.int32, (tb, tb), 0)
    j = lax.broadcasted_iota(jnp.int32, (tb, tb), 1)
    same = (i // c) == (j // c)
    lower = jnp.where(same & (j <= i), 1.0, 0.0).astype(BF16)
    upper = jnp.where(same & (i <= j), 1.0, 0.0).astype(BF16)
    g = decay(a_ref[...], alog_ref[...], dt_ref[...])
    gt = decay(at_ref[...], alogt_ref[...], dtt_ref[...])
    g0, g1, g2 = split3(g)
    gc_ref[...] = _dot(lower, g0) + (_dot(lower, g1) + _dot(lower, g2))
    t0, t1, t2 = split3(gt)
    gr_ref[...] = _dot(t0, upper) + (_dot(t1, upper) + _dot(t2, upper))
    beta_ref[...] = _sigmoid(b_ref[...])


def _gates(a_raw, b_raw, a_log, dt_bias, c, tb):
    n = a_raw.shape[0]
    tok = pl.BlockSpec((tb, HV_C), lambda i: (i, 0))
    hed = pl.BlockSpec((HV_C, tb), lambda i: (0, i))
    row = pl.BlockSpec((1, HV_C), lambda i: (0, 0))
    colv = pl.BlockSpec((HV_C, 1), lambda i: (0, 0))
    return pl.pallas_call(
        functools.partial(_gates_kernel, c=c, tb=tb),
        grid=(n // tb,),
        in_specs=[tok, hed, tok, row, colv, row, colv],
        out_specs=[tok, hed, tok],
        out_shape=[jax.ShapeDtypeStruct((n, HV_C), F32), jax.ShapeDtypeStruct((HV_C, n), F32),
                   jax.ShapeDtypeStruct((n, HV_C), F32)],
        compiler_params=_params(("parallel",), 16),
        name="gates",
    )(a_raw, a_raw.T, b_raw, a_log.reshape(1, HV_C), a_log.reshape(HV_C, 1),
      dt_bias.reshape(1, HV_C), dt_bias.reshape(HV_C, 1))


def _delta_kernel(q_ref, k_ref, v_ref, z_ref, gc_ref, gr_ref, beta_ref, s0_ref, nw_ref, *refs, c, hg):
    o_ref, s_ref = refs[-2:]

    @pl.when(pl.program_id(2) == 0)
    def _():
        s_ref[...] = s0_ref[...]

    assert HV_C // HK_C == 2
    c2 = 2 * c
    i = lax.broadcasted_iota(jnp.int32, (c2, c2), 0)
    j = lax.broadcasted_iota(jnp.int32, (c2, c2), 1)
    eye = jnp.where(i == j, 1.0, 0.0)
    same = (i // c) == (j // c)
    top = lax.broadcasted_iota(jnp.int32, (c2, 1), 0) < c
    pairs = range(hg // 2)
    stack = lambda a, b: jnp.concatenate([a, b], axis=0)

    def per_head(x):
        return jnp.concatenate([jnp.where(top, x, 0.0), jnp.where(top, 0.0, x)], axis=1)

    gc_all, gr_all, beta_all = gc_ref[0, 0], gr_ref[0, 0], beta_ref[0, 0]
    q1 = [q_ref[:, n * DK_C:(n + 1) * DK_C] for n in pairs]
    k1 = [k_ref[:, n * DK_C:(n + 1) * DK_C] for n in pairs]
    q = [stack(q1[n], q1[n]) for n in pairs]
    k = [stack(k1[n], k1[n]) for n in pairs]
    v = [stack(v_ref[:, 2 * n * DV_C:(2 * n + 1) * DV_C], v_ref[:, (2 * n + 1) * DV_C:(2 * n + 2) * DV_C]) for n in pairs]
    z = [stack(z_ref[:, 2 * n * DV_C:(2 * n + 1) * DV_C], z_ref[:, (2 * n + 1) * DV_C:(2 * n + 2) * DV_C]) for n in pairs]
    s_old = [(s_ref[2 * n], s_ref[2 * n + 1]) for n in pairs]
    gcol = [stack(gc_all[:, 2 * n:2 * n + 1], gc_all[:, 2 * n + 1:2 * n + 2]) for n in pairs]
    grow = [jnp.concatenate([gr_all[2 * n:2 * n + 1, :], gr_all[2 * n + 1:2 * n + 2, :]], axis=1) for n in pairs]
    beta = [stack(beta_all[:, 2 * n:2 * n + 1], beta_all[:, 2 * n + 1:2 * n + 2]) for n in pairs]
    d_incl = [jnp.exp(jnp.where(same & (i >= j), gcol[n] - grow[n], NEG_INF)) for n in pairs]
    kh = [k[n].astype(BF16) for n in pairs]
    m = [beta[n] * _dot_nt(kh[n], kh[n]) * jnp.where(i > j, d_incl[n], 0.0) for n in pairs]
    t = [eye - jnp.where(i // 2 == j // 2, m[n], 0.0) for n in pairs]
    s = 2
    while s < c:
        join = (i // (2 * s) == j // (2 * s)) & (i // s != j // s)
        tb = [t[n].astype(BF16) for n in pairs]
        x = [_dot(tb[n], jnp.where(join, m[n], 0.0).astype(BF16)) for n in pairs]
        t = [t[n] - _dot(x[n].astype(BF16), tb[n]) for n in pairs]
        s *= 2
    e_g = [jnp.exp(gcol[n]) for n in pairs]
    sol = [_dot(t[n].astype(BF16),
                jnp.concatenate([v[n] * beta[n], k[n] * (beta[n] * e_g[n])], axis=-1).astype(BF16)) for n in pairs]
    s_bf = [stack(s_old[n][0], s_old[n][1]).astype(BF16) for n in pairs]
    u = [sol[n][:, :DV_C] - _dot(per_head(sol[n][:, DV_C:]).astype(BF16), s_bf[n]) for n in pairs]
    ub = [u[n].astype(BF16) for n in pairs]
    qk = [_dot_nt(q[n].astype(BF16), kh[n]) * d_incl[n] for n in pairs]
    o = [_dot(jnp.concatenate([per_head(q[n] * e_g[n]), qk[n]], axis=1).astype(BF16), stack(s_bf[n], ub[n]))
         for n in pairs]
    g_last = [stack(jnp.broadcast_to(gcol[n][c - 1:c, :], (c, 1)), jnp.broadcast_to(gcol[n][c2 - 1:c2, :], (c, 1)))
              for n in pairs]
    ds = [_dot_tn((k[n] * jnp.exp(g_last[n] - gcol[n])).astype(BF16), per_head(u[n]).astype(BF16)) for n in pairs]
    s_new = [(s_old[n][0] * jnp.exp(gcol[n][c - 1:c, :]) + ds[n][:, :DV_C],
              s_old[n][1] * jnp.exp(gcol[n][c2 - 1:c2, :]) + ds[n][:, DV_C:]) for n in pairs]
    o = [o[n] * lax.rsqrt(jnp.mean(o[n] * o[n], axis=-1, keepdims=True) + RMS_EPS) * nw_ref[...] for n in pairs]
    o = [(o[n] * (z[n] * _sigmoid(z[n]))).astype(BF16) for n in pairs]
    for n in pairs:
        s_ref[2 * n] = s_new[n][0]
        s_ref[2 * n + 1] = s_new[n][1]
        o_ref[:, 2 * n * DV_C:(2 * n + 1) * DV_C] = o[n][:c]
        o_ref[:, (2 * n + 1) * DV_C:(2 * n + 2) * DV_C] = o[n][c:]


def _delta(qkv_act, z_all, gc, gr, beta, s0, norm_w, o_all, *, row0, z_col0, n_batch, t, c, hg):
    nc = t // c
    nhg = HV_C // hg
    rep = HV_C // HK_C
    qw = hg // rep * DK_C
    rb = row0 // c
    vmem = 24
    operands = [qkv_act, qkv_act, qkv_act, z_all, gc, gr, beta, s0, norm_w.reshape(1, DV_C)]
    in_specs = [pl.BlockSpec((c, qw), lambda b, g, ci: (b * nc + ci, g)),
                pl.BlockSpec((c, qw), lambda b, g, ci: (b * nc + ci, C_QK_DIM // qw + g)),
                pl.BlockSpec((c, hg * DV_C), lambda b, g, ci: (b * nc + ci, 2 * C_QK_DIM // (hg * DV_C) + g)),
                pl.BlockSpec((c, hg * DV_C), lambda b, g, ci: (rb + b * nc + ci, z_col0 // (hg * DV_C) + g)),
                pl.BlockSpec((1, 1, c, hg), lambda b, g, ci: (b * nc + ci, g, 0, 0)),
                pl.BlockSpec((1, 1, hg, c), lambda b, g, ci: (b * nc + ci, g, 0, 0)),
                pl.BlockSpec((1, 1, c, hg), lambda b, g, ci: (b * nc + ci, g, 0, 0)),
                pl.BlockSpec((hg, DK_C, DV_C), lambda b, g, ci: (b * nhg + g, 0, 0)),
                pl.BlockSpec((1, DV_C), lambda b, g, ci: (0, 0))]
    aliases = {}
    if o_all is not None:
        operands.append(o_all)
        in_specs.append(pl.BlockSpec(memory_space=pl.ANY))
        aliases = {len(operands) - 1: 0}
    return pl.pallas_call(
        functools.partial(_delta_kernel, c=c, hg=hg),
        grid=(n_batch, nhg, nc),
        in_specs=in_specs,
        out_specs=[pl.BlockSpec((c, hg * DV_C), lambda b, g, ci: (rb + b * nc + ci, g)),
                   pl.BlockSpec((hg, DK_C, DV_C), lambda b, g, ci: (b * nhg + g, 0, 0))],
        out_shape=[jax.ShapeDtypeStruct((z_all.shape[0], C_V_DIM), BF16),
                   jax.ShapeDtypeStruct((n_batch * HV_C, DK_C, DV_C), F32)],
        input_output_aliases=aliases,
        compiler_params=_params(("parallel", "parallel", "arbitrary"), vmem),
        name="delta",
    )(*operands)


def _chunk_layout(x, n_chunks, c, nhg, hg, head_major):
    if head_major:
        return x.reshape(nhg, hg, n_chunks, c).transpose(2, 0, 1, 3)
    return x.reshape(n_chunks, c, nhg, hg).transpose(0, 2, 1, 3)


TM = 512
TM_MLP = 1088
TF_MLP = 512
C_HG = 32
B_QBLK = 256
B_KBLK = 512
B_KBLK_S = 384


def _pad_cols(w, width):
    return jnp.pad(w, ((0, 0), (0, width - w.shape[1])))


def kernel(x_prompt, x_sample, cache_a_k, cache_a_v, cache_b_k, cache_b_v, cache_b_kidx, state_c_conv,
           state_c_ssm, a_w_in, a_rel_bias, a_w_out, b_w_in, b_w_out, c_w_in, c_conv_w, c_a_log, c_dt_bias,
           c_norm_w, c_w_out, ln1_g, ln1_b, mlp_w1, mlp_w2, ln2_g, ln2_b):
    nb_p, t_p, _ = x_prompt.shape
    nb_s, t_s, _ = x_sample.shape
    mp = nb_p * t_p
    ms = nb_s * t_s
    y = _join_rows(x_prompt.reshape(mp, D_MODEL), x_sample.reshape(ms, D_MODEL), TM)
    xin = y
    outs = {n: [] for n in ("akp", "avp", "aks", "avs", "bkp", "bvp", "bip", "bks", "bvs", "bis",
                            "ccp", "csp", "ccs", "css")}
    for i in range(DEPTH):
        kind, j = i % N_MIXERS, i // N_MIXERS
        if kind == 0:
            assert t_p % A_QBLK == 0 and t_p >= LEFT_CHUNKS * CHUNK == A_QBLK
            qkv = _mm(xin, a_w_in, j, TM, 1536, 3 * D_MODEL // 1536)
            past = cache_a_k.shape[2]
            win = -(-(past + t_s) // LANES) * LANES
            mix_in, k_tail, v_tail = _attn_a_prompt(qkv, _a_bias_rows(a_rel_bias[j], LEFT_CHUNKS * CHUNK),
                                                    _a_band_prompt(), _zeros(mp + ms, D_MODEL, TM), nb_p, t_p)
            mix_in, k_new, v_new = _attn_a_sample(qkv, cache_a_k.reshape(-1, D_MODEL), cache_a_v.reshape(-1, D_MODEL),
                                                  _a_bias_rows(a_rel_bias[j], past), _a_band_sample(past, t_s, win),
                                                  mix_in, j, past, mp, nb_s, t_s)
            outs["akp"].append(k_tail.reshape(nb_p, A_QBLK, H_A, DH_A))
            outs["avp"].append(v_tail.reshape(nb_p, A_QBLK, H_A, DH_A))
            outs["aks"].append(k_new.reshape(nb_s, t_s, H_A, DH_A))
            outs["avs"].append(v_new.reshape(nb_s, t_s, H_A, DH_A))
            w_out = a_w_out
        elif kind == 1:
            nq_, nk_, ni_ = H_B * DH_B, KV_B * DH_B, H_IDX * D_IDX
            w = b_w_in[j]
            w_ki = w[:, nq_ + 2 * nk_ + ni_:nq_ + 2 * nk_ + ni_ + D_IDX]
            w_wi = w[:, nq_ + 2 * nk_ + ni_ + D_IDX:]
            zc = lambda n: jnp.zeros((D_MODEL, n), F32)
            w_x = jnp.concatenate([w_ki, zc(LANES - D_IDX), zc(LANES - D_IDX), w_ki,
                                   w_wi, zc(2 * LANES - H_IDX)], axis=1)
            assert nq_ == 2 * ni_ and 2 * nk_ == ni_
            qq = _mm(xin, b_w_in, j, TM, ni_, 3, (BF16,), col_of=lambda t: jnp.where(t < 2, t, t + 1),
                     n_scaled=2, scale=DH_B ** -0.5 * LOG2E)
            k32, k16 = _mm(xin, b_w_in, j, TM, nk_, 1, (F32, BF16), col_of=lambda t: nq_ // nk_)
            v32, v16 = _mm(xin, b_w_in, j, TM, nk_, 1, (F32, BF16), col_of=lambda t: nq_ // nk_ + 1)
            x32, x16 = _mm(xin, w_x[None], 0, TM, B_XW, 1, (F32, BF16))
            k_p, k_s = _split_rows(k32, mp, TM)
            v_p, v_s = _split_rows(v32, mp, TM)
            outs["bkp"].append(k_p.reshape(nb_p, t_p, KV_B, DH_B))
            outs["bvp"].append(v_p.reshape(nb_p, t_p, KV_B, DH_B))
            outs["bip"].append(x32[:mp, :D_IDX].reshape(nb_p, t_p, D_IDX))
            outs["bks"].append(k_s.reshape(nb_s, t_s, KV_B, DH_B))
            outs["bvs"].append(v_s.reshape(nb_s, t_s, KV_B, DH_B))
            outs["bis"].append(x32[mp:, :D_IDX].reshape(nb_s, t_s, D_IDX))
            mix_in = _dsa(qq, x32, k16, v16, x16, x16, _zeros(mp + ms, D_MODEL, TM), ka_col=0, kz_col=1, row0=0,
                          n_batch=nb_p, lp=t_p, nq=B_QBLK, n_qblk=t_p // B_QBLK, kblk=B_KBLK, n_keys=t_p,
                          q_pos0=0, q_step=B_QBLK)
            past = cache_b_k.shape[2]
            n_keys = past + t_s
            lp = -(-n_keys // B_KBLK_S) * B_KBLK_S
            k_s, v_s, ka_s, kz_s = _sample_keys(
                cache_b_k[j].reshape(nb_s * past, nk_), cache_b_v[j].reshape(nb_s * past, nk_),
                cache_b_kidx[j].reshape(nb_s * past, D_IDX), k16, v16, x16,
                row0=mp, n_batch=nb_s, past=past, s_len=t_s, lp=lp)
            mix_in = _dsa(qq, x32, k_s, v_s, ka_s, kz_s, mix_in, ka_col=0, kz_col=0, row0=mp, n_batch=nb_s, lp=lp,
                          nq=t_s, n_qblk=1, kblk=B_KBLK_S, n_keys=n_keys, q_pos0=PAST_LEN, q_step=0)
            w_out = b_w_out
        else:
            n_main = C_CONV_DIM + C_V_DIM
            proj = _mm(xin, c_w_in, j, TM, 1536, n_main // 1536)
            gate_raw = _mm_x3(y, _pad_cols(c_w_in[j][:, n_main:], LANES), TM)
            b_raw, a_raw = gate_raw[:, :HV_C], gate_raw[:, HV_C:2 * HV_C]
            nhg = HV_C // C_HG
            mix_in = _zeros(mp + ms, C_V_DIM, TM)
            for (row0, nbt, tt, state, s0, names) in (
                    (0, nb_p, t_p, jnp.zeros((nb_p, CONV_W - 1, C_CONV_DIM), F32),
                     jnp.zeros((nb_p * HV_C, DK_C, DV_C), F32), ("ccp", "csp")),
                    (mp, nb_s, t_s, state_c_conv[j], state_c_ssm[j].reshape(nb_s * HV_C, DK_C, DV_C), ("ccs", "css"))):
                c = min(CHUNK, tt)
                n_rows = nbt * tt
                n_new = min(tt, CONV_W - 1)
                u_tail = jnp.stack([proj[row0 + (b + 1) * tt - n_new:row0 + (b + 1) * tt, :C_CONV_DIM]
                                    for b in range(nbt)])
                outs[names[0]].append(jnp.concatenate([state[:, n_new:], u_tail], axis=1))
                state8 = jnp.pad(state, ((0, 0), (SUBLANES - (CONV_W - 1), 0), (0, 0)))
                act = _conv(proj, state8, c_conv_w[j], row0=row0, n_batch=nbt, t=tt,
                            cw=CONV_PW if tt > 256 else C_CONV_DIM, rows=min(tt, 256))
                gc, gr, beta = _gates(a_raw[row0:row0 + n_rows], b_raw[row0:row0 + n_rows], c_a_log[j], c_dt_bias[j],
                                      c, 512)
                n_chunks = n_rows // c
                mix_in, s_new = _delta(act, proj, _chunk_layout(gc, n_chunks, c, nhg, C_HG, False),
                                       _chunk_layout(gr, n_chunks, c, nhg, C_HG, True),
                                       _chunk_layout(beta, n_chunks, c, nhg, C_HG, False), s0, c_norm_w[j], mix_in,
                                       row0=row0, z_col0=C_CONV_DIM, n_batch=nbt, t=tt, c=c, hg=C_HG)
                outs[names[1]].append(s_new.reshape(nbt, HV_C, DK_C, DV_C))
            w_out = c_w_out
        y = _mm_ln(mix_in, w_out.astype(BF16), j, y, ln1_g[i], ln1_b[i], TM, 1024)
        y, xin = _mlp_ln(y, mlp_w1, mlp_w2, i, ln2_g[i], ln2_b[i], TM_MLP, TF_MLP)
    st = lambda name: jnp.stack(outs[name])
    y_p, y_s = _split_rows(y, mp, TM)
    return (y_p.reshape(nb_p, t_p, D_MODEL), y_s.reshape(nb_s, t_s, D_MODEL),
            st("akp"), st("avp"), st("aks"), st("avs"),
            st("bkp"), st("bvp"), st("bip"), st("bks"), st("bvs"), st("bis"),
            st("ccp"), st("csp"), st("ccs"), st("css"))
```
